```python
import math
import jax, jax.numpy as jnp
from jax import lax
import numpy as np

D_MODEL = 2048
BATCH = 2
SEQ = 4096
DEPTH = 2
DEC_BATCH = 16
DEC_SEQ = 64
PAST_LEN = 1024

CHUNK = 64
N_MIXERS = 4
D_MIX = D_MODEL
GROUP_W = D_MIX // N_MIXERS
SSM_HEAD_DIM = 64
SSM_HEADS = GROUP_W // SSM_HEAD_DIM
SSM_GROUPS = 2
SSM_STATE = 128
SSM_CONV = 4
SSM_CONV_DIM = GROUP_W + 2 * SSM_GROUPS * SSM_STATE
HGRN_HEAD_DIM = 128
HGRN_HEADS = GROUP_W // HGRN_HEAD_DIM
HGRN_F_FLOOR = 1e-30
FOX_HEAD_DIM = 64
FOX_HEADS = GROUP_W // FOX_HEAD_DIM
FOX_Q_BLOCK = 128
FOX_F_BIAS_INIT = 2.0
LRU_BLOCKS = 4
LRU_BLOCK_W = GROUP_W // LRU_BLOCKS
LRU_CONV = 4
LRU_C = 8.0
D_FF = -(-8 * D_MODEL // (3 * 256)) * 256
EPS = 1e-6
IN_WIDTHS = (GROUP_W, SSM_CONV_DIM, SSM_HEADS,
             GROUP_W, GROUP_W, GROUP_W, GROUP_W,
             GROUP_W, GROUP_W, GROUP_W, FOX_HEADS, GROUP_W,
             GROUP_W, GROUP_W)
D_IN = sum(IN_WIDTHS)

kernel_name = 'hybrid_ssd_hgrn2_fox_rglru_stream_step'


def _normalize(x):
    xf = x.astype(jnp.float32)
    return (xf * lax.rsqrt(jnp.mean(xf * xf, axis=-1, keepdims=True) + EPS)).astype(x.dtype)


def _rms(x, g):
    return _normalize(x) * g.astype(x.dtype)


def _split_in(h):
    parts = []
    off = 0
    for w in IN_WIDTHS:
        parts.append(h[..., off:off + w])
        off += w
    return parts


def _causal_conv(x, buf, w, bias):
    K = w.shape[0]
    T = x.shape[1]
    xp = jnp.concatenate([buf.astype(x.dtype), x], axis=1)
    y = bias.astype(x.dtype)
    for j in range(K):
        y = y + xp[:, j:j + T] * w[j]
    return y, xp[:, T:]


def _chunk_recurrence(q, k, v, log_g, s0):
    b, T, H, dk = q.shape
    dv = v.shape[-1]
    L = CHUNK if T % CHUNK == 0 else math.gcd(T, CHUNK)
    nc = T // L

    def blocks(a):
        return jnp.moveaxis(a.astype(jnp.float32).reshape(b, nc, L, *a.shape[2:]), 1, 0)

    tri = jnp.arange(L)[:, None] >= jnp.arange(L)[None, :]
    scalar = log_g.shape[-1] == 1

    def step(S, blk):
        qc, kc, vc, gc = blk
        cum = jnp.cumsum(gc, axis=1)
        diff = jnp.where(tri[None, :, :, None, None], cum[:, :, None] - cum[:, None, :], -jnp.inf)
        dec = jnp.exp(diff)
        if scalar:
            att = jnp.einsum('bthd,bshd->btsh', qc, kc) * dec[..., 0]
        else:
            att = jnp.einsum('bthd,bshd,btshd->btsh', qc, kc, dec)
        o = (jnp.einsum('btsh,bshv->bthv', att, vc)
             + jnp.einsum('bthd,bhdv->bthv', qc * jnp.exp(cum), S))
        tot = cum[:, -1]
        k_dec = kc * jnp.exp(tot[:, None] - cum)
        S = jnp.exp(tot)[..., None] * S + jnp.einsum('bshd,bshv->bhdv', k_dec, vc)
        return S, o

    S, o = lax.scan(step, s0.astype(jnp.float32), (blocks(q), blocks(k), blocks(v), blocks(log_g)))
    o = jnp.moveaxis(o, 0, 1).reshape(b, T, H, dv)
    return o.astype(v.dtype), S.astype(s0.dtype)


def _fox_attention(q, k, v, cq, ck):
    b, Tq, H, dh = q.shape
    Tk = k.shape[1]
    q_off = Tk - Tq
    QB = FOX_Q_BLOCK if Tq % FOX_Q_BLOCK == 0 else Tq
    nb = Tq // QB
    kpos = jnp.arange(Tk)
    ckh = jnp.swapaxes(ck, 1, 2)

    def one(blk):
        qb, cqb, qpos = blk
        s = jnp.einsum('bqhd,bkhd->bhqk', qb, k).astype(jnp.float32)
        s = s + jnp.swapaxes(cqb, 1, 2)[..., None] - ckh[:, :, None, :]
        s = jnp.where(kpos[None, :] <= qpos[:, None], s, -jnp.inf)
        p = jax.nn.softmax(s, axis=-1)
        return jnp.einsum('bhqk,bkhd->bqhd', p.astype(v.dtype), v)

    qs = jnp.moveaxis(q.reshape(b, nb, QB, H, dh), 1, 0)
    cqs = jnp.moveaxis(cq.reshape(b, nb, QB, H), 1, 0)
    qpos = (q_off + jnp.arange(Tq)).reshape(nb, QB)
    o = lax.map(one, (qs, cqs, qpos))
    return jnp.moveaxis(o, 0, 1).reshape(b, Tq, H, dh)


def _lin_combine(c1, c2):
    a1, b1 = c1
    a2, b2 = c2
    return a1 * a2, a2 * b1 + b2


def _rglru(xc, lam, w_r, b_r, w_i, b_i, h0, pos0):
    b, T, W = xc.shape
    xb = xc.reshape(b, T, LRU_BLOCKS, LRU_BLOCK_W)
    r = jax.nn.sigmoid(jnp.einsum('btnc,ncd->btnd', xb, w_r).reshape(b, T, W) + b_r).astype(jnp.float32)
    i = jax.nn.sigmoid(jnp.einsum('btnc,ncd->btnd', xb, w_i).reshape(b, T, W) + b_i).astype(jnp.float32)
    log_a = LRU_C * r * jax.nn.log_sigmoid(lam.astype(jnp.float32))
    reset = ((pos0 + jnp.arange(T)) == 0)[None, :, None]
    a = jnp.where(reset, 0.0, jnp.exp(log_a))
    mult = jnp.where(reset, 1.0, jnp.sqrt(-jnp.expm1(2.0 * log_a)))
    u = mult * i * xc.astype(jnp.float32)
    u = u.at[:, 0].add(a[:, 0] * h0.astype(jnp.float32))
    _, h = lax.associative_scan(_lin_combine, (a, u), axis=1)
    return h.astype(xc.dtype), h[:, -1].astype(h0.dtype)


def _layer(x, state, pr):
    (ssm_conv, ssm_s, hgrn_s, fox_k, fox_v, fox_lf, lru_conv, lru_h) = state
    f32 = jnp.float32
    b, T, _ = x.shape
    past = fox_k.shape[1]
    h = _rms(x, pr['ln1'])
    (a_z, a_xbc, a_dt, b_q, b_f, b_i, b_g,
     c_q, c_k, c_v, c_f, c_g, d_x, d_g) = _split_in(h @ pr['w_in'])

    xbc, ssm_conv_new = _causal_conv(a_xbc, ssm_conv, pr['ssm_conv_w'], pr['ssm_conv_b'])
    xbc = jax.nn.silu(xbc)
    gn = SSM_GROUPS * SSM_STATE
    a_x = xbc[..., :GROUP_W].reshape(b, T, SSM_HEADS, SSM_HEAD_DIM)
    a_B = jnp.repeat(xbc[..., GROUP_W:GROUP_W + gn].reshape(b, T, SSM_GROUPS, SSM_STATE),
                     SSM_HEADS // SSM_GROUPS, axis=2)
    a_C = jnp.repeat(xbc[..., GROUP_W + gn:].reshape(b, T, SSM_GROUPS, SSM_STATE),
                     SSM_HEADS // SSM_GROUPS, axis=2)
    dt = jax.nn.softplus(a_dt.astype(f32) + pr['ssm_dt_bias'].astype(f32))
    A = -jnp.exp(pr['ssm_a_log'].astype(f32))
    o, ssm_s_new = _chunk_recurrence(a_C, a_B, a_x * dt[..., None].astype(a_x.dtype),
                                     (dt * A)[..., None], ssm_s)
    ya = (o + pr['ssm_d'][:, None] * a_x).reshape(b, T, GROUP_W) * jax.nn.silu(a_z)
    ya = _normalize(ya.reshape(b, T, SSM_GROUPS, GROUP_W // SSM_GROUPS)).reshape(b, T, GROUP_W) * pr['ssm_norm']

    lb = pr['lb'].astype(f32)
    fr = b_f.astype(f32)
    f_gate = lb + (1.0 - lb) * jax.nn.sigmoid(fr)
    logf = jnp.log(jnp.maximum(f_gate, HGRN_F_FLOOR))
    hk = (1.0 - lb) * jax.nn.sigmoid(-fr)
    hq = jax.nn.silu(b_q)
    hs = lambda t: t.reshape(b, T, HGRN_HEADS, HGRN_HEAD_DIM)
    o, hgrn_s_new = _chunk_recurrence(hs(hq), hs(hk), hs(b_i), hs(logf), hgrn_s)
    yb = _rms(o, pr['hgrn_norm'].reshape(HGRN_HEADS, HGRN_HEAD_DIM)).reshape(b, T, GROUP_W) * jax.nn.silu(b_g)

    fs = lambda t: t.reshape(b, T, FOX_HEADS, FOX_HEAD_DIM)
    fq = _rms(fs(c_q), pr['fox_q_norm']) * (FOX_HEAD_DIM ** -0.5)
    fk = _rms(fs(c_k), pr['fox_k_norm'])
    fv = fs(c_v)
    flf = jax.nn.log_sigmoid(c_f.astype(f32) + pr['fox_f_bias'].astype(f32))
    K = jnp.concatenate([fox_k.astype(fk.dtype), fk], axis=1)
    V = jnp.concatenate([fox_v.astype(fv.dtype), fv], axis=1)
    cum = jnp.cumsum(jnp.concatenate([fox_lf.astype(f32), flf], axis=1), axis=1)
    o = _fox_attention(fq, K, V, cum[:, past:], cum)
    yc = o.reshape(b, T, GROUP_W) * jax.nn.sigmoid(c_g)

    xc, lru_conv_new = _causal_conv(d_x, lru_conv, pr['lru_conv_w'], pr['lru_conv_b'])
    hseq, lru_h_new = _rglru(xc, pr['lru_lambda'], pr['lru_w_r'], pr['lru_b_r'],
                             pr['lru_w_i'], pr['lru_b_i'], lru_h, past)
    yd = hseq * jax.nn.gelu(d_g)

    x = x + jnp.concatenate([ya, yb, yc, yd], axis=-1) @ pr['w_out']
    h2 = _rms(x, pr['ln2'])
    x = x + (jax.nn.silu(h2 @ pr['w_gate']) * (h2 @ pr['w_up'])) @ pr['w_down']
    return x, (ssm_conv_new, ssm_s_new, hgrn_s_new, fk, fv, flf, lru_conv_new, lru_h_new)


def _stack_states(states):
    return [jnp.stack(parts, axis=0) for parts in zip(*states)]


def setup_inputs(seed: int = 0) -> dict:
    key = jax.random.key(seed)
    ks = jax.random.split(key, 40)
    f32 = jnp.float32

    def nrm(k, shape, scale):
        return jax.random.normal(k, shape, f32) * scale

    def uni(k, shape, lo, hi):
        return jax.random.uniform(k, shape, f32, lo, hi)

    dt0 = jnp.exp(uni(ks[20], (DEPTH, SSM_HEADS), math.log(1e-3), math.log(1e-1)))
    a0 = uni(ks[33], (DEPTH, GROUP_W), 0.9, 0.999) ** (1.0 / LRU_C)
    return {
        'x_prompt': nrm(ks[0], (BATCH, SEQ, D_MODEL), 1.0),
        'x_sample': nrm(ks[1], (DEC_BATCH, DEC_SEQ, D_MODEL), 1.0),
        'cache_fox_k': nrm(ks[2], (DEPTH, DEC_BATCH, PAST_LEN, FOX_HEADS, FOX_HEAD_DIM), 1.0),
        'cache_fox_v': nrm(ks[3], (DEPTH, DEC_BATCH, PAST_LEN, FOX_HEADS, FOX_HEAD_DIM), 1.0),
        'cache_fox_logf': jax.nn.log_sigmoid(FOX_F_BIAS_INIT + nrm(ks[4], (DEPTH, DEC_BATCH, PAST_LEN, FOX_HEADS), 1.0)),
        'state_ssm_conv': nrm(ks[5], (DEPTH, DEC_BATCH, SSM_CONV - 1, SSM_CONV_DIM), 1.0),
        'state_ssm': nrm(ks[6], (DEPTH, DEC_BATCH, SSM_HEADS, SSM_STATE, SSM_HEAD_DIM), 0.1),
        'state_hgrn': nrm(ks[7], (DEPTH, DEC_BATCH, HGRN_HEADS, HGRN_HEAD_DIM, HGRN_HEAD_DIM), 0.1),
        'state_lru_conv': nrm(ks[8], (DEPTH, DEC_BATCH, LRU_CONV - 1, GROUP_W), 1.0),
        'state_lru': nrm(ks[9], (DEPTH, DEC_BATCH, GROUP_W), 0.3),
        'ln1': 1.0 + nrm(ks[10], (DEPTH, D_MODEL), 0.02),
        'ln2': 1.0 + nrm(ks[11], (DEPTH, D_MODEL), 0.02),
        'w_in': nrm(ks[12], (DEPTH, D_MODEL, D_IN), D_MODEL ** -0.5),
        'w_out': nrm(ks[13], (DEPTH, D_MIX, D_MODEL), D_MIX ** -0.5),
        'ssm_conv_w': nrm(ks[14], (DEPTH, SSM_CONV, SSM_CONV_DIM), SSM_CONV ** -0.5),
        'ssm_conv_b': nrm(ks[15], (DEPTH, SSM_CONV_DIM), 0.02),
        'ssm_dt_bias': dt0 + jnp.log(-jnp.expm1(-dt0)),
        'ssm_a_log': jnp.log(uni(ks[21], (DEPTH, SSM_HEADS), 1.0, 16.0)),
        'ssm_d': 1.0 + nrm(ks[22], (DEPTH, SSM_HEADS), 0.1),
        'ssm_norm': 1.0 + nrm(ks[23], (DEPTH, GROUP_W), 0.02),
        'hgrn_lb_logits': nrm(ks[24], (DEPTH, GROUP_W), 0.5),
        'hgrn_norm': 1.0 + nrm(ks[25], (DEPTH, GROUP_W), 0.02),
        'fox_q_norm': 1.0 + nrm(ks[26], (DEPTH, FOX_HEAD_DIM), 0.02),
        'fox_k_norm': 1.0 + nrm(ks[27], (DEPTH, FOX_HEAD_DIM), 0.02),
        'fox_f_bias': FOX_F_BIAS_INIT + nrm(ks[28], (DEPTH, FOX_HEADS), 0.1),
        'lru_conv_w': nrm(ks[29], (DEPTH, LRU_CONV, GROUP_W), LRU_CONV ** -0.5),
        'lru_conv_b': nrm(ks[30], (DEPTH, GROUP_W), 0.02),
        'lru_w_r': nrm(ks[31], (DEPTH, LRU_BLOCKS, LRU_BLOCK_W, LRU_BLOCK_W), LRU_BLOCK_W ** -0.5),
        'lru_b_r': nrm(ks[32], (DEPTH, GROUP_W), 0.02),
        'lru_w_i': nrm(ks[34], (DEPTH, LRU_BLOCKS, LRU_BLOCK_W, LRU_BLOCK_W), LRU_BLOCK_W ** -0.5),
        'lru_b_i': nrm(ks[35], (DEPTH, GROUP_W), 0.02),
        'lru_lambda': jnp.log(a0) - jnp.log1p(-a0),
        'w_gate': nrm(ks[36], (DEPTH, D_MODEL, D_FF), D_MODEL ** -0.5),
        'w_up': nrm(ks[37], (DEPTH, D_MODEL, D_FF), D_MODEL ** -0.5),
        'w_down': nrm(ks[38], (DEPTH, D_FF, D_MODEL), D_FF ** -0.5),
    }


def reference(x_prompt, x_sample, cache_fox_k, cache_fox_v, cache_fox_logf, state_ssm_conv, state_ssm,
              state_hgrn, state_lru_conv, state_lru, ln1, ln2, w_in, w_out, ssm_conv_w, ssm_conv_b,
              ssm_dt_bias, ssm_a_log, ssm_d, ssm_norm, hgrn_lb_logits, hgrn_norm, fox_q_norm, fox_k_norm,
              fox_f_bias, lru_conv_w, lru_conv_b, lru_w_r, lru_b_r, lru_w_i, lru_b_i, lru_lambda,
              w_gate, w_up, w_down):
    f32 = jnp.float32
    lb_p = jax.nn.softmax(hgrn_lb_logits.astype(f32), axis=0)
    lower_bounds = jnp.cumsum(lb_p, axis=0) - lb_p[0]
    layers = [dict(ln1=ln1[l], ln2=ln2[l], w_in=w_in[l], w_out=w_out[l], ssm_conv_w=ssm_conv_w[l],
                   ssm_conv_b=ssm_conv_b[l], ssm_dt_bias=ssm_dt_bias[l], ssm_a_log=ssm_a_log[l],
                   ssm_d=ssm_d[l], ssm_norm=ssm_norm[l], lb=lower_bounds[l], hgrn_norm=hgrn_norm[l],
                   fox_q_norm=fox_q_norm[l], fox_k_norm=fox_k_norm[l], fox_f_bias=fox_f_bias[l],
                   lru_conv_w=lru_conv_w[l], lru_conv_b=lru_conv_b[l], lru_w_r=lru_w_r[l], lru_b_r=lru_b_r[l],
                   lru_w_i=lru_w_i[l], lru_b_i=lru_b_i[l], lru_lambda=lru_lambda[l],
                   w_gate=w_gate[l], w_up=w_up[l], w_down=w_down[l]) for l in range(DEPTH)]
    bp = x_prompt.shape[0]
    dtp = x_prompt.dtype
    fresh = (jnp.zeros((bp, SSM_CONV - 1, SSM_CONV_DIM), dtp),
             jnp.zeros((bp, SSM_HEADS, SSM_STATE, SSM_HEAD_DIM), dtp),
             jnp.zeros((bp, HGRN_HEADS, HGRN_HEAD_DIM, HGRN_HEAD_DIM), dtp),
             jnp.zeros((bp, 0, FOX_HEADS, FOX_HEAD_DIM), dtp),
             jnp.zeros((bp, 0, FOX_HEADS, FOX_HEAD_DIM), dtp),
             jnp.zeros((bp, 0, FOX_HEADS), f32),
             jnp.zeros((bp, LRU_CONV - 1, GROUP_W), dtp),
             jnp.zeros((bp, GROUP_W), dtp))
    y_prompt = x_prompt
    y_sample = x_sample
    prompt_states = []
    sample_states = []
    for l in range(DEPTH):
        y_prompt, st_p = _layer(y_prompt, fresh, layers[l])
        prompt_states.append(st_p)
        past = (state_ssm_conv[l], state_ssm[l], state_hgrn[l], cache_fox_k[l], cache_fox_v[l],
                cache_fox_logf[l], state_lru_conv[l], state_lru[l])
        y_sample, st_s = _layer(y_sample, past, layers[l])
        sample_states.append(st_s)
    (p_ssm_conv, p_ssm, p_hgrn, p_fox_k, p_fox_v, p_fox_logf, p_lru_conv, p_lru) = _stack_states(prompt_states)
    (s_ssm_conv, s_ssm, s_hgrn, s_fox_k, s_fox_v, s_fox_logf, s_lru_conv, s_lru) = _stack_states(sample_states)
    return (y_prompt, y_sample, p_ssm_conv, p_ssm, p_hgrn, p_fox_k, p_fox_v, p_fox_logf, p_lru_conv, p_lru,
            s_ssm_conv, s_ssm, s_hgrn, s_fox_k, s_fox_v, s_fox_logf, s_lru_conv, s_lru)
```

```python
import functools
import math

import jax
import jax.numpy as jnp
from jax import lax
from jax.experimental import pallas as pl
from jax.experimental.pallas import tpu as pltpu

F32 = jnp.float32
BF16 = jnp.bfloat16

D_MODEL = 2048
GROUP_W = 512
CHUNK = 64
SUB = 16
SSM_HEADS = 8
SSM_HEAD_DIM = 64
SSM_GROUPS = 2
SSM_STATE = 128
SSM_CONV = 4
SSM_CONV_DIM = 1024
HGRN_HEADS = 4
HGRN_HEAD_DIM = 128
HGRN_F_FLOOR = 1e-30
FOX_HEADS = 8
FOX_HEAD_DIM = 64
LRU_BLOCKS = 4
LRU_BLOCK_W = 128
LRU_CONV = 4
LRU_C = 8.0
EPS = 1e-6
NEG = -1e30

LANES = 128
CONV_PAD = 8
VMEM_LIMIT = 56 * 1024 * 1024

COL_A_Z, COL_A_X, COL_A_BC, COL_B_Q, COL_B_F, COL_B_I, COL_B_G = 0, 1, 2, 3, 4, 5, 6
COL_C_Q, COL_C_K, COL_C_V, COL_C_G, COL_D_X, COL_D_G = 7, 8, 9, 10, 11, 12
N_MAIN_BLOCKS = 13


def _cparams(sem):
    return pltpu.CompilerParams(dimension_semantics=sem, vmem_limit_bytes=VMEM_LIMIT)


def _dot(a, b):
    return jnp.dot(a, b, preferred_element_type=F32)


def _dot_nt(a, b):
    return lax.dot_general(a, b, (((1,), (1,)), ((), ())), preferred_element_type=F32)


def _dot_tn(a, b):
    return lax.dot_general(a, b, (((0,), (0,)), ((), ())), preferred_element_type=F32)


def _split3(x):
    hi = x.astype(BF16)
    r = x - hi.astype(F32)
    mid = r.astype(BF16)
    lo = (r - mid.astype(F32)).astype(BF16)
    return hi, mid, lo


def _dot01(m01, x):
    hi, mid, lo = _split3(x)
    return _dot(m01, hi) + _dot(m01, mid) + _dot(m01, lo)


def _dot01_r(x, m01):
    hi, mid, lo = _split3(x)
    return _dot(hi, m01) + _dot(mid, m01) + _dot(lo, m01)


def _tri(n, lower=True):
    r = lax.broadcasted_iota(jnp.int32, (n, n), 0)
    c = lax.broadcasted_iota(jnp.int32, (n, n), 1)
    return (r >= c) if lower else (r <= c)


def _sigmoid(x):
    return jax.nn.sigmoid(x)


def _silu(x):
    return x * jax.nn.sigmoid(x)


def _softplus(x):
    return jnp.maximum(x, 0.0) + jnp.log1p(jnp.exp(-jnp.abs(x)))


def _log_sigmoid(x):
    return -_softplus(-x)


def _in_proj_kernel(x_ref, g_ref, w_ref, ws_ref, o_ref, os_ref, h_scr):
    @pl.when(pl.program_id(1) == 0)
    def _():
        x = x_ref[...]
        ms = jnp.mean(x * x, axis=-1, keepdims=True)
        h = (x * lax.rsqrt(ms + EPS) * g_ref[...]).astype(BF16)
        h_scr[...] = h
        os_ref[...] = _dot(h, ws_ref[...])

    o_ref[...] = _dot(h_scr[...], w_ref[...])


def _in_proj(x2d, ln, w_main, w_small, tm):
    m = x2d.shape[0]
    n_main, n_small = w_main.shape[1], w_small.shape[1]
    return pl.pallas_call(
        _in_proj_kernel,
        grid=(m // tm, n_main // GROUP_W),
        in_specs=[
            pl.BlockSpec((tm, D_MODEL), lambda i, j: (i, 0)),
            pl.BlockSpec((1, D_MODEL), lambda i, j: (0, 0)),
            pl.BlockSpec((D_MODEL, GROUP_W), lambda i, j: (0, j)),
            pl.BlockSpec((D_MODEL, n_small), lambda i, j: (0, 0)),
        ],
        out_specs=[
            pl.BlockSpec((tm, GROUP_W), lambda i, j: (i, j)),
            pl.BlockSpec((tm, n_small), lambda i, j: (i, 0)),
        ],
        out_shape=[jax.ShapeDtypeStruct((m, n_main), F32), jax.ShapeDtypeStruct((m, n_small), F32)],
        scratch_shapes=[pltpu.VMEM((tm, D_MODEL), BF16)],
        compiler_params=_cparams(("parallel", "arbitrary")),
        name="in_proj",
    )(x2d, ln, w_main, w_small)


def _conv_block(xp_scr, cw_ref, cb_ref, tb, kw):
    first = CONV_PAD - (kw - 1)
    y = cb_ref[...]
    for j in range(kw):
        y = y + cw_ref[j:j + 1, :] * xp_scr[pl.ds(first + j, tb), :]
    tail = xp_scr[pl.ds(first + tb, kw - 1), :]
    xp_scr[pl.ds(first, kw - 1), :] = tail
    return y, tail


def _ssd_kernel(z_ref, xlo_ref, xhi_ref, sm_ref, conv0_ref, s0_ref, cw_ref, cb_ref, dtb_ref, alog_ref,
                dexp_ref, nrm_ref, y_ref, convo_ref, so_ref,
                xp_scr, xbc_scr, g_scr, dt_scr, y_scr, s_scr, *, tb):
    c = pl.program_id(1)
    last = pl.num_programs(1) - 1
    hpg = SSM_HEADS // SSM_GROUPS
    gw = hpg * SSM_HEAD_DIM

    @pl.when(c == 0)
    def _():
        s_scr[...] = s0_ref[...]
        xp_scr[0:CONV_PAD, :] = jnp.zeros((CONV_PAD, SSM_CONV_DIM), F32)
        xp_scr[CONV_PAD - (SSM_CONV - 1):CONV_PAD, :] = conv0_ref[...]

    xp_scr[CONV_PAD:CONV_PAD + tb, 0:GROUP_W] = xlo_ref[...]
    xp_scr[CONV_PAD:CONV_PAD + tb, GROUP_W:SSM_CONV_DIM] = xhi_ref[...]
    conv, tail = _conv_block(xp_scr, cw_ref, cb_ref, tb, SSM_CONV)

    @pl.when(c == last)
    def _():
        convo_ref[...] = tail

    xbc_scr[...] = _silu(conv)
    dt = _softplus(sm_ref[...] + dtb_ref[...])
    dt_scr[...] = dt
    g_scr[...] = dt * (-jnp.exp(alog_ref[...]))

    tri_mask = _tri(CHUNK)
    tri_l = tri_mask.astype(BF16)
    tri_u = _tri(CHUNK, lower=False).astype(BF16)
    b_off = GROUP_W
    c_off = GROUP_W + SSM_GROUPS * SSM_STATE

    def chunk(ci, carry):
        rows = pl.ds(pl.multiple_of(ci * CHUNK, CHUNK), CHUNK)
        gc = g_scr[rows, :]
        dtc = dt_scr[rows, :]
        cum = _dot01(tri_l, gc)
        cum_t = _dot01_r(gc.T, tri_u)
        tot = cum[CHUNK - 1:CHUNK, :]
        ys = []
        for grp in range(SSM_GROUPS):
            bm = xbc_scr[rows, b_off + grp * SSM_STATE:b_off + (grp + 1) * SSM_STATE].astype(BF16)
            cm = xbc_scr[rows, c_off + grp * SSM_STATE:c_off + (grp + 1) * SSM_STATE].astype(BF16)
            gmat = _dot_nt(cm, bm)
            s_g = s_scr[grp]
            cs = _dot(cm, s_g.astype(BF16))
            vdec, etot = [], []
            for hh in range(hpg):
                h = grp * hpg + hh
                col = cum[:, h:h + 1]
                row = cum_t[h:h + 1, :]
                dec = jnp.where(tri_mask, jnp.exp(jnp.minimum(col - row, 0.0)), 0.0)
                att = (gmat * dec).astype(BF16)
                xh = xbc_scr[rows, h * SSM_HEAD_DIM:(h + 1) * SSM_HEAD_DIM]
                vh = xh * dtc[:, h:h + 1]
                o = _dot(att, vh.astype(BF16)) + jnp.exp(col) * cs[:, hh * SSM_HEAD_DIM:(hh + 1) * SSM_HEAD_DIM]
                ys.append(o + dexp_ref[:, h * SSM_HEAD_DIM:(h + 1) * SSM_HEAD_DIM] * xh)
                toth = tot[:, h:h + 1]
                vdec.append(vh * jnp.exp(toth - col))
                etot.append(jnp.broadcast_to(jnp.exp(toth), (1, SSM_HEAD_DIM)))
            vdec = jnp.concatenate(vdec, axis=1).astype(BF16)
            etot = jnp.concatenate(etot, axis=1)
            s_scr[grp] = etot * s_g + _dot_tn(bm, vdec)
        y_scr[rows, :] = jnp.concatenate(ys, axis=1)
        return carry

    lax.fori_loop(0, tb // CHUNK, chunk, 0)

    y = y_scr[...] * _silu(z_ref[...])
    outs = []
    for grp in range(SSM_GROUPS):
        yg = y[:, grp * gw:(grp + 1) * gw]
        ms = jnp.mean(yg * yg, axis=-1, keepdims=True)
        outs.append(yg * lax.rsqrt(ms + EPS))
    y_ref[...] = (jnp.concatenate(outs, axis=1) * nrm_ref[...]).astype(y_ref.dtype)

    @pl.when(c == last)
    def _():
        so_ref[...] = s_scr[...]


def _ssd(proj, small, conv0, s0, cw, cb, dtb, alog, dexp, nrm, nb, t, tb):
    nc = t // tb
    hpg = SSM_HEADS // SSM_GROUPS
    sshape = (SSM_GROUPS, SSM_STATE, hpg * SSM_HEAD_DIM)
    row = lambda col: pl.BlockSpec((tb, GROUP_W), lambda b, c: (b * nc + c, col))
    full = lambda a: pl.BlockSpec(a.shape, lambda b, c: (0,) * a.ndim)
    return pl.pallas_call(
        functools.partial(_ssd_kernel, tb=tb),
        grid=(nb, nc),
        in_specs=[row(COL_A_Z), row(COL_A_X), row(COL_A_BC),
                  pl.BlockSpec((tb, LANES), lambda b, c: (b * nc + c, 0)),
                  pl.BlockSpec((None, SSM_CONV - 1, SSM_CONV_DIM), lambda b, c: (b, 0, 0)),
                  pl.BlockSpec((None,) + sshape, lambda b, c: (b, 0, 0, 0)),
                  full(cw), full(cb), full(dtb), full(alog), full(dexp), full(nrm)],
        out_specs=[pl.BlockSpec((tb, GROUP_W), lambda b, c: (b * nc + c, 0)),
                   pl.BlockSpec((None, SSM_CONV - 1, SSM_CONV_DIM), lambda b, c: (b, 0, 0)),
                   pl.BlockSpec((None,) + sshape, lambda b, c: (b, 0, 0, 0))],
        out_shape=[jax.ShapeDtypeStruct((nb * t, GROUP_W), BF16),
                   jax.ShapeDtypeStruct((nb, SSM_CONV - 1, SSM_CONV_DIM), F32),
                   jax.ShapeDtypeStruct((nb,) + sshape, F32)],
        scratch_shapes=[pltpu.VMEM((CONV_PAD + tb, SSM_CONV_DIM), F32),
                        pltpu.VMEM((tb, SSM_CONV_DIM), F32),
                        pltpu.VMEM((tb, LANES), F32),
                        pltpu.VMEM((tb, LANES), F32),
                        pltpu.VMEM((tb, GROUP_W), F32),
                        pltpu.VMEM(sshape, F32)],
        compiler_params=_cparams(("parallel", "arbitrary")),
        name="ssd",
    )(proj, proj, proj, small, conv0, s0, cw, cb, dtb, alog, dexp, nrm)


def _hgrn_kernel(q_ref, f_ref, i_ref, g_ref, s0_ref, lbl_ref, nrm_ref, y_ref, so_ref,
                 c_scr, k_scr, st_scr, *, tb, layer):
    c = pl.program_id(1)
    last = pl.num_programs(1) - 1
    hd = HGRN_HEAD_DIM
    nsub = CHUNK // SUB

    @pl.when(c == 0)
    def _():
        for h in range(HGRN_HEADS):
            st_scr[h] = s0_ref[h].T

    ll = lbl_ref[...]
    e = jnp.exp(ll - jnp.max(ll, axis=0, keepdims=True))
    p = e / jnp.sum(e, axis=0, keepdims=True)
    cs = p[0:1, :]
    for l in range(1, layer + 1):
        cs = cs + p[l:l + 1, :]
    lb = cs - p[0:1, :]

    tri_l = _tri(CHUNK).astype(BF16)
    rowid = lax.broadcasted_iota(jnp.int32, (CHUNK, 1), 0)
    subid = lax.broadcasted_iota(jnp.int32, (SUB, 1), 0)

    def chunk(ci, carry):
        r0 = pl.multiple_of(ci * CHUNK, CHUNK)
        rows = pl.ds(r0, CHUNK)
        fr = f_ref[rows, :]
        fg = lb + (1.0 - lb) * _sigmoid(fr)
        logf = jnp.log(jnp.maximum(fg, HGRN_F_FLOOR))
        hk = (1.0 - lb) * _sigmoid(-fr)
        hq = _silu(q_ref[rows, :])
        v = i_ref[rows, :]
        vb = v.astype(BF16)
        cum = _dot01(tri_l, logf)
        tot = cum[CHUNK - 1:CHUNK, :]
        c_scr[...] = cum
        k_scr[...] = hk
        qe = (hq * jnp.exp(cum)).astype(BF16)
        kd = (hk * jnp.exp(tot - cum)).astype(BF16)
        etot = jnp.exp(tot)

        qcat, kcat = [], []
        for j in range(nsub - 1):
            ej = cum[(j + 1) * SUB - 1:(j + 1) * SUB, :]
            qcat.append(jnp.where(rowid >= (j + 1) * SUB, hq * jnp.exp(jnp.minimum(cum - ej, 0.0)), 0.0).astype(BF16))
            in_j = (rowid >= j * SUB) & (rowid < (j + 1) * SUB)
            kcat.append(jnp.where(in_j, hk * jnp.exp(jnp.minimum(ej - cum, 0.0)), 0.0).astype(BF16))

        o_heads = []
        for h in range(HGRN_HEADS):
            hs = slice(h * hd, (h + 1) * hd)
            qc = jnp.concatenate([q[:, hs] for q in qcat], axis=1)
            kc = jnp.concatenate([k[:, hs] for k in kcat], axis=1)
            a_off = _dot_nt(qc, kc)
            st = st_scr[h]
            o_heads.append(_dot(a_off.astype(BF16), vb[:, hs]) + _dot_nt(qe[:, hs], st.astype(BF16)))
            st_scr[h] = st * etot[:, hs] + _dot_tn(vb[:, hs], kd[:, hs])
        o = jnp.concatenate(o_heads, axis=1)

        o_sub = []
        for i in range(nsub):
            base = i * SUB
            ci_ = cum[base:base + SUB, :]
            qi = hq[base:base + SUB, :]
            acc = [jnp.zeros((SUB, hd), F32) for _ in range(HGRN_HEADS)]
            for j in range(SUB):
                crow = c_scr[pl.ds(base + j, 1), :]
                krow = k_scr[pl.ds(base + j, 1), :]
                vrow = i_ref[pl.ds(r0 + base + j, 1), :]
                x = jnp.where(subid >= j, qi * jnp.exp(jnp.minimum(ci_ - crow, 0.0)) * krow, 0.0)
                for h in range(HGRN_HEADS):
                    hs = slice(h * hd, (h + 1) * hd)
                    a = jnp.sum(x[:, hs], axis=-1, keepdims=True)
                    acc[h] = acc[h] + a * vrow[:, hs]
            o_sub.append(jnp.concatenate(acc, axis=1))
        o = o + jnp.concatenate(o_sub, axis=0)

        outs = []
        for h in range(HGRN_HEADS):
            oh = o[:, h * hd:(h + 1) * hd]
            ms = jnp.mean(oh * oh, axis=-1, keepdims=True)
            outs.append(oh * lax.rsqrt(ms + EPS))
        y = jnp.concatenate(outs, axis=1) * nrm_ref[...] * _silu(g_ref[rows, :])
        y_ref[rows, :] = y.astype(y_ref.dtype)
        return carry

    lax.fori_loop(0, tb // CHUNK, chunk, 0)

    @pl.when(c == last)
    def _():
        for h in range(HGRN_HEADS):
            so_ref[h] = st_scr[h].T


def _hgrn(proj, s0, lbl, nrm, nb, t, tb, layer):
    nc = t // tb
    sshape = (HGRN_HEADS, HGRN_HEAD_DIM, HGRN_HEAD_DIM)
    row = lambda col: pl.BlockSpec((tb, GROUP_W), lambda b, c: (b * nc + c, col))
    full = lambda a: pl.BlockSpec(a.shape, lambda b, c: (0,) * a.ndim)
    return pl.pallas_call(
        functools.partial(_hgrn_kernel, tb=tb, layer=layer),
        grid=(nb, nc),
        in_specs=[row(COL_B_Q), row(COL_B_F), row(COL_B_I), row(COL_B_G),
                  pl.BlockSpec((None,) + sshape, lambda b, c: (b, 0, 0, 0)),
                  full(lbl), full(nrm)],
        out_specs=[pl.BlockSpec((tb, GROUP_W), lambda b, c: (b * nc + c, 0)),
                   pl.BlockSpec((None,) + sshape, lambda b, c: (b, 0, 0, 0))],
        out_shape=[jax.ShapeDtypeStruct((nb * t, GROUP_W), BF16),
                   jax.ShapeDtypeStruct((nb,) + sshape, F32)],
        scratch_shapes=[pltpu.VMEM((CHUNK, GROUP_W), F32),
                        pltpu.VMEM((CHUNK, GROUP_W), F32),
                        pltpu.VMEM(sshape, F32)],
        compiler_params=_cparams(("parallel", "arbitrary")),
        name="hgrn2",
    )(proj, proj, proj, proj, s0, lbl, nrm)


def _lru_kernel(x_ref, g_ref, conv0_ref, h0_ref, cw_ref, cb_ref, wr_ref, br_ref, wi_ref, bi_ref, lam_ref,
                y_ref, convo_ref, ho_ref, xp_scr, h_scr, *, tb, reset_first):
    c = pl.program_id(1)
    last = pl.num_programs(1) - 1

    @pl.when(c == 0)
    def _():
        h_scr[...] = h0_ref[...]
        xp_scr[0:CONV_PAD, :] = jnp.zeros((CONV_PAD, GROUP_W), F32)
        xp_scr[CONV_PAD - (LRU_CONV - 1):CONV_PAD, :] = conv0_ref[...]

    xp_scr[CONV_PAD:CONV_PAD + tb, :] = x_ref[...]
    xc, tail = _conv_block(xp_scr, cw_ref, cb_ref, tb, LRU_CONV)

    @pl.when(c == last)
    def _():
        convo_ref[...] = tail

    xcb = xc.astype(BF16)

    def blockdiag(w_ref):
        return jnp.concatenate(
            [_dot(xcb[:, n * LRU_BLOCK_W:(n + 1) * LRU_BLOCK_W], w_ref[n]) for n in range(LRU_BLOCKS)], axis=1)

    r = _sigmoid(blockdiag(wr_ref) + br_ref[...])
    gi = _sigmoid(blockdiag(wi_ref) + bi_ref[...])
    log_a = LRU_C * r * _log_sigmoid(lam_ref[...])
    a = jnp.exp(log_a)
    mult = jnp.sqrt(-jnp.tanh(log_a) * (a * a + 1.0))
    rowid = lax.broadcasted_iota(jnp.int32, (tb, 1), 0)
    first_row = rowid == 0
    if reset_first:
        rst = first_row & (c == 0)
        a = jnp.where(rst, 0.0, a)
        mult = jnp.where(rst, 1.0, mult)
    u = mult * gi * xc
    u = u + jnp.where(first_row, a * h_scr[...], 0.0)

    av, bv = a, u
    d = 1
    while d < tb:
        m = rowid >= d
        a_sh = pltpu.roll(av, d, axis=0)
        b_sh = pltpu.roll(bv, d, axis=0)
        bv = jnp.where(m, av * b_sh + bv, bv)
        av = jnp.where(m, av * a_sh, av)
        d *= 2
    h_new = bv[tb - 1:tb, :]
    h_scr[...] = h_new
    y_ref[...] = (bv * jax.nn.gelu(g_ref[...])).astype(y_ref.dtype)

    @pl.when(c == last)
    def _():
        ho_ref[...] = h_new


def _lru(proj, conv0, h0, cw, cb, wr, br, wi, bi, lam, nb, t, tb, reset_first):
    nc = t // tb
    row = lambda col: pl.BlockSpec((tb, GROUP_W), lambda b, c: (b * nc + c, col))
    full = lambda a: pl.BlockSpec(a.shape, lambda b, c: (0,) * a.ndim)
    return pl.pallas_call(
        functools.partial(_lru_kernel, tb=tb, reset_first=reset_first),
        grid=(nb, nc),
        in_specs=[row(COL_D_X), row(COL_D_G),
                  pl.BlockSpec((None, LRU_CONV - 1, GROUP_W), lambda b, c: (b, 0, 0)),
                  pl.BlockSpec((None, 1, GROUP_W), lambda b, c: (b, 0, 0)),
                  full(cw), full(cb), full(wr), full(br), full(wi), full(bi), full(lam)],
        out_specs=[pl.BlockSpec((tb, GROUP_W), lambda b, c: (b * nc + c, 0)),
                   pl.BlockSpec((None, LRU_CONV - 1, GROUP_W), lambda b, c: (b, 0, 0)),
                   pl.BlockSpec((None, 1, GROUP_W), lambda b, c: (b, 0, 0))],
        out_shape=[jax.ShapeDtypeStruct((nb * t, GROUP_W), BF16),
                   jax.ShapeDtypeStruct((nb, LRU_CONV - 1, GROUP_W), F32),
                   jax.ShapeDtypeStruct((nb, 1, GROUP_W), F32)],
        scratch_shapes=[pltpu.VMEM((CONV_PAD + tb, GROUP_W), F32),
                        pltpu.VMEM((1, GROUP_W), F32)],
        compiler_params=_cparams(("parallel", "arbitrary")),
        name="rglru",
    )(proj, proj, conv0, h0, cw, cb, wr, br, wi, bi, lam)


def _foxprep_kernel(q_ref, k_ref, v_ref, sm_ref, hsum_ref, qn_ref, kn_ref, fb_ref,
                    fq_ref, fk_ref, fv_ref, kb_ref, vb_ref, lf_ref):
    hsum = hsum_ref[...]

    def hnorm(x, gain):
        ms = _dot01_r(x * x, hsum) * (1.0 / FOX_HEAD_DIM)
        return x * lax.rsqrt(ms + EPS) * gain

    fq_ref[...] = (hnorm(q_ref[...], qn_ref[...]) * (FOX_HEAD_DIM ** -0.5)).astype(BF16)
    fk = hnorm(k_ref[...], kn_ref[...])
    fk_ref[...] = fk
    kb_ref[...] = fk.astype(BF16)
    v = v_ref[...]
    fv_ref[...] = v
    vb_ref[...] = v.astype(BF16)
    lf_ref[...] = _log_sigmoid(sm_ref[...] + fb_ref[...])


def _foxprep(proj, small, hsum, qn, kn, fb, tb):
    m = proj.shape[0]
    row = lambda col: pl.BlockSpec((tb, GROUP_W), lambda i: (i, col))
    full = lambda a: pl.BlockSpec(a.shape, lambda i: (0,) * a.ndim)
    out_row = pl.BlockSpec((tb, GROUP_W), lambda i: (i, 0))
    return pl.pallas_call(
        _foxprep_kernel,
        grid=(m // tb,),
        in_specs=[row(COL_C_Q), row(COL_C_K), row(COL_C_V),
                  pl.BlockSpec((tb, LANES), lambda i: (i, 1)),
                  full(hsum), full(qn), full(kn), full(fb)],
        out_specs=[out_row, out_row, out_row, out_row, out_row,
                   pl.BlockSpec((tb, LANES), lambda i: (i, 0))],
        out_shape=[jax.ShapeDtypeStruct((m, GROUP_W), BF16),
                   jax.ShapeDtypeStruct((m, GROUP_W), F32),
                   jax.ShapeDtypeStruct((m, GROUP_W), F32),
                   jax.ShapeDtypeStruct((m, GROUP_W), BF16),
                   jax.ShapeDtypeStruct((m, GROUP_W), BF16),
                   jax.ShapeDtypeStruct((m, LANES), F32)],
        compiler_params=_cparams(("parallel",)),
        name="fox_prep",
    )(proj, proj, proj, small, hsum, qn, kn, fb)


def _cumsum_kernel(x_ref, o_ref, carry_scr, *, tb):
    @pl.when(pl.program_id(1) == 0)
    def _():
        carry_scr[...] = jnp.zeros_like(carry_scr)

    cum = _dot01(_tri(tb).astype(BF16), x_ref[...]) + carry_scr[...]
    o_ref[...] = cum
    carry_scr[...] = cum[tb - 1:tb, :]


def _seq_cumsum(x, tb):
    nb, length, w = x.shape
    return pl.pallas_call(
        functools.partial(_cumsum_kernel, tb=tb),
        grid=(nb, length // tb),
        in_specs=[pl.BlockSpec((None, tb, w), lambda b, c: (b, c, 0))],
        out_specs=pl.BlockSpec((None, tb, w), lambda b, c: (b, c, 0)),
        out_shape=jax.ShapeDtypeStruct(x.shape, F32),
        scratch_shapes=[pltpu.VMEM((1, w), F32)],
        compiler_params=_cparams(("parallel", "arbitrary")),
        name="gate_cumsum",
    )(x)


def _aligned(x, m):
    return x if isinstance(x, int) else pl.multiple_of(x, m)


def _attn_kernel(*refs, tq, past, tkp, single_block):
    if past:
        q_ref, cq_ref, ck_ref, k_ref, v_ref, g_ref, pk_ref, pv_ref, y_ref = refs
    else:
        q_ref, cq_ref, ck_ref, k_ref, v_ref, g_ref, y_ref = refs
    qi = 0 if single_block else pl.program_id(1)
    lane = lax.broadcasted_iota(jnp.int32, (1, LANES), 1)
    low = lane < FOX_HEAD_DIM
    causal = _tri(tq)
    zero = jnp.zeros((), BF16)

    def step(qpad, cqh, kblk, vblk, ck_row, mask, carry):
        m, l, acc = carry
        s = _dot_nt(qpad, kblk) + cqh - ck_row
        if mask is not None:
            s = jnp.where(mask, s, NEG)
        m_new = jnp.maximum(m, jnp.max(s, axis=-1, keepdims=True))
        alpha = jnp.exp(m - m_new)
        p = jnp.exp(s - m_new)
        l = alpha * l + jnp.sum(p, axis=-1, keepdims=True)
        acc = alpha * acc + _dot(p.astype(BF16), vblk)
        return m_new, l, acc

    outs = []
    for pair in range(FOX_HEADS // 2):
        ps = slice(pair * LANES, (pair + 1) * LANES)
        qp = q_ref[:, ps]
        halves = []
        for half in range(2):
            h = 2 * pair + half
            qpad = jnp.where(low if half == 0 else ~low, qp, zero)
            cqh = cq_ref[:, h:h + 1]
            carry = (jnp.full((tq, 1), NEG, F32), jnp.zeros((tq, 1), F32), jnp.zeros((tq, LANES), F32))
            for j in range(past // tkp if past else 0):
                kblk = pk_ref[j * tkp:(j + 1) * tkp, ps].astype(BF16)
                vblk = pv_ref[j * tkp:(j + 1) * tkp, ps].astype(BF16)
                carry = step(qpad, cqh, kblk, vblk, ck_ref[h:h + 1, j * tkp:(j + 1) * tkp], None, carry)

            def body(j, carry):
                r0 = pl.multiple_of(j * tq, tq)
                ck_row = ck_ref[h:h + 1, pl.ds(pl.multiple_of(past + j * tq, tq), tq)]
                return step(qpad, cqh, k_ref[pl.ds(r0, tq), ps], v_ref[pl.ds(r0, tq), ps], ck_row, None, carry)

            if not single_block:
                carry = lax.fori_loop(0, qi, body, carry)
            r0 = _aligned(qi * tq, tq)
            ck_row = ck_ref[h:h + 1, pl.ds(_aligned(past + qi * tq, tq), tq)]
            _, l, acc = step(qpad, cqh, k_ref[pl.ds(r0, tq), ps], v_ref[pl.ds(r0, tq), ps], ck_row, causal, carry)
            halves.append(acc / l)
        outs.append(jnp.where(low, halves[0], halves[1]))
    y = jnp.concatenate(outs, axis=1) * _sigmoid(g_ref[...])
    y_ref[...] = y.astype(y_ref.dtype)


def _attention(fq, cq, ck_t, kb, vb, proj, past_k, past_v, nb, t, tq):
    nq = t // tq
    past = 0 if past_k is None else past_k.shape[1]
    tkp = min(past, 256) if past else 0
    ltot = ck_t.shape[-1]
    in_specs = [pl.BlockSpec((tq, GROUP_W), lambda b, i: (b * nq + i, 0)),
                pl.BlockSpec((None, tq, LANES), lambda b, i: (b, i, 0)),
                pl.BlockSpec((None, FOX_HEADS, ltot), lambda b, i: (b, 0, 0)),
                pl.BlockSpec((None, t, GROUP_W), lambda b, i: (b, 0, 0)),
                pl.BlockSpec((None, t, GROUP_W), lambda b, i: (b, 0, 0)),
                pl.BlockSpec((tq, GROUP_W), lambda b, i: (b * nq + i, COL_C_G))]
    args = [fq, cq, ck_t, kb.reshape(nb, t, GROUP_W), vb.reshape(nb, t, GROUP_W), proj]
    if past:
        in_specs += [pl.BlockSpec((None, past, GROUP_W), lambda b, i: (b, 0, 0))] * 2
        args += [past_k, past_v]
    return pl.pallas_call(
        functools.partial(_attn_kernel, tq=tq, past=past, tkp=tkp, single_block=(nq == 1)),
        grid=(nb, nq),
        in_specs=in_specs,
        out_specs=pl.BlockSpec((tq, GROUP_W), lambda b, i: (b * nq + i, 0)),
        out_shape=jax.ShapeDtypeStruct((nb * t, GROUP_W), BF16),
        compiler_params=_cparams(("parallel", "arbitrary")),
        name="fox_attention",
    )(*args)


def _out_proj_kernel(x_ref, ya_ref, yb_ref, yc_ref, yd_ref, w_ref, o_ref):
    acc = x_ref[...]
    for n, y_ref in enumerate((ya_ref, yb_ref, yc_ref, yd_ref)):
        acc = acc + _dot(y_ref[...], w_ref[n * GROUP_W:(n + 1) * GROUP_W, :])
    o_ref[...] = acc


def _out_proj(x2d, ys, w_out, tm):
    m = x2d.shape[0]
    yspec = pl.BlockSpec((tm, GROUP_W), lambda i: (i, 0))
    return pl.pallas_call(
        _out_proj_kernel,
        grid=(m // tm,),
        in_specs=[pl.BlockSpec((tm, D_MODEL), lambda i: (i, 0)), yspec, yspec, yspec, yspec,
                  pl.BlockSpec(w_out.shape, lambda i: (0, 0))],
        out_specs=pl.BlockSpec((tm, D_MODEL), lambda i: (i, 0)),
        out_shape=jax.ShapeDtypeStruct((m, D_MODEL), F32),
        compiler_params=_cparams(("parallel",)),
        name="out_proj",
    )(x2d, *ys, w_out)


def _ffn_kernel(x_ref, g_ref, wg_ref, wu_ref, wd_ref, o_ref, h_scr, acc_scr):
    f = pl.program_id(1)

    @pl.when(f == 0)
    def _():
        x = x_ref[...]
        ms = jnp.mean(x * x, axis=-1, keepdims=True)
        h_scr[...] = (x * lax.rsqrt(ms + EPS) * g_ref[...]).astype(BF16)
        acc_scr[...] = x

    h = h_scr[...]
    a = _silu(_dot(h, wg_ref[...])) * _dot(h, wu_ref[...])
    acc_scr[...] += _dot(a.astype(BF16), wd_ref[...])

    @pl.when(f == pl.num_programs(1) - 1)
    def _():
        o_ref[...] = acc_scr[...]


def _ffn(x2d, ln, wg, wu, wd, tm, tf):
    m = x2d.shape[0]
    d_ff = wg.shape[1]
    return pl.pallas_call(
        _ffn_kernel,
        grid=(m // tm, d_ff // tf),
        in_specs=[pl.BlockSpec((tm, D_MODEL), lambda i, f: (i, 0)),
                  pl.BlockSpec((1, D_MODEL), lambda i, f: (0, 0)),
                  pl.BlockSpec((D_MODEL, tf), lambda i, f: (0, f)),
                  pl.BlockSpec((D_MODEL, tf), lambda i, f: (0, f)),
                  pl.BlockSpec((tf, D_MODEL), lambda i, f: (f, 0))],
        out_specs=pl.BlockSpec((tm, D_MODEL), lambda i, f: (i, 0)),
        out_shape=jax.ShapeDtypeStruct((m, D_MODEL), F32),
        scratch_shapes=[pltpu.VMEM((tm, D_MODEL), BF16), pltpu.VMEM((tm, D_MODEL), F32)],
        compiler_params=_cparams(("parallel", "arbitrary")),
        name="swiglu_ffn",
    )(x2d, ln, wg, wu, wd)


def _pad_lanes(v, offset=0):
    return jnp.zeros((1, LANES), F32).at[0, offset:offset + v.shape[0]].set(v.astype(F32))


def _ssm_state_to_groups(s):
    nb = s.shape[0]
    hpg = SSM_HEADS // SSM_GROUPS
    s = s.reshape(nb, SSM_GROUPS, hpg, SSM_STATE, SSM_HEAD_DIM)
    return jnp.transpose(s, (0, 1, 3, 2, 4)).reshape(nb, SSM_GROUPS, SSM_STATE, hpg * SSM_HEAD_DIM)


def _ssm_state_from_groups(s):
    nb = s.shape[0]
    hpg = SSM_HEADS // SSM_GROUPS
    s = s.reshape(nb, SSM_GROUPS, SSM_STATE, hpg, SSM_HEAD_DIM)
    return jnp.transpose(s, (0, 1, 3, 2, 4)).reshape(nb, SSM_HEADS, SSM_STATE, SSM_HEAD_DIM)


def _block_rows(t, want):
    return want if t % want == 0 else t


def _layer(x2d, nb, t, state, pr, layer):
    (ssm_conv, ssm_s, hgrn_s, fox_k, fox_v, fox_lf, lru_conv, lru_h) = state
    m = nb * t
    past = 0 if fox_k is None else fox_k.shape[1]
    tm = _block_rows(m, 512)
    tb = _block_rows(t, 256)

    proj, small = _in_proj(x2d, pr['ln1'], pr['w_main'], pr['w_small'], tm)

    ya, ssm_conv_new, ssm_g = _ssd(proj, small, ssm_conv, _ssm_state_to_groups(ssm_s), pr['ssm_conv_w'],
                                   pr['ssm_conv_b'], pr['ssm_dt_bias'], pr['ssm_a_log'], pr['ssm_d'],
                                   pr['ssm_norm'], nb, t, tb)
    ssm_s_new = _ssm_state_from_groups(ssm_g)

    yb, hgrn_s_new = _hgrn(proj, hgrn_s, pr['hgrn_lb_logits'], pr['hgrn_norm'], nb, t, tb, layer)

    yd, lru_conv_new, lru_h_new = _lru(proj, lru_conv, lru_h.reshape(nb, 1, GROUP_W), pr['lru_conv_w'],
                                       pr['lru_conv_b'], pr['lru_w_r'], pr['lru_b_r'], pr['lru_w_i'],
                                       pr['lru_b_i'], pr['lru_lambda'], nb, t, tb, reset_first=(past == 0))

    fq, fk, fv, kb, vb, lf = _foxprep(proj, small, pr['head_sum'], pr['fox_q_norm'], pr['fox_k_norm'],
                                      pr['fox_f_bias'], tm)
    lf = lf.reshape(nb, t, LANES)
    if past:
        past_lf = jnp.pad(fox_lf.astype(F32), ((0, 0), (0, 0), (0, LANES - FOX_HEADS)))
        lf_all = jnp.concatenate([past_lf, lf], axis=1)
    else:
        lf_all = lf
    ltot = past + t
    cum = _seq_cumsum(lf_all, _block_rows(ltot, 512))
    cq = cum[:, past:, :] if past else cum
    ck_t = jnp.transpose(cum[:, :, :FOX_HEADS], (0, 2, 1))
    pk = None if not past else fox_k.reshape(nb, past, GROUP_W)
    pv = None if not past else fox_v.reshape(nb, past, GROUP_W)
    yc = _attention(fq, cq, ck_t, kb, vb, proj, pk, pv, nb, t, _block_rows(t, 256))

    x1 = _out_proj(x2d, (ya, yb, yc, yd), pr['w_out'], tm)
    x2 = _ffn(x1, pr['ln2'], pr['w_gate'], pr['w_up'], pr['w_down'], tm, 512)

    new_state = (ssm_conv_new, ssm_s_new, hgrn_s_new,
                 fk.reshape(nb, t, FOX_HEADS, FOX_HEAD_DIM), fv.reshape(nb, t, FOX_HEADS, FOX_HEAD_DIM),
                 lf[:, :, :FOX_HEADS], lru_conv_new, lru_h_new.reshape(nb, GROUP_W))
    return x2, new_state


def _prep_layer_params(l, ln1, ln2, w_in, w_out, ssm_conv_w, ssm_conv_b, ssm_dt_bias, ssm_a_log, ssm_d,
                       ssm_norm, hgrn_lb_logits, hgrn_norm, fox_q_norm, fox_k_norm, fox_f_bias, lru_conv_w,
                       lru_conv_b, lru_w_r, lru_b_r, lru_w_i, lru_b_i, lru_lambda, w_gate, w_up, w_down):
    w = w_in[l]
    o_dt = GROUP_W + SSM_CONV_DIM
    o_cf = o_dt + SSM_HEADS + 7 * GROUP_W
    w_main = jnp.concatenate([w[:, :o_dt], w[:, o_dt + SSM_HEADS:o_cf], w[:, o_cf + FOX_HEADS:]], axis=1)
    w_small = jnp.zeros((D_MODEL, 2 * LANES), F32)
    w_small = w_small.at[:, :SSM_HEADS].set(w[:, o_dt:o_dt + SSM_HEADS])
    w_small = w_small.at[:, LANES:LANES + FOX_HEADS].set(w[:, o_cf:o_cf + FOX_HEADS])
    hid = jnp.arange(GROUP_W) // FOX_HEAD_DIM
    row = lambda v: v.astype(F32).reshape(1, -1)
    return dict(
        ln1=row(ln1[l]), ln2=row(ln2[l]),
        w_main=w_main.astype(BF16), w_small=w_small.astype(BF16), w_out=w_out[l].astype(BF16),
        ssm_conv_w=ssm_conv_w[l], ssm_conv_b=row(ssm_conv_b[l]),
        ssm_dt_bias=_pad_lanes(ssm_dt_bias[l]), ssm_a_log=_pad_lanes(ssm_a_log[l]),
        ssm_d=row(jnp.repeat(ssm_d[l], SSM_HEAD_DIM)), ssm_norm=row(ssm_norm[l]),
        hgrn_lb_logits=hgrn_lb_logits.astype(F32), hgrn_norm=row(hgrn_norm[l]),
        head_sum=(hid[:, None] == hid[None, :]).astype(BF16),
        fox_q_norm=row(jnp.tile(fox_q_norm[l], FOX_HEADS)), fox_k_norm=row(jnp.tile(fox_k_norm[l], FOX_HEADS)),
        fox_f_bias=_pad_lanes(fox_f_bias[l]),
        lru_conv_w=lru_conv_w[l], lru_conv_b=row(lru_conv_b[l]),
        lru_w_r=lru_w_r[l].astype(BF16), lru_b_r=row(lru_b_r[l]),
        lru_w_i=lru_w_i[l].astype(BF16), lru_b_i=row(lru_b_i[l]), lru_lambda=row(lru_lambda[l]),
        w_gate=w_gate[l].astype(BF16), w_up=w_up[l].astype(BF16), w_down=w_down[l].astype(BF16))


def kernel(x_prompt, x_sample, cache_fox_k, cache_fox_v, cache_fox_logf, state_ssm_conv, state_ssm, state_hgrn,
           state_lru_conv, state_lru, ln1, ln2, w_in, w_out, ssm_conv_w, ssm_conv_b, ssm_dt_bias, ssm_a_log,
           ssm_d, ssm_norm, hgrn_lb_logits, hgrn_norm, fox_q_norm, fox_k_norm, fox_f_bias, lru_conv_w,
           lru_conv_b, lru_w_r, lru_b_r, lru_w_i, lru_b_i, lru_lambda, w_gate, w_up, w_down):
    depth = ln1.shape[0]
    bp, tp, _ = x_prompt.shape
    bs, ts, _ = x_sample.shape
    fresh = (jnp.zeros((bp, SSM_CONV - 1, SSM_CONV_DIM), F32),
             jnp.zeros((bp, SSM_HEADS, SSM_STATE, SSM_HEAD_DIM), F32),
             jnp.zeros((bp, HGRN_HEADS, HGRN_HEAD_DIM, HGRN_HEAD_DIM), F32),
             None, None, None,
             jnp.zeros((bp, LRU_CONV - 1, GROUP_W), F32),
             jnp.zeros((bp, GROUP_W), F32))
    yp = x_prompt.reshape(bp * tp, D_MODEL)
    ys = x_sample.reshape(bs * ts, D_MODEL)
    p_states, s_states = [], []
    for l in range(depth):
        pr = _prep_layer_params(l, ln1, ln2, w_in, w_out, ssm_conv_w, ssm_conv_b, ssm_dt_bias, ssm_a_log, ssm_d,
                                ssm_norm, hgrn_lb_logits, hgrn_norm, fox_q_norm, fox_k_norm, fox_f_bias,
                                lru_conv_w, lru_conv_b, lru_w_r, lru_b_r, lru_w_i, lru_b_i, lru_lambda,
                                w_gate, w_up, w_down)
        yp, st_p = _layer(yp, bp, tp, fresh, pr, l)
        p_states.append(st_p)
        past = (state_ssm_conv[l], state_ssm[l], state_hgrn[l], cache_fox_k[l], cache_fox_v[l],
                cache_fox_logf[l], state_lru_conv[l], state_lru[l])
        ys, st_s = _layer(ys, bs, ts, past, pr, l)
        s_states.append(st_s)
    stack = lambda states: [jnp.stack(parts, axis=0) for parts in zip(*states)]
    return (yp.reshape(bp, tp, D_MODEL), ys.reshape(bs, ts, D_MODEL), *stack(p_states), *stack(s_states))
```

```python
import functools
import math

import jax
import jax.numpy as jnp
import numpy as np
from jax import lax
from jax.experimental import pallas as pl
from jax.experimental.pallas import tpu as pltpu

F32 = jnp.float32
BF16 = jnp.bfloat16

D_MODEL = 2048
GROUP_W = 512
CHUNK = 64
SUB = 16
SSM_HEADS = 8
SSM_HEAD_DIM = 64
SSM_GROUPS = 2
SSM_STATE = 128
SSM_CONV = 4
SSM_CONV_DIM = 1024
HGRN_HEADS = 4
HGRN_HEAD_DIM = 128
HGRN_F_FLOOR = 1e-30
FOX_HEADS = 8
FOX_HEAD_DIM = 64
LRU_BLOCKS = 4
LRU_BLOCK_W = 128
LRU_CONV = 4
LRU_C = 8.0
EPS = 1e-6
NEG = -1e30

LANES = 128
CONV_PAD = 8
VMEM_LIMIT = 56 * 1024 * 1024

COL_A_Z, COL_A_X, COL_A_BC, COL_B_Q, COL_B_F, COL_B_I, COL_B_G = 0, 1, 2, 3, 4, 5, 6
COL_C_Q, COL_C_K, COL_C_V, COL_C_G, COL_D_X, COL_D_G = 7, 8, 9, 10, 11, 12
N_MAIN_BLOCKS = 13


def _cparams(sem):
    return pltpu.CompilerParams(dimension_semantics=sem, vmem_limit_bytes=VMEM_LIMIT)


def _dot(a, b):
    return jnp.dot(a, b, preferred_element_type=F32)


def _dot_nt(a, b):
    return lax.dot_general(a, b, (((1,), (1,)), ((), ())), preferred_element_type=F32)


def _dot_tn(a, b):
    return lax.dot_general(a, b, (((0,), (0,)), ((), ())), preferred_element_type=F32)


def _split3(x):
    hi = x.astype(BF16)
    r = x - hi.astype(F32)
    mid = r.astype(BF16)
    lo = (r - mid.astype(F32)).astype(BF16)
    return hi, mid, lo


def _dot01(m01, x):
    hi, mid, lo = _split3(x)
    return _dot(m01, hi) + _dot(m01, mid) + _dot(m01, lo)


def _dot01_r(x, m01):
    hi, mid, lo = _split3(x)
    return _dot(hi, m01) + _dot(mid, m01) + _dot(lo, m01)


def _tri(n, lower=True):
    r = lax.broadcasted_iota(jnp.int32, (n, n), 0)
    c = lax.broadcasted_iota(jnp.int32, (n, n), 1)
    return (r >= c) if lower else (r <= c)


def _sigmoid(x):
    return jax.nn.sigmoid(x)


def _silu(x):
    return x * jax.nn.sigmoid(x)


def _softplus(x):
    return jnp.maximum(x, 0.0) + jnp.log1p(jnp.exp(-jnp.abs(x)))


def _log_sigmoid(x):
    return -_softplus(-x)


def _in_proj_kernel(x_ref, g_ref, w_ref, ws_ref, o_ref, os_ref, h_scr):
    @pl.when(pl.program_id(1) == 0)
    def _():
        x = x_ref[...]
        ms = jnp.mean(x * x, axis=-1, keepdims=True)
        h = (x * lax.rsqrt(ms + EPS) * g_ref[...]).astype(BF16)
        h_scr[...] = h
        os_ref[...] = _dot(h, ws_ref[...])

    o_ref[...] = _dot(h_scr[...], w_ref[...])


def _in_proj(x2d, ln, w_main, w_small, tm):
    m = x2d.shape[0]
    n_main, n_small = w_main.shape[1], w_small.shape[1]
    return pl.pallas_call(
        _in_proj_kernel,
        grid=(m // tm, n_main // GROUP_W),
        in_specs=[
            pl.BlockSpec((tm, D_MODEL), lambda i, j: (i, 0)),
            pl.BlockSpec((1, D_MODEL), lambda i, j: (0, 0)),
            pl.BlockSpec((D_MODEL, GROUP_W), lambda i, j: (0, j)),
            pl.BlockSpec((D_MODEL, n_small), lambda i, j: (0, 0)),
        ],
        out_specs=[
            pl.BlockSpec((tm, GROUP_W), lambda i, j: (i, j)),
            pl.BlockSpec((tm, n_small), lambda i, j: (i, 0)),
        ],
        out_shape=[jax.ShapeDtypeStruct((m, n_main), F32), jax.ShapeDtypeStruct((m, n_small), F32)],
        scratch_shapes=[pltpu.VMEM((tm, D_MODEL), BF16)],
        compiler_params=_cparams(("parallel", "arbitrary")),
        name="in_proj",
    )(x2d, ln, w_main, w_small)


def _conv_block(xp_scr, cw_ref, cb_ref, tb, kw):
    first = CONV_PAD - (kw - 1)
    y = cb_ref[...]
    for j in range(kw):
        y = y + cw_ref[j:j + 1, :] * xp_scr[pl.ds(first + j, tb), :]
    tail = xp_scr[pl.ds(first + tb, kw - 1), :]
    xp_scr[pl.ds(first, kw - 1), :] = tail
    return y, tail


def _ssd_kernel(z_ref, xlo_ref, xhi_ref, sm_ref, conv0_ref, s0_ref, cw_ref, cb_ref, dtb_ref, alog_ref,
                dexp_ref, nrm_ref, y_ref, convo_ref, so_ref,
                xp_scr, xbc_scr, g_scr, dt_scr, y_scr, s_scr, *, tb):
    c = pl.program_id(1)
    last = pl.num_programs(1) - 1
    hpg = SSM_HEADS // SSM_GROUPS
    gw = hpg * SSM_HEAD_DIM

    @pl.when(c == 0)
    def _():
        s_scr[...] = s0_ref[...]
        xp_scr[0:CONV_PAD, :] = jnp.zeros((CONV_PAD, SSM_CONV_DIM), F32)
        xp_scr[CONV_PAD - (SSM_CONV - 1):CONV_PAD, :] = conv0_ref[...]

    xp_scr[CONV_PAD:CONV_PAD + tb, 0:GROUP_W] = xlo_ref[...]
    xp_scr[CONV_PAD:CONV_PAD + tb, GROUP_W:SSM_CONV_DIM] = xhi_ref[...]
    conv, tail = _conv_block(xp_scr, cw_ref, cb_ref, tb, SSM_CONV)

    @pl.when(c == last)
    def _():
        convo_ref[...] = tail

    xbc_scr[...] = _silu(conv)
    dt = _softplus(sm_ref[...] + dtb_ref[...])
    dt_scr[...] = dt
    g_scr[...] = dt * (-jnp.exp(alog_ref[...]))

    tri_mask = _tri(CHUNK)
    tri_l = tri_mask.astype(BF16)
    tri_u = _tri(CHUNK, lower=False).astype(BF16)
    b_off = GROUP_W
    c_off = GROUP_W + SSM_GROUPS * SSM_STATE

    def chunk(ci, carry):
        rows = pl.ds(pl.multiple_of(ci * CHUNK, CHUNK), CHUNK)
        gc = g_scr[rows, :]
        dtc = dt_scr[rows, :]
        cum = _dot01(tri_l, gc)
        cum_t = _dot01_r(gc.T, tri_u)
        tot = cum[CHUNK - 1:CHUNK, :]
        ys = []
        for grp in range(SSM_GROUPS):
            bm = xbc_scr[rows, b_off + grp * SSM_STATE:b_off + (grp + 1) * SSM_STATE].astype(BF16)
            cm = xbc_scr[rows, c_off + grp * SSM_STATE:c_off + (grp + 1) * SSM_STATE].astype(BF16)
            gmat = _dot_nt(cm, bm)
            s_g = s_scr[grp]
            cs = _dot(cm, s_g.astype(BF16))
            vdec, etot = [], []
            for hh in range(hpg):
                h = grp * hpg + hh
                col = cum[:, h:h + 1]
                row = cum_t[h:h + 1, :]
                dec = jnp.where(tri_mask, jnp.exp(jnp.minimum(col - row, 0.0)), 0.0)
                att = (gmat * dec).astype(BF16)
                xh = xbc_scr[rows, h * SSM_HEAD_DIM:(h + 1) * SSM_HEAD_DIM]
                vh = xh * dtc[:, h:h + 1]
                o = _dot(att, vh.astype(BF16)) + jnp.exp(col) * cs[:, hh * SSM_HEAD_DIM:(hh + 1) * SSM_HEAD_DIM]
                ys.append(o + dexp_ref[:, h * SSM_HEAD_DIM:(h + 1) * SSM_HEAD_DIM] * xh)
                toth = tot[:, h:h + 1]
                vdec.append(vh * jnp.exp(toth - col))
                etot.append(jnp.broadcast_to(jnp.exp(toth), (1, SSM_HEAD_DIM)))
            vdec = jnp.concatenate(vdec, axis=1).astype(BF16)
            etot = jnp.concatenate(etot, axis=1)
            s_scr[grp] = etot * s_g + _dot_tn(bm, vdec)
        y_scr[rows, :] = jnp.concatenate(ys, axis=1)
        return carry

    lax.fori_loop(0, tb // CHUNK, chunk, 0)

    y = y_scr[...] * _silu(z_ref[...])
    outs = []
    for grp in range(SSM_GROUPS):
        yg = y[:, grp * gw:(grp + 1) * gw]
        ms = jnp.mean(yg * yg, axis=-1, keepdims=True)
        outs.append(yg * lax.rsqrt(ms + EPS))
    y_ref[...] = (jnp.concatenate(outs, axis=1) * nrm_ref[...]).astype(y_ref.dtype)

    @pl.when(c == last)
    def _():
        so_ref[...] = s_scr[...]


def _ssd(proj, small, conv0, s0, cw, cb, dtb, alog, dexp, nrm, nb, t, tb):
    nc = t // tb
    hpg = SSM_HEADS // SSM_GROUPS
    sshape = (SSM_GROUPS, SSM_STATE, hpg * SSM_HEAD_DIM)
    row = lambda col: pl.BlockSpec((tb, GROUP_W), lambda b, c: (b * nc + c, col))
    full = lambda a: pl.BlockSpec(a.shape, lambda b, c: (0,) * a.ndim)
    return pl.pallas_call(
        functools.partial(_ssd_kernel, tb=tb),
        grid=(nb, nc),
        in_specs=[row(COL_A_Z), row(COL_A_X), row(COL_A_BC),
                  pl.BlockSpec((tb, LANES), lambda b, c: (b * nc + c, 0)),
                  pl.BlockSpec((None, SSM_CONV - 1, SSM_CONV_DIM), lambda b, c: (b, 0, 0)),
                  pl.BlockSpec((None,) + sshape, lambda b, c: (b, 0, 0, 0)),
                  full(cw), full(cb), full(dtb), full(alog), full(dexp), full(nrm)],
        out_specs=[pl.BlockSpec((tb, GROUP_W), lambda b, c: (b * nc + c, 0)),
                   pl.BlockSpec((None, SSM_CONV - 1, SSM_CONV_DIM), lambda b, c: (b, 0, 0)),
                   pl.BlockSpec((None,) + sshape, lambda b, c: (b, 0, 0, 0))],
        out_shape=[jax.ShapeDtypeStruct((nb * t, GROUP_W), BF16),
                   jax.ShapeDtypeStruct((nb, SSM_CONV - 1, SSM_CONV_DIM), F32),
                   jax.ShapeDtypeStruct((nb,) + sshape, F32)],
        scratch_shapes=[pltpu.VMEM((CONV_PAD + tb, SSM_CONV_DIM), F32),
                        pltpu.VMEM((tb, SSM_CONV_DIM), F32),
                        pltpu.VMEM((tb, LANES), F32),
                        pltpu.VMEM((tb, LANES), F32),
                        pltpu.VMEM((tb, GROUP_W), F32),
                        pltpu.VMEM(sshape, F32)],
        compiler_params=_cparams(("parallel", "arbitrary")),
        name="ssd",
    )(proj, proj, proj, small, conv0, s0, cw, cb, dtb, alog, dexp, nrm)


def _hgrn_kernel(q_ref, f_ref, i_ref, g_ref, s0_ref, lbl_ref, nrm_ref, y_ref, so_ref,
                 c_scr, k_scr, st_scr, *, tb, layer):
    c = pl.program_id(1)
    last = pl.num_programs(1) - 1
    hd = HGRN_HEAD_DIM
    nsub = CHUNK // SUB

    @pl.when(c == 0)
    def _():
        for h in range(HGRN_HEADS):
            st_scr[h] = s0_ref[h].T

    ll = lbl_ref[...]
    e = jnp.exp(ll - jnp.max(ll, axis=0, keepdims=True))
    p = e / jnp.sum(e, axis=0, keepdims=True)
    cs = p[0:1, :]
    for l in range(1, layer + 1):
        cs = cs + p[l:l + 1, :]
    lb = cs - p[0:1, :]

    tri_l = _tri(CHUNK).astype(BF16)
    rowid = lax.broadcasted_iota(jnp.int32, (CHUNK, 1), 0)
    subid = lax.broadcasted_iota(jnp.int32, (SUB, 1), 0)

    def chunk(ci, carry):
        r0 = pl.multiple_of(ci * CHUNK, CHUNK)
        rows = pl.ds(r0, CHUNK)
        fr = f_ref[rows, :]
        fg = lb + (1.0 - lb) * _sigmoid(fr)
        logf = jnp.log(jnp.maximum(fg, HGRN_F_FLOOR))
        hk = (1.0 - lb) * _sigmoid(-fr)
        hq = _silu(q_ref[rows, :])
        v = i_ref[rows, :]
        vb = v.astype(BF16)
        cum = _dot01(tri_l, logf)
        tot = cum[CHUNK - 1:CHUNK, :]
        c_scr[...] = cum
        k_scr[...] = hk
        qe = (hq * jnp.exp(cum)).astype(BF16)
        kd = (hk * jnp.exp(tot - cum)).astype(BF16)
        etot = jnp.exp(tot)

        qcat, kcat = [], []
        for j in range(nsub - 1):
            ej = cum[(j + 1) * SUB - 1:(j + 1) * SUB, :]
            qcat.append(jnp.where(rowid >= (j + 1) * SUB, hq * jnp.exp(jnp.minimum(cum - ej, 0.0)), 0.0).astype(BF16))
            in_j = (rowid >= j * SUB) & (rowid < (j + 1) * SUB)
            kcat.append(jnp.where(in_j, hk * jnp.exp(jnp.minimum(ej - cum, 0.0)), 0.0).astype(BF16))

        o_heads = []
        for h in range(HGRN_HEADS):
            hs = slice(h * hd, (h + 1) * hd)
            qc = jnp.concatenate([q[:, hs] for q in qcat], axis=1)
            kc = jnp.concatenate([k[:, hs] for k in kcat], axis=1)
            a_off = _dot_nt(qc, kc)
            st = st_scr[h]
            o_heads.append(_dot(a_off.astype(BF16), vb[:, hs]) + _dot_nt(qe[:, hs], st.astype(BF16)))
            st_scr[h] = st * etot[:, hs] + _dot_tn(vb[:, hs], kd[:, hs])
        o = jnp.concatenate(o_heads, axis=1)

        o_sub = []
        for i in range(nsub):
            base = i * SUB
            ci_ = cum[base:base + SUB, :]
            qi = hq[base:base + SUB, :]
            acc = [jnp.zeros((SUB, hd), F32) for _ in range(HGRN_HEADS)]
            for j in range(SUB):
                crow = c_scr[pl.ds(base + j, 1), :]
                krow = k_scr[pl.ds(base + j, 1), :]
                vrow = i_ref[pl.ds(r0 + base + j, 1), :]
                x = jnp.where(subid >= j, qi * jnp.exp(jnp.minimum(ci_ - crow, 0.0)) * krow, 0.0)
                for h in range(HGRN_HEADS):
                    hs = slice(h * hd, (h + 1) * hd)
                    a = jnp.sum(x[:, hs], axis=-1, keepdims=True)
                    acc[h] = acc[h] + a * vrow[:, hs]
            o_sub.append(jnp.concatenate(acc, axis=1))
        o = o + jnp.concatenate(o_sub, axis=0)

        outs = []
        for h in range(HGRN_HEADS):
            oh = o[:, h * hd:(h + 1) * hd]
            ms = jnp.mean(oh * oh, axis=-1, keepdims=True)
            outs.append(oh * lax.rsqrt(ms + EPS))
        y = jnp.concatenate(outs, axis=1) * nrm_ref[...] * _silu(g_ref[rows, :])
        y_ref[rows, :] = y.astype(y_ref.dtype)
        return carry

    lax.fori_loop(0, tb // CHUNK, chunk, 0)

    @pl.when(c == last)
    def _():
        for h in range(HGRN_HEADS):
            so_ref[h] = st_scr[h].T


def _hgrn(proj, s0, lbl, nrm, nb, t, tb, layer):
    nc = t // tb
    sshape = (HGRN_HEADS, HGRN_HEAD_DIM, HGRN_HEAD_DIM)
    row = lambda col: pl.BlockSpec((tb, GROUP_W), lambda b, c: (b * nc + c, col))
    full = lambda a: pl.BlockSpec(a.shape, lambda b, c: (0,) * a.ndim)
    return pl.pallas_call(
        functools.partial(_hgrn_kernel, tb=tb, layer=layer),
        grid=(nb, nc),
        in_specs=[row(COL_B_Q), row(COL_B_F), row(COL_B_I), row(COL_B_G),
                  pl.BlockSpec((None,) + sshape, lambda b, c: (b, 0, 0, 0)),
                  full(lbl), full(nrm)],
        out_specs=[pl.BlockSpec((tb, GROUP_W), lambda b, c: (b * nc + c, 0)),
                   pl.BlockSpec((None,) + sshape, lambda b, c: (b, 0, 0, 0))],
        out_shape=[jax.ShapeDtypeStruct((nb * t, GROUP_W), BF16),
                   jax.ShapeDtypeStruct((nb,) + sshape, F32)],
        scratch_shapes=[pltpu.VMEM((CHUNK, GROUP_W), F32),
                        pltpu.VMEM((CHUNK, GROUP_W), F32),
                        pltpu.VMEM(sshape, F32)],
        compiler_params=_cparams(("parallel", "arbitrary")),
        name="hgrn2",
    )(proj, proj, proj, proj, s0, lbl, nrm)


def _lru_kernel(x_ref, g_ref, conv0_ref, h0_ref, cw_ref, cb_ref, wr_ref, br_ref, wi_ref, bi_ref, lam_ref,
                y_ref, convo_ref, ho_ref, xp_scr, h_scr, *, tb, reset_first):
    c = pl.program_id(1)
    last = pl.num_programs(1) - 1

    @pl.when(c == 0)
    def _():
        h_scr[...] = h0_ref[...]
        xp_scr[0:CONV_PAD, :] = jnp.zeros((CONV_PAD, GROUP_W), F32)
        xp_scr[CONV_PAD - (LRU_CONV - 1):CONV_PAD, :] = conv0_ref[...]

    xp_scr[CONV_PAD:CONV_PAD + tb, :] = x_ref[...]
    xc, tail = _conv_block(xp_scr, cw_ref, cb_ref, tb, LRU_CONV)

    @pl.when(c == last)
    def _():
        convo_ref[...] = tail

    xcb = xc.astype(BF16)

    def blockdiag(w_ref):
        return jnp.concatenate(
            [_dot(xcb[:, n * LRU_BLOCK_W:(n + 1) * LRU_BLOCK_W], w_ref[n]) for n in range(LRU_BLOCKS)], axis=1)

    r = _sigmoid(blockdiag(wr_ref) + br_ref[...])
    gi = _sigmoid(blockdiag(wi_ref) + bi_ref[...])
    log_a = LRU_C * r * _log_sigmoid(lam_ref[...])
    a = jnp.exp(log_a)
    mult = jnp.sqrt(-jnp.tanh(log_a) * (a * a + 1.0))
    rowid = lax.broadcasted_iota(jnp.int32, (tb, 1), 0)
    first_row = rowid == 0
    if reset_first:
        rst = first_row & (c == 0)
        a = jnp.where(rst, 0.0, a)
        mult = jnp.where(rst, 1.0, mult)
    u = mult * gi * xc
    u = u + jnp.where(first_row, a * h_scr[...], 0.0)

    av, bv = a, u
    d = 1
    while d < tb:
        m = rowid >= d
        a_sh = pltpu.roll(av, d, axis=0)
        b_sh = pltpu.roll(bv, d, axis=0)
        bv = jnp.where(m, av * b_sh + bv, bv)
        av = jnp.where(m, av * a_sh, av)
        d *= 2
    h_new = bv[tb - 1:tb, :]
    h_scr[...] = h_new
    y_ref[...] = (bv * jax.nn.gelu(g_ref[...])).astype(y_ref.dtype)

    @pl.when(c == last)
    def _():
        ho_ref[...] = h_new


def _lru(proj, conv0, h0, cw, cb, wr, br, wi, bi, lam, nb, t, tb, reset_first):
    nc = t // tb
    row = lambda col: pl.BlockSpec((tb, GROUP_W), lambda b, c: (b * nc + c, col))
    full = lambda a: pl.BlockSpec(a.shape, lambda b, c: (0,) * a.ndim)
    return pl.pallas_call(
        functools.partial(_lru_kernel, tb=tb, reset_first=reset_first),
        grid=(nb, nc),
        in_specs=[row(COL_D_X), row(COL_D_G),
                  pl.BlockSpec((None, LRU_CONV - 1, GROUP_W), lambda b, c: (b, 0, 0)),
                  pl.BlockSpec((None, 1, GROUP_W), lambda b, c: (b, 0, 0)),
                  full(cw), full(cb), full(wr), full(br), full(wi), full(bi), full(lam)],
        out_specs=[pl.BlockSpec((tb, GROUP_W), lambda b, c: (b * nc + c, 0)),
                   pl.BlockSpec((None, LRU_CONV - 1, GROUP_W), lambda b, c: (b, 0, 0)),
                   pl.BlockSpec((None, 1, GROUP_W), lambda b, c: (b, 0, 0))],
        out_shape=[jax.ShapeDtypeStruct((nb * t, GROUP_W), BF16),
                   jax.ShapeDtypeStruct((nb, LRU_CONV - 1, GROUP_W), F32),
                   jax.ShapeDtypeStruct((nb, 1, GROUP_W), F32)],
        scratch_shapes=[pltpu.VMEM((CONV_PAD + tb, GROUP_W), F32),
                        pltpu.VMEM((1, GROUP_W), F32)],
        compiler_params=_cparams(("parallel", "arbitrary")),
        name="rglru",
    )(proj, proj, conv0, h0, cw, cb, wr, br, wi, bi, lam)


def _gate_kernel(*refs, tb, past):
    if past:
        sm_ref, fb_ref, plf_ref, lf_ref, cum_ref, cump_ref, carry_scr = refs
    else:
        sm_ref, fb_ref, lf_ref, cum_ref, carry_scr = refs

    @pl.when(pl.program_id(1) == 0)
    def _():
        if past:
            cp = _dot01(_tri(past).astype(BF16), plf_ref[...])
            cump_ref[...] = cp
            carry_scr[...] = cp[past - 1:past, :]
        else:
            carry_scr[...] = jnp.zeros_like(carry_scr)

    lf = _log_sigmoid(sm_ref[...] + fb_ref[...])
    lf_ref[...] = lf
    cum = _dot01(_tri(tb).astype(BF16), lf) + carry_scr[...]
    cum_ref[...] = cum
    carry_scr[...] = cum[tb - 1:tb, :]


def _gate(small, fb, past_lf, nb, t, tb):
    nc = t // tb
    past = 0 if past_lf is None else past_lf.shape[1]
    rows = pl.BlockSpec((tb, LANES), lambda b, c: (b * nc + c, 0))
    in_specs = [pl.BlockSpec((tb, LANES), lambda b, c: (b * nc + c, 1)),
                pl.BlockSpec(fb.shape, lambda b, c: (0, 0))]
    out_specs = [rows, rows]
    out_shape = [jax.ShapeDtypeStruct((nb * t, LANES), F32)] * 2
    args = [small, fb]
    if past:
        in_specs.append(pl.BlockSpec((None, past, LANES), lambda b, c: (b, 0, 0)))
        out_specs.append(pl.BlockSpec((None, past, LANES), lambda b, c: (b, 0, 0)))
        out_shape.append(jax.ShapeDtypeStruct((nb, past, LANES), F32))
        args.append(past_lf)
    return pl.pallas_call(
        functools.partial(_gate_kernel, tb=tb, past=past),
        grid=(nb, nc),
        in_specs=in_specs, out_specs=out_specs, out_shape=out_shape,
        scratch_shapes=[pltpu.VMEM((1, LANES), F32)],
        compiler_params=_cparams(("parallel", "arbitrary")),
        name="fox_gate",
    )(*args)


def _head_norm(x, gain, hsum):
    ms = _dot01_r(x * x, hsum) * (1.0 / FOX_HEAD_DIM)
    return x * lax.rsqrt(ms + EPS) * gain


def _foxprep_kernel(q_ref, k_ref, v_ref, hsum_ref, qn_ref, kn_ref, fq_ref, fk_ref, fv_ref, kb_ref, vb_ref):
    hsum = hsum_ref[...]
    fq_ref[...] = (_head_norm(q_ref[...], qn_ref[...], hsum) * (FOX_HEAD_DIM ** -0.5)).astype(BF16)
    fk = _head_norm(k_ref[...], kn_ref[...], hsum)
    fk_ref[...] = fk
    kb_ref[...] = fk.astype(BF16)
    v = v_ref[...]
    fv_ref[...] = v
    vb_ref[...] = v.astype(BF16)


def _foxprep(proj, hsum, qn, kn, tb):
    m = proj.shape[0]
    row = lambda col: pl.BlockSpec((tb, GROUP_W), lambda i: (i, col))
    full = lambda a: pl.BlockSpec(a.shape, lambda i: (0,) * a.ndim)
    out_row = pl.BlockSpec((tb, GROUP_W), lambda i: (i, 0))
    return pl.pallas_call(
        _foxprep_kernel,
        grid=(m // tb,),
        in_specs=[row(COL_C_Q), row(COL_C_K), row(COL_C_V), full(hsum), full(qn), full(kn)],
        out_specs=[out_row] * 5,
        out_shape=[jax.ShapeDtypeStruct((m, GROUP_W), BF16),
                   jax.ShapeDtypeStruct((m, GROUP_W), F32),
                   jax.ShapeDtypeStruct((m, GROUP_W), F32),
                   jax.ShapeDtypeStruct((m, GROUP_W), BF16),
                   jax.ShapeDtypeStruct((m, GROUP_W), BF16)],
        compiler_params=_cparams(("parallel",)),
        name="fox_prep",
    )(proj, proj, proj, hsum, qn, kn)


AUG_W = FOX_HEADS * LANES
N_PIECES = 3


def _aug_constants():
    pq = np.zeros((N_PIECES * LANES, AUG_W), np.float32)
    pk = np.zeros((N_PIECES * LANES, AUG_W), np.float32)
    ones_q = np.zeros((1, AUG_W), np.float32)
    ones_k = np.zeros((1, AUG_W), np.float32)
    ones_v = np.zeros((1, AUG_W), np.float32)
    head = np.zeros((1, AUG_W), np.float32)
    for h in range(FOX_HEADS):
        own = h * LANES + FOX_HEAD_DIM * (h % 2)
        other = h * LANES + FOX_HEAD_DIM * (1 - h % 2)
        head[0, own:own + FOX_HEAD_DIM] = 1.0
        for piece in range(N_PIECES):
            pq[piece * LANES + h, other + piece] = 1.0
            ones_k[0, other + piece] = 1.0
            ones_q[0, other + N_PIECES + piece] = 1.0
            pk[piece * LANES + h, other + N_PIECES + piece] = -1.0
        ones_v[0, other] = 1.0
    return dict(aug_pq=jnp.asarray(pq, BF16), aug_pk=jnp.asarray(pk, BF16), aug_ones_q=jnp.asarray(ones_q),
                aug_ones_k=jnp.asarray(ones_k), aug_ones_v=jnp.asarray(ones_v), aug_head=jnp.asarray(head))


def _foxprep_aug_kernel(q_ref, k_ref, v_ref, cum_ref, hsum_ref, qn_ref, kn_ref, pq_ref, pk_ref, oq_ref, ok_ref,
                        ov_ref, head_ref, fk_ref, fv_ref, qa_ref, ka_ref, va_ref):
    hsum = hsum_ref[...]
    fq = _head_norm(q_ref[...], qn_ref[...], hsum) * (FOX_HEAD_DIM ** -0.5)
    fk = _head_norm(k_ref[...], kn_ref[...], hsum)
    v = v_ref[...]
    fk_ref[...] = fk
    fv_ref[...] = v
    pieces = jnp.concatenate(_split3(cum_ref[...]), axis=1)
    own = head_ref[...] > 0.5

    def per_head(x):
        return jnp.concatenate([x[:, (h // 2) * LANES:(h // 2 + 1) * LANES] for h in range(FOX_HEADS)], axis=1)

    qa = jnp.where(own, per_head(fq), _dot(pieces, pq_ref[...]) + oq_ref[...]).astype(BF16)
    ka = jnp.where(own, per_head(fk), _dot(pieces, pk_ref[...]) + ok_ref[...]).astype(BF16)
    va = jnp.where(own, per_head(v), ov_ref[...]).astype(BF16)
    for h in range(FOX_HEADS):
        hs = slice(h * LANES, (h + 1) * LANES)
        qa_ref[h] = qa[:, hs]
        ka_ref[h] = ka[:, hs]
        va_ref[h] = va[:, hs]


def _foxprep_aug(proj, cum, hsum, qn, kn, aug, tb):
    m = proj.shape[0]
    row = lambda col: pl.BlockSpec((tb, GROUP_W), lambda i: (i, col))
    full = lambda a: pl.BlockSpec(a.shape, lambda i: (0,) * a.ndim)
    out_row = pl.BlockSpec((tb, GROUP_W), lambda i: (i, 0))
    out_aug = pl.BlockSpec((FOX_HEADS, tb, LANES), lambda i: (0, i, 0))
    consts = [aug['aug_pq'], aug['aug_pk'], aug['aug_ones_q'], aug['aug_ones_k'], aug['aug_ones_v'],
              aug['aug_head']]
    return pl.pallas_call(
        _foxprep_aug_kernel,
        grid=(m // tb,),
        in_specs=[row(COL_C_Q), row(COL_C_K), row(COL_C_V), pl.BlockSpec((tb, LANES), lambda i: (i, 0)),
                  full(hsum), full(qn), full(kn)] + [full(a) for a in consts],
        out_specs=[out_row, out_row, out_aug, out_aug, out_aug],
        out_shape=[jax.ShapeDtypeStruct((m, GROUP_W), F32),
                   jax.ShapeDtypeStruct((m, GROUP_W), F32)]
                  + [jax.ShapeDtypeStruct((FOX_HEADS, m, LANES), BF16)] * 3,
        compiler_params=_cparams(("parallel",)),
        name="fox_prep_aug",
    )(proj, proj, proj, cum, hsum, qn, kn, *consts)


def _attn_aug_kernel(q_ref, k_ref, v_ref, g_ref, y_ref, *, tq, hg):
    qi = pl.program_id(2)
    causal = _tri(tq)
    low = lax.broadcasted_iota(jnp.int32, (1, LANES), 1) < FOX_HEAD_DIM
    qs = [q_ref[i] for i in range(hg)]

    def step(j, carry, masked):
        rows = pl.ds(pl.multiple_of(j * tq, tq), tq)
        new = []
        for i in range(hg):
            m, acc = carry[i]
            s = _dot_nt(qs[i], k_ref[i, rows, :])
            if masked:
                s = jnp.where(causal, s, NEG)
            m_new = jnp.maximum(m, jnp.max(s, axis=-1, keepdims=True))
            p = jnp.exp(s - m_new).astype(BF16)
            acc = jnp.exp(m - m_new) * acc + _dot(p, v_ref[i, rows, :])
            new.append((m_new, acc))
        return tuple(new)

    init = tuple((jnp.full((tq, 1), NEG, F32), jnp.zeros((tq, LANES), F32)) for _ in range(hg))
    carry = lax.fori_loop(0, qi, lambda j, c: step(j, c, False), init)
    carry = step(qi, carry, True)
    outs = []
    for i in range(0, hg, 2):
        acc_e, acc_o = carry[i][1], carry[i + 1][1]
        out_e = acc_e / acc_e[:, FOX_HEAD_DIM:FOX_HEAD_DIM + 1]
        out_o = acc_o / acc_o[:, 0:1]
        outs.append(jnp.where(low, out_e, out_o))
    y = jnp.concatenate(outs, axis=1) * _sigmoid(g_ref[...])
    y_ref[...] = y.astype(y_ref.dtype)


def _attention_aug(qa, ka, va, proj, nb, t, tq, hg):
    nq = t // tq
    ngrp = FOX_HEADS // hg
    wout = hg * FOX_HEAD_DIM
    gate_col0 = COL_C_G * GROUP_W // wout
    return pl.pallas_call(
        functools.partial(_attn_aug_kernel, tq=tq, hg=hg),
        grid=(nb, ngrp, nq),
        in_specs=[pl.BlockSpec((hg, tq, LANES), lambda b, g, i: (g, b * nq + i, 0)),
                  pl.BlockSpec((hg, t, LANES), lambda b, g, i: (g, b, 0)),
                  pl.BlockSpec((hg, t, LANES), lambda b, g, i: (g, b, 0)),
                  pl.BlockSpec((tq, wout), lambda b, g, i: (b * nq + i, gate_col0 + g))],
        out_specs=pl.BlockSpec((tq, wout), lambda b, g, i: (b * nq + i, g)),
        out_shape=jax.ShapeDtypeStruct((nb * t, GROUP_W), BF16),
        compiler_params=_cparams(("parallel", "parallel", "arbitrary")),
        name="fox_attention_aug",
    )(qa, ka, va, proj)


def _aligned(x, m):
    return x if isinstance(x, int) else pl.multiple_of(x, m)


def _attn_kernel(*refs, tq, past, tkp, single_block):
    if past:
        q_ref, cq_ref, ck_ref, k_ref, v_ref, g_ref, pk_ref, pv_ref, y_ref = refs
    else:
        q_ref, cq_ref, ck_ref, k_ref, v_ref, g_ref, y_ref = refs
    qi = 0 if single_block else pl.program_id(1)
    lane = lax.broadcasted_iota(jnp.int32, (1, LANES), 1)
    low = lane < FOX_HEAD_DIM
    causal = _tri(tq)
    zero = jnp.zeros((), BF16)

    def step(qpad, cqh, kblk, vblk, ck_row, mask, carry):
        m, l, acc = carry
        s = _dot_nt(qpad, kblk) + cqh - ck_row
        if mask is not None:
            s = jnp.where(mask, s, NEG)
        m_new = jnp.maximum(m, jnp.max(s, axis=-1, keepdims=True))
        alpha = jnp.exp(m - m_new)
        p = jnp.exp(s - m_new)
        l = alpha * l + jnp.sum(p, axis=-1, keepdims=True)
        acc = alpha * acc + _dot(p.astype(BF16), vblk)
        return m_new, l, acc

    outs = []
    for pair in range(FOX_HEADS // 2):
        ps = slice(pair * LANES, (pair + 1) * LANES)
        qp = q_ref[:, ps]
        halves = []
        for half in range(2):
            h = 2 * pair + half
            qpad = jnp.where(low if half == 0 else ~low, qp, zero)
            cqh = cq_ref[:, h:h + 1]
            carry = (jnp.full((tq, 1), NEG, F32), jnp.zeros((tq, 1), F32), jnp.zeros((tq, LANES), F32))
            for j in range(past // tkp if past else 0):
                kblk = pk_ref[j * tkp:(j + 1) * tkp, ps].astype(BF16)
                vblk = pv_ref[j * tkp:(j + 1) * tkp, ps].astype(BF16)
                carry = step(qpad, cqh, kblk, vblk, ck_ref[h:h + 1, j * tkp:(j + 1) * tkp], None, carry)

            def body(j, carry):
                r0 = pl.multiple_of(j * tq, tq)
                ck_row = ck_ref[h:h + 1, pl.ds(pl.multiple_of(past + j * tq, tq), tq)]
                return step(qpad, cqh, k_ref[pl.ds(r0, tq), ps], v_ref[pl.ds(r0, tq), ps], ck_row, None, carry)

            if not single_block:
                carry = lax.fori_loop(0, qi, body, carry)
            r0 = _aligned(qi * tq, tq)
            ck_row = ck_ref[h:h + 1, pl.ds(_aligned(past + qi * tq, tq), tq)]
            _, l, acc = step(qpad, cqh, k_ref[pl.ds(r0, tq), ps], v_ref[pl.ds(r0, tq), ps], ck_row, causal, carry)
            halves.append(acc / l)
        outs.append(jnp.where(low, halves[0], halves[1]))
    y = jnp.concatenate(outs, axis=1) * _sigmoid(g_ref[...])
    y_ref[...] = y.astype(y_ref.dtype)


def _attention(fq, cq, ck_t, kb, vb, proj, past_k, past_v, nb, t, tq):
    nq = t // tq
    past = 0 if past_k is None else past_k.shape[1]
    tkp = min(past, 256) if past else 0
    ltot = ck_t.shape[-1]
    in_specs = [pl.BlockSpec((tq, GROUP_W), lambda b, i: (b * nq + i, 0)),
                pl.BlockSpec((None, tq, LANES), lambda b, i: (b, i, 0)),
                pl.BlockSpec((None, FOX_HEADS, ltot), lambda b, i: (b, 0, 0)),
                pl.BlockSpec((None, t, GROUP_W), lambda b, i: (b, 0, 0)),
                pl.BlockSpec((None, t, GROUP_W), lambda b, i: (b, 0, 0)),
                pl.BlockSpec((tq, GROUP_W), lambda b, i: (b * nq + i, COL_C_G))]
    args = [fq, cq, ck_t, kb.reshape(nb, t, GROUP_W), vb.reshape(nb, t, GROUP_W), proj]
    if past:
        in_specs += [pl.BlockSpec((None, past, GROUP_W), lambda b, i: (b, 0, 0))] * 2
        args += [past_k, past_v]
    return pl.pallas_call(
        functools.partial(_attn_kernel, tq=tq, past=past, tkp=tkp, single_block=(nq == 1)),
        grid=(nb, nq),
        in_specs=in_specs,
        out_specs=pl.BlockSpec((tq, GROUP_W), lambda b, i: (b * nq + i, 0)),
        out_shape=jax.ShapeDtypeStruct((nb * t, GROUP_W), BF16),
        compiler_params=_cparams(("parallel", "arbitrary")),
        name="fox_attention",
    )(*args)


def _out_proj_kernel(x_ref, ya_ref, yb_ref, yc_ref, yd_ref, w_ref, o_ref):
    acc = x_ref[...]
    for n, y_ref in enumerate((ya_ref, yb_ref, yc_ref, yd_ref)):
        acc = acc + _dot(y_ref[...], w_ref[n * GROUP_W:(n + 1) * GROUP_W, :])
    o_ref[...] = acc


def _out_proj(x2d, ys, w_out, tm):
    m = x2d.shape[0]
    yspec = pl.BlockSpec((tm, GROUP_W), lambda i: (i, 0))
    return pl.pallas_call(
        _out_proj_kernel,
        grid=(m // tm,),
        in_specs=[pl.BlockSpec((tm, D_MODEL), lambda i: (i, 0)), yspec, yspec, yspec, yspec,
                  pl.BlockSpec(w_out.shape, lambda i: (0, 0))],
        out_specs=pl.BlockSpec((tm, D_MODEL), lambda i: (i, 0)),
        out_shape=jax.ShapeDtypeStruct((m, D_MODEL), F32),
        compiler_params=_cparams(("parallel",)),
        name="out_proj",
    )(x2d, *ys, w_out)


def _ffn_kernel(x_ref, g_ref, wg_ref, wu_ref, wd_ref, o_ref, h_scr, acc_scr):
    f = pl.program_id(1)

    @pl.when(f == 0)
    def _():
        x = x_ref[...]
        ms = jnp.mean(x * x, axis=-1, keepdims=True)
        h_scr[...] = (x * lax.rsqrt(ms + EPS) * g_ref[...]).astype(BF16)
        acc_scr[...] = x

    h = h_scr[...]
    a = _silu(_dot(h, wg_ref[...])) * _dot(h, wu_ref[...])
    acc_scr[...] += _dot(a.astype(BF16), wd_ref[...])

    @pl.when(f == pl.num_programs(1) - 1)
    def _():
        o_ref[...] = acc_scr[...]


def _ffn(x2d, ln, wg, wu, wd, tm, tf):
    m = x2d.shape[0]
    d_ff = wg.shape[1]
    return pl.pallas_call(
        _ffn_kernel,
        grid=(m // tm, d_ff // tf),
        in_specs=[pl.BlockSpec((tm, D_MODEL), lambda i, f: (i, 0)),
                  pl.BlockSpec((1, D_MODEL), lambda i, f: (0, 0)),
                  pl.BlockSpec((D_MODEL, tf), lambda i, f: (0, f)),
                  pl.BlockSpec((D_MODEL, tf), lambda i, f: (0, f)),
                  pl.BlockSpec((tf, D_MODEL), lambda i, f: (f, 0))],
        out_specs=pl.BlockSpec((tm, D_MODEL), lambda i, f: (i, 0)),
        out_shape=jax.ShapeDtypeStruct((m, D_MODEL), F32),
        scratch_shapes=[pltpu.VMEM((tm, D_MODEL), BF16), pltpu.VMEM((tm, D_MODEL), F32)],
        compiler_params=_cparams(("parallel", "arbitrary")),
        name="swiglu_ffn",
    )(x2d, ln, wg, wu, wd)


def _pad_lanes(v, offset=0):
    return jnp.zeros((1, LANES), F32).at[0, offset:offset + v.shape[0]].set(v.astype(F32))


def _ssm_state_to_groups(s):
    nb = s.shape[0]
    hpg = SSM_HEADS // SSM_GROUPS
    s = s.reshape(nb, SSM_GROUPS, hpg, SSM_STATE, SSM_HEAD_DIM)
    return jnp.transpose(s, (0, 1, 3, 2, 4)).reshape(nb, SSM_GROUPS, SSM_STATE, hpg * SSM_HEAD_DIM)


def _ssm_state_from_groups(s):
    nb = s.shape[0]
    hpg = SSM_HEADS // SSM_GROUPS
    s = s.reshape(nb, SSM_GROUPS, SSM_STATE, hpg, SSM_HEAD_DIM)
    return jnp.transpose(s, (0, 1, 3, 2, 4)).reshape(nb, SSM_HEADS, SSM_STATE, SSM_HEAD_DIM)


def _block_rows(t, want):
    return want if t % want == 0 else t


def _layer(x2d, nb, t, state, pr, layer):
    (ssm_conv, ssm_s, hgrn_s, fox_k, fox_v, fox_lf, lru_conv, lru_h) = state
    m = nb * t
    past = 0 if fox_k is None else fox_k.shape[1]
    tm = _block_rows(m, 512)
    tb = _block_rows(t, 256)

    proj, small = _in_proj(x2d, pr['ln1'], pr['w_main'], pr['w_small'], tm)

    ya, ssm_conv_new, ssm_g = _ssd(proj, small, ssm_conv, _ssm_state_to_groups(ssm_s), pr['ssm_conv_w'],
                                   pr['ssm_conv_b'], pr['ssm_dt_bias'], pr['ssm_a_log'], pr['ssm_d'],
                                   pr['ssm_norm'], nb, t, tb)
    ssm_s_new = _ssm_state_from_groups(ssm_g)

    yb, hgrn_s_new = _hgrn(proj, hgrn_s, pr['hgrn_lb_logits'], pr['hgrn_norm'], nb, t, tb, layer)

    yd, lru_conv_new, lru_h_new = _lru(proj, lru_conv, lru_h.reshape(nb, 1, GROUP_W), pr['lru_conv_w'],
                                       pr['lru_conv_b'], pr['lru_w_r'], pr['lru_b_r'], pr['lru_w_i'],
                                       pr['lru_b_i'], pr['lru_lambda'], nb, t, tb, reset_first=(past == 0))

    if past:
        past_lf = jnp.pad(fox_lf.astype(F32), ((0, 0), (0, 0), (0, LANES - FOX_HEADS)))
        lf, cum, cum_past = _gate(small, pr['fox_f_bias'], past_lf, nb, t, t)
        fq, fk, fv, kb, vb = _foxprep(proj, pr['head_sum'], pr['fox_q_norm'], pr['fox_k_norm'], tm)
        cq = cum.reshape(nb, t, LANES)
        ck_t = jnp.transpose(jnp.concatenate([cum_past, cq], axis=1)[:, :, :FOX_HEADS], (0, 2, 1))
        yc = _attention(fq, cq, ck_t, kb, vb, proj, fox_k.reshape(nb, past, GROUP_W),
                        fox_v.reshape(nb, past, GROUP_W), nb, t, t)
    else:
        lf, cum = _gate(small, pr['fox_f_bias'], None, nb, t, _block_rows(t, 512))
        fk, fv, qa, ka, va = _foxprep_aug(proj, cum, pr['head_sum'], pr['fox_q_norm'], pr['fox_k_norm'],
                                          pr['aug'], tm)
        yc = _attention_aug(qa, ka, va, proj, nb, t, _block_rows(t, 512), FOX_HEADS)
    lf = lf.reshape(nb, t, LANES)

    x1 = _out_proj(x2d, (ya, yb, yc, yd), pr['w_out'], tm)
    x2 = _ffn(x1, pr['ln2'], pr['w_gate'], pr['w_up'], pr['w_down'], tm, 512)

    new_state = (ssm_conv_new, ssm_s_new, hgrn_s_new,
                 fk.reshape(nb, t, FOX_HEADS, FOX_HEAD_DIM), fv.reshape(nb, t, FOX_HEADS, FOX_HEAD_DIM),
                 lf[:, :, :FOX_HEADS], lru_conv_new, lru_h_new.reshape(nb, GROUP_W))
    return x2, new_state


def _prep_layer_params(l, ln1, ln2, w_in, w_out, ssm_conv_w, ssm_conv_b, ssm_dt_bias, ssm_a_log, ssm_d,
                       ssm_norm, hgrn_lb_logits, hgrn_norm, fox_q_norm, fox_k_norm, fox_f_bias, lru_conv_w,
                       lru_conv_b, lru_w_r, lru_b_r, lru_w_i, lru_b_i, lru_lambda, w_gate, w_up, w_down):
    w = w_in[l]
    o_dt = GROUP_W + SSM_CONV_DIM
    o_cf = o_dt + SSM_HEADS + 7 * GROUP_W
    w_main = jnp.concatenate([w[:, :o_dt], w[:, o_dt + SSM_HEADS:o_cf], w[:, o_cf + FOX_HEADS:]], axis=1)
    w_small = jnp.zeros((D_MODEL, 2 * LANES), F32)
    w_small = w_small.at[:, :SSM_HEADS].set(w[:, o_dt:o_dt + SSM_HEADS])
    w_small = w_small.at[:, LANES:LANES + FOX_HEADS].set(w[:, o_cf:o_cf + FOX_HEADS])
    hid = jnp.arange(GROUP_W) // FOX_HEAD_DIM
    row = lambda v: v.astype(F32).reshape(1, -1)
    return dict(
        ln1=row(ln1[l]), ln2=row(ln2[l]),
        w_main=w_main.astype(BF16), w_small=w_small.astype(BF16), w_out=w_out[l].astype(BF16),
        ssm_conv_w=ssm_conv_w[l], ssm_conv_b=row(ssm_conv_b[l]),
        ssm_dt_bias=_pad_lanes(ssm_dt_bias[l]), ssm_a_log=_pad_lanes(ssm_a_log[l]),
        ssm_d=row(jnp.repeat(ssm_d[l], SSM_HEAD_DIM)), ssm_norm=row(ssm_norm[l]),
        hgrn_lb_logits=hgrn_lb_logits.astype(F32), hgrn_norm=row(hgrn_norm[l]),
        head_sum=(hid[:, None] == hid[None, :]).astype(BF16), aug=_aug_constants(),
        fox_q_norm=row(jnp.tile(fox_q_norm[l], FOX_HEADS)), fox_k_norm=row(jnp.tile(fox_k_norm[l], FOX_HEADS)),
        fox_f_bias=_pad_lanes(fox_f_bias[l]),
        lru_conv_w=lru_conv_w[l], lru_conv_b=row(lru_conv_b[l]),
        lru_w_r=lru_w_r[l].astype(BF16), lru_b_r=row(lru_b_r[l]),
        lru_w_i=lru_w_i[l].astype(BF16), lru_b_i=row(lru_b_i[l]), lru_lambda=row(lru_lambda[l]),
        w_gate=w_gate[l].astype(BF16), w_up=w_up[l].astype(BF16), w_down=w_down[l].astype(BF16))


def kernel(x_prompt, x_sample, cache_fox_k, cache_fox_v, cache_fox_logf, state_ssm_conv, state_ssm, state_hgrn,
           state_lru_conv, state_lru, ln1, ln2, w_in, w_out, ssm_conv_w, ssm_conv_b, ssm_dt_bias, ssm_a_log,
           ssm_d, ssm_norm, hgrn_lb_logits, hgrn_norm, fox_q_norm, fox_k_norm, fox_f_bias, lru_conv_w,
           lru_conv_b, lru_w_r, lru_b_r, lru_w_i, lru_b_i, lru_lambda, w_gate, w_up, w_down):
    depth = ln1.shape[0]
    bp, tp, _ = x_prompt.shape
    bs, ts, _ = x_sample.shape
    fresh = (jnp.zeros((bp, SSM_CONV - 1, SSM_CONV_DIM), F32),
             jnp.zeros((bp, SSM_HEADS, SSM_STATE, SSM_HEAD_DIM), F32),
             jnp.zeros((bp, HGRN_HEADS, HGRN_HEAD_DIM, HGRN_HEAD_DIM), F32),
             None, None, None,
             jnp.zeros((bp, LRU_CONV - 1, GROUP_W), F32),
             jnp.zeros((bp, GROUP_W), F32))
    yp = x_prompt.reshape(bp * tp, D_MODEL)
    ys = x_sample.reshape(bs * ts, D_MODEL)
    p_states, s_states = [], []
    for l in range(depth):
        pr = _prep_layer_params(l, ln1, ln2, w_in, w_out, ssm_conv_w, ssm_conv_b, ssm_dt_bias, ssm_a_log, ssm_d,
                                ssm_norm, hgrn_lb_logits, hgrn_norm, fox_q_norm, fox_k_norm, fox_f_bias,
                                lru_conv_w, lru_conv_b, lru_w_r, lru_b_r, lru_w_i, lru_b_i, lru_lambda,
                                w_gate, w_up, w_down)
        yp, st_p = _layer(yp, bp, tp, fresh, pr, l)
        p_states.append(st_p)
        past = (state_ssm_conv[l], state_ssm[l], state_hgrn[l], cache_fox_k[l], cache_fox_v[l],
                cache_fox_logf[l], state_lru_conv[l], state_lru[l])
        ys, st_s = _layer(ys, bs, ts, past, pr, l)
        s_states.append(st_s)
    stack = lambda states: [jnp.stack(parts, axis=0) for parts in zip(*states)]
    return (yp.reshape(bp, tp, D_MODEL), ys.reshape(bs, ts, D_MODEL), *stack(p_states), *stack(s_states))
```

```python
import functools
import math

import jax
import jax.numpy as jnp
import numpy as np
from jax import lax
from jax.experimental import pallas as pl
from jax.experimental.pallas import tpu as pltpu

F32 = jnp.float32
BF16 = jnp.bfloat16

D_MODEL = 2048
GROUP_W = 512
CHUNK = 64
SUB = 16
SSM_HEADS = 8
SSM_HEAD_DIM = 64
SSM_GROUPS = 2
SSM_STATE = 128
SSM_CONV = 4
SSM_CONV_DIM = 1024
HGRN_HEADS = 4
HGRN_HEAD_DIM = 128
HGRN_F_FLOOR = 1e-30
FOX_HEADS = 8
FOX_HEAD_DIM = 64
LRU_BLOCKS = 4
LRU_BLOCK_W = 128
LRU_CONV = 4
LRU_C = 8.0
EPS = 1e-6
NEG = -1e30

LANES = 128
CONV_PAD = 8
VMEM_LIMIT = 56 * 1024 * 1024

COL_A_Z, COL_A_X, COL_A_BC, COL_B_Q, COL_B_F, COL_B_I, COL_B_G = 0, 1, 2, 3, 4, 5, 6
COL_C_Q, COL_C_K, COL_C_V, COL_C_G, COL_D_X, COL_D_G = 7, 8, 9, 10, 11, 12
N_MAIN_BLOCKS = 13


def _cparams(sem):
    return pltpu.CompilerParams(dimension_semantics=sem, vmem_limit_bytes=VMEM_LIMIT)


def _dot(a, b):
    return jnp.dot(a, b, preferred_element_type=F32)


def _dot_nt(a, b):
    return lax.dot_general(a, b, (((1,), (1,)), ((), ())), preferred_element_type=F32)


def _dot_tn(a, b):
    return lax.dot_general(a, b, (((0,), (0,)), ((), ())), preferred_element_type=F32)


def _split3(x):
    hi = x.astype(BF16)
    r = x - hi.astype(F32)
    mid = r.astype(BF16)
    lo = (r - mid.astype(F32)).astype(BF16)
    return hi, mid, lo


def _dot01(m01, x):
    hi, mid, lo = _split3(x)
    return _dot(m01, hi) + _dot(m01, mid) + _dot(m01, lo)


def _dot01_r(x, m01):
    hi, mid, lo = _split3(x)
    return _dot(hi, m01) + _dot(mid, m01) + _dot(lo, m01)


def _tri(n, lower=True):
    r = lax.broadcasted_iota(jnp.int32, (n, n), 0)
    c = lax.broadcasted_iota(jnp.int32, (n, n), 1)
    return (r >= c) if lower else (r <= c)


def _sigmoid(x):
    return jax.nn.sigmoid(x)


def _silu(x):
    return x * jax.nn.sigmoid(x)


def _softplus(x):
    return jnp.maximum(x, 0.0) + jnp.log1p(jnp.exp(-jnp.abs(x)))


def _log_sigmoid(x):
    return -_softplus(-x)


def _in_proj_kernel(x_ref, g_ref, w_ref, ws_ref, o_ref, os_ref, h_scr):
    @pl.when(pl.program_id(1) == 0)
    def _():
        x = x_ref[...]
        ms = jnp.mean(x * x, axis=-1, keepdims=True)
        h = (x * lax.rsqrt(ms + EPS) * g_ref[...]).astype(BF16)
        h_scr[...] = h
        os_ref[...] = _dot(h, ws_ref[...])

    o_ref[...] = _dot(h_scr[...], w_ref[...])


def _in_proj(x2d, ln, w_main, w_small, layer, tm):
    m = x2d.shape[0]
    n_main, n_small = w_main.shape[2], w_small.shape[2]
    return pl.pallas_call(
        _in_proj_kernel,
        grid=(m // tm, n_main // GROUP_W),
        in_specs=[
            pl.BlockSpec((tm, D_MODEL), lambda i, j: (i, 0)),
            pl.BlockSpec((1, D_MODEL), lambda i, j: (0, 0)),
            pl.BlockSpec((None, D_MODEL, GROUP_W), lambda i, j: (layer, 0, j)),
            pl.BlockSpec((None, D_MODEL, n_small), lambda i, j: (layer, 0, 0)),
        ],
        out_specs=[
            pl.BlockSpec((tm, GROUP_W), lambda i, j: (i, j)),
            pl.BlockSpec((tm, n_small), lambda i, j: (i, 0)),
        ],
        out_shape=[jax.ShapeDtypeStruct((m, n_main), F32), jax.ShapeDtypeStruct((m, n_small), F32)],
        scratch_shapes=[pltpu.VMEM((tm, D_MODEL), BF16)],
        compiler_params=_cparams(("parallel", "arbitrary")),
        name="in_proj",
    )(x2d, ln, w_main, w_small)


def _conv_block(xp_scr, cw_ref, cb_ref, tb, kw):
    first = CONV_PAD - (kw - 1)
    y = cb_ref[...]
    for j in range(kw):
        y = y + cw_ref[j:j + 1, :] * xp_scr[pl.ds(first + j, tb), :]
    tail = xp_scr[pl.ds(first + tb, kw - 1), :]
    xp_scr[pl.ds(first, kw - 1), :] = tail
    return y, tail


def _ssd_kernel(z_ref, xlo_ref, xhi_ref, sm_ref, conv0_ref, s0_ref, cw_ref, cb_ref, dtb_ref, alog_ref,
                dexp_ref, nrm_ref, y_ref, convo_ref, so_ref,
                xp_scr, xbc_scr, g_scr, dt_scr, y_scr, s_scr, *, tb):
    c = pl.program_id(1)
    last = pl.num_programs(1) - 1
    hpg = SSM_HEADS // SSM_GROUPS
    gw = hpg * SSM_HEAD_DIM

    @pl.when(c == 0)
    def _():
        s_scr[...] = s0_ref[...]
        xp_scr[0:CONV_PAD, :] = jnp.zeros((CONV_PAD, SSM_CONV_DIM), F32)
        xp_scr[CONV_PAD - (SSM_CONV - 1):CONV_PAD, :] = conv0_ref[...]

    xp_scr[CONV_PAD:CONV_PAD + tb, 0:GROUP_W] = xlo_ref[...]
    xp_scr[CONV_PAD:CONV_PAD + tb, GROUP_W:SSM_CONV_DIM] = xhi_ref[...]
    conv, tail = _conv_block(xp_scr, cw_ref, cb_ref, tb, SSM_CONV)

    @pl.when(c == last)
    def _():
        convo_ref[...] = tail

    xbc_scr[...] = _silu(conv)
    dt = _softplus(sm_ref[...] + dtb_ref[...])
    dt_scr[...] = dt
    g_scr[...] = dt * (-jnp.exp(alog_ref[...]))

    tri_mask = _tri(CHUNK)
    tri_l = tri_mask.astype(BF16)
    tri_u = _tri(CHUNK, lower=False).astype(BF16)
    b_off = GROUP_W
    c_off = GROUP_W + SSM_GROUPS * SSM_STATE

    def chunk(ci, carry):
        rows = pl.ds(pl.multiple_of(ci * CHUNK, CHUNK), CHUNK)
        gc = g_scr[rows, :]
        dtc = dt_scr[rows, :]
        cum = _dot01(tri_l, gc)
        cum_t = _dot01_r(gc.T, tri_u)
        tot = cum[CHUNK - 1:CHUNK, :]
        ys = []
        for grp in range(SSM_GROUPS):
            bm = xbc_scr[rows, b_off + grp * SSM_STATE:b_off + (grp + 1) * SSM_STATE].astype(BF16)
            cm = xbc_scr[rows, c_off + grp * SSM_STATE:c_off + (grp + 1) * SSM_STATE].astype(BF16)
            gmat = _dot_nt(cm, bm)
            s_g = s_scr[grp]
            cs = _dot(cm, s_g.astype(BF16))
            vdec, etot = [], []
            for hh in range(hpg):
                h = grp * hpg + hh
                col = cum[:, h:h + 1]
                row = cum_t[h:h + 1, :]
                dec = jnp.where(tri_mask, jnp.exp(jnp.minimum(col - row, 0.0)), 0.0)
                att = (gmat * dec).astype(BF16)
                xh = xbc_scr[rows, h * SSM_HEAD_DIM:(h + 1) * SSM_HEAD_DIM]
                vh = xh * dtc[:, h:h + 1]
                o = _dot(att, vh.astype(BF16)) + jnp.exp(col) * cs[:, hh * SSM_HEAD_DIM:(hh + 1) * SSM_HEAD_DIM]
                ys.append(o + dexp_ref[:, h * SSM_HEAD_DIM:(h + 1) * SSM_HEAD_DIM] * xh)
                toth = tot[:, h:h + 1]
                vdec.append(vh * jnp.exp(toth - col))
                etot.append(jnp.broadcast_to(jnp.exp(toth), (1, SSM_HEAD_DIM)))
            vdec = jnp.concatenate(vdec, axis=1).astype(BF16)
            etot = jnp.concatenate(etot, axis=1)
            s_scr[grp] = etot * s_g + _dot_tn(bm, vdec)
        y_scr[rows, :] = jnp.concatenate(ys, axis=1)
        return carry

    lax.fori_loop(0, tb // CHUNK, chunk, 0)

    y = y_scr[...] * _silu(z_ref[...])
    outs = []
    for grp in range(SSM_GROUPS):
        yg = y[:, grp * gw:(grp + 1) * gw]
        ms = jnp.mean(yg * yg, axis=-1, keepdims=True)
        outs.append(yg * lax.rsqrt(ms + EPS))
    y_ref[...] = (jnp.concatenate(outs, axis=1) * nrm_ref[...]).astype(y_ref.dtype)

    @pl.when(c == last)
    def _():
        so_ref[...] = s_scr[...]


def _ssd(proj, small, conv0, s0, cw, cb, dtb, alog, dexp, nrm, nb, t, tb):
    nc = t // tb
    hpg = SSM_HEADS // SSM_GROUPS
    sshape = (SSM_GROUPS, SSM_STATE, hpg * SSM_HEAD_DIM)
    row = lambda col: pl.BlockSpec((tb, GROUP_W), lambda b, c: (b * nc + c, col))
    full = lambda a: pl.BlockSpec(a.shape, lambda b, c: (0,) * a.ndim)
    return pl.pallas_call(
        functools.partial(_ssd_kernel, tb=tb),
        grid=(nb, nc),
        in_specs=[row(COL_A_Z), row(COL_A_X), row(COL_A_BC),
                  pl.BlockSpec((tb, LANES), lambda b, c: (b * nc + c, 0)),
                  pl.BlockSpec((None, SSM_CONV - 1, SSM_CONV_DIM), lambda b, c: (b, 0, 0)),
                  pl.BlockSpec((None,) + sshape, lambda b, c: (b, 0, 0, 0)),
                  full(cw), full(cb), full(dtb), full(alog), full(dexp), full(nrm)],
        out_specs=[pl.BlockSpec((tb, GROUP_W), lambda b, c: (b * nc + c, 0)),
                   pl.BlockSpec((None, SSM_CONV - 1, SSM_CONV_DIM), lambda b, c: (b, 0, 0)),
                   pl.BlockSpec((None,) + sshape, lambda b, c: (b, 0, 0, 0))],
        out_shape=[jax.ShapeDtypeStruct((nb * t, GROUP_W), BF16),
                   jax.ShapeDtypeStruct((nb, SSM_CONV - 1, SSM_CONV_DIM), F32),
                   jax.ShapeDtypeStruct((nb,) + sshape, F32)],
        scratch_shapes=[pltpu.VMEM((CONV_PAD + tb, SSM_CONV_DIM), F32),
                        pltpu.VMEM((tb, SSM_CONV_DIM), F32),
                        pltpu.VMEM((tb, LANES), F32),
                        pltpu.VMEM((tb, LANES), F32),
                        pltpu.VMEM((tb, GROUP_W), F32),
                        pltpu.VMEM(sshape, F32)],
        compiler_params=_cparams(("parallel", "arbitrary")),
        name="ssd",
    )(proj, proj, proj, small, conv0, s0, cw, cb, dtb, alog, dexp, nrm)


def _hgrn_kernel(q_ref, f_ref, i_ref, g_ref, s0_ref, lbl_ref, nrm_ref, y_ref, so_ref,
                 c_scr, k_scr, st_scr, *, tb, layer):
    c = pl.program_id(1)
    last = pl.num_programs(1) - 1
    hd = HGRN_HEAD_DIM
    nsub = CHUNK // SUB

    @pl.when(c == 0)
    def _():
        for h in range(HGRN_HEADS):
            st_scr[h] = s0_ref[h].T

    ll = lbl_ref[...]
    e = jnp.exp(ll - jnp.max(ll, axis=0, keepdims=True))
    p = e / jnp.sum(e, axis=0, keepdims=True)
    cs = p[0:1, :]
    for l in range(1, layer + 1):
        cs = cs + p[l:l + 1, :]
    lb = cs - p[0:1, :]

    tri_l = _tri(CHUNK).astype(BF16)
    rowid = lax.broadcasted_iota(jnp.int32, (CHUNK, 1), 0)
    subid = lax.broadcasted_iota(jnp.int32, (SUB, 1), 0)

    def chunk(ci, carry):
        r0 = pl.multiple_of(ci * CHUNK, CHUNK)
        rows = pl.ds(r0, CHUNK)
        fr = f_ref[rows, :]
        fg = lb + (1.0 - lb) * _sigmoid(fr)
        logf = jnp.log(jnp.maximum(fg, HGRN_F_FLOOR))
        hk = (1.0 - lb) * _sigmoid(-fr)
        hq = _silu(q_ref[rows, :])
        v = i_ref[rows, :]
        vb = v.astype(BF16)
        cum = _dot01(tri_l, logf)
        tot = cum[CHUNK - 1:CHUNK, :]
        c_scr[...] = cum
        k_scr[...] = hk
        qe = (hq * jnp.exp(cum)).astype(BF16)
        kd = (hk * jnp.exp(tot - cum)).astype(BF16)
        etot = jnp.exp(tot)

        qcat, kcat = [], []
        for j in range(nsub - 1):
            ej = cum[(j + 1) * SUB - 1:(j + 1) * SUB, :]
            qcat.append(jnp.where(rowid >= (j + 1) * SUB, hq * jnp.exp(jnp.minimum(cum - ej, 0.0)), 0.0).astype(BF16))
            in_j = (rowid >= j * SUB) & (rowid < (j + 1) * SUB)
            kcat.append(jnp.where(in_j, hk * jnp.exp(jnp.minimum(ej - cum, 0.0)), 0.0).astype(BF16))

        o_heads = []
        for h in range(HGRN_HEADS):
            hs = slice(h * hd, (h + 1) * hd)
            qc = jnp.concatenate([q[:, hs] for q in qcat], axis=1)
            kc = jnp.concatenate([k[:, hs] for k in kcat], axis=1)
            a_off = _dot_nt(qc, kc)
            st = st_scr[h]
            o_heads.append(_dot(a_off.astype(BF16), vb[:, hs]) + _dot_nt(qe[:, hs], st.astype(BF16)))
            st_scr[h] = st * etot[:, hs] + _dot_tn(vb[:, hs], kd[:, hs])
        o = jnp.concatenate(o_heads, axis=1)

        o_sub = []
        for i in range(nsub):
            base = i * SUB
            ci_ = cum[base:base + SUB, :]
            qi = hq[base:base + SUB, :]
            acc = [jnp.zeros((SUB, hd), F32) for _ in range(HGRN_HEADS)]
            for j in range(SUB):
                crow = c_scr[pl.ds(base + j, 1), :]
                krow = k_scr[pl.ds(base + j, 1), :]
                vrow = i_ref[pl.ds(r0 + base + j, 1), :]
                x = jnp.where(subid >= j, qi * jnp.exp(jnp.minimum(ci_ - crow, 0.0)) * krow, 0.0)
                for h in range(HGRN_HEADS):
                    hs = slice(h * hd, (h + 1) * hd)
                    a = jnp.sum(x[:, hs], axis=-1, keepdims=True)
                    acc[h] = acc[h] + a * vrow[:, hs]
            o_sub.append(jnp.concatenate(acc, axis=1))
        o = o + jnp.concatenate(o_sub, axis=0)

        outs = []
        for h in range(HGRN_HEADS):
            oh = o[:, h * hd:(h + 1) * hd]
            ms = jnp.mean(oh * oh, axis=-1, keepdims=True)
            outs.append(oh * lax.rsqrt(ms + EPS))
        y = jnp.concatenate(outs, axis=1) * nrm_ref[...] * _silu(g_ref[rows, :])
        y_ref[rows, :] = y.astype(y_ref.dtype)
        return carry

    lax.fori_loop(0, tb // CHUNK, chunk, 0)

    @pl.when(c == last)
    def _():
        for h in range(HGRN_HEADS):
            so_ref[h] = st_scr[h].T


def _hgrn(proj, s0, lbl, nrm, nb, t, tb, layer):
    nc = t // tb
    sshape = (HGRN_HEADS, HGRN_HEAD_DIM, HGRN_HEAD_DIM)
    row = lambda col: pl.BlockSpec((tb, GROUP_W), lambda b, c: (b * nc + c, col))
    full = lambda a: pl.BlockSpec(a.shape, lambda b, c: (0,) * a.ndim)
    return pl.pallas_call(
        functools.partial(_hgrn_kernel, tb=tb, layer=layer),
        grid=(nb, nc),
        in_specs=[row(COL_B_Q), row(COL_B_F), row(COL_B_I), row(COL_B_G),
                  pl.BlockSpec((None,) + sshape, lambda b, c: (b, 0, 0, 0)),
                  full(lbl), full(nrm)],
        out_specs=[pl.BlockSpec((tb, GROUP_W), lambda b, c: (b * nc + c, 0)),
                   pl.BlockSpec((None,) + sshape, lambda b, c: (b, 0, 0, 0))],
        out_shape=[jax.ShapeDtypeStruct((nb * t, GROUP_W), BF16),
                   jax.ShapeDtypeStruct((nb,) + sshape, F32)],
        scratch_shapes=[pltpu.VMEM((CHUNK, GROUP_W), F32),
                        pltpu.VMEM((CHUNK, GROUP_W), F32),
                        pltpu.VMEM(sshape, F32)],
        compiler_params=_cparams(("parallel", "arbitrary")),
        name="hgrn2",
    )(proj, proj, proj, proj, s0, lbl, nrm)


def _lru_kernel(x_ref, g_ref, conv0_ref, h0_ref, cw_ref, cb_ref, wr_ref, br_ref, wi_ref, bi_ref, lam_ref,
                y_ref, convo_ref, ho_ref, xp_scr, h_scr, *, tb, reset_first):
    c = pl.program_id(1)
    last = pl.num_programs(1) - 1

    @pl.when(c == 0)
    def _():
        h_scr[...] = h0_ref[...]
        xp_scr[0:CONV_PAD, :] = jnp.zeros((CONV_PAD, GROUP_W), F32)
        xp_scr[CONV_PAD - (LRU_CONV - 1):CONV_PAD, :] = conv0_ref[...]

    xp_scr[CONV_PAD:CONV_PAD + tb, :] = x_ref[...]
    xc, tail = _conv_block(xp_scr, cw_ref, cb_ref, tb, LRU_CONV)

    @pl.when(c == last)
    def _():
        convo_ref[...] = tail

    xcb = xc.astype(BF16)

    def blockdiag(w_ref):
        return jnp.concatenate(
            [_dot(xcb[:, n * LRU_BLOCK_W:(n + 1) * LRU_BLOCK_W], w_ref[n]) for n in range(LRU_BLOCKS)], axis=1)

    r = _sigmoid(blockdiag(wr_ref) + br_ref[...])
    gi = _sigmoid(blockdiag(wi_ref) + bi_ref[...])
    log_a = LRU_C * r * _log_sigmoid(lam_ref[...])
    a = jnp.exp(log_a)
    mult = jnp.sqrt(-jnp.tanh(log_a) * (a * a + 1.0))
    rowid = lax.broadcasted_iota(jnp.int32, (tb, 1), 0)
    first_row = rowid == 0
    if reset_first:
        rst = first_row & (c == 0)
        a = jnp.where(rst, 0.0, a)
        mult = jnp.where(rst, 1.0, mult)
    u = mult * gi * xc
    u = u + jnp.where(first_row, a * h_scr[...], 0.0)

    av, bv = a, u
    d = 1
    while d < tb:
        m = rowid >= d
        a_sh = pltpu.roll(av, d, axis=0)
        b_sh = pltpu.roll(bv, d, axis=0)
        bv = jnp.where(m, av * b_sh + bv, bv)
        av = jnp.where(m, av * a_sh, av)
        d *= 2
    h_new = bv[tb - 1:tb, :]
    h_scr[...] = h_new
    y_ref[...] = (bv * jax.nn.gelu(g_ref[...])).astype(y_ref.dtype)

    @pl.when(c == last)
    def _():
        ho_ref[...] = h_new


def _lru(proj, conv0, h0, cw, cb, wr, br, wi, bi, lam, nb, t, tb, reset_first):
    nc = t // tb
    row = lambda col: pl.BlockSpec((tb, GROUP_W), lambda b, c: (b * nc + c, col))
    full = lambda a: pl.BlockSpec(a.shape, lambda b, c: (0,) * a.ndim)
    return pl.pallas_call(
        functools.partial(_lru_kernel, tb=tb, reset_first=reset_first),
        grid=(nb, nc),
        in_specs=[row(COL_D_X), row(COL_D_G),
                  pl.BlockSpec((None, LRU_CONV - 1, GROUP_W), lambda b, c: (b, 0, 0)),
                  pl.BlockSpec((None, 1, GROUP_W), lambda b, c: (b, 0, 0)),
                  full(cw), full(cb), full(wr), full(br), full(wi), full(bi), full(lam)],
        out_specs=[pl.BlockSpec((tb, GROUP_W), lambda b, c: (b * nc + c, 0)),
                   pl.BlockSpec((None, LRU_CONV - 1, GROUP_W), lambda b, c: (b, 0, 0)),
                   pl.BlockSpec((None, 1, GROUP_W), lambda b, c: (b, 0, 0))],
        out_shape=[jax.ShapeDtypeStruct((nb * t, GROUP_W), BF16),
                   jax.ShapeDtypeStruct((nb, LRU_CONV - 1, GROUP_W), F32),
                   jax.ShapeDtypeStruct((nb, 1, GROUP_W), F32)],
        scratch_shapes=[pltpu.VMEM((CONV_PAD + tb, GROUP_W), F32),
                        pltpu.VMEM((1, GROUP_W), F32)],
        compiler_params=_cparams(("parallel", "arbitrary")),
        name="rglru",
    )(proj, proj, conv0, h0, cw, cb, wr, br, wi, bi, lam)


def _gate_kernel(*refs, tb, past):
    if past:
        sm_ref, fb_ref, plf_ref, lf_ref, cum_ref, cump_ref, carry_scr = refs
    else:
        sm_ref, fb_ref, lf_ref, cum_ref, carry_scr = refs

    @pl.when(pl.program_id(1) == 0)
    def _():
        if past:
            cp = _dot01(_tri(past).astype(BF16), plf_ref[...])
            cump_ref[...] = cp
            carry_scr[...] = cp[past - 1:past, :]
        else:
            carry_scr[...] = jnp.zeros_like(carry_scr)

    lf = _log_sigmoid(sm_ref[...] + fb_ref[...])
    lf_ref[...] = lf
    cum = _dot01(_tri(tb).astype(BF16), lf) + carry_scr[...]
    cum_ref[...] = cum
    carry_scr[...] = cum[tb - 1:tb, :]


def _gate(small, fb, past_lf, nb, t, tb):
    nc = t // tb
    past = 0 if past_lf is None else past_lf.shape[1]
    rows = pl.BlockSpec((tb, LANES), lambda b, c: (b * nc + c, 0))
    in_specs = [pl.BlockSpec((tb, LANES), lambda b, c: (b * nc + c, 1)),
                pl.BlockSpec(fb.shape, lambda b, c: (0, 0))]
    out_specs = [rows, rows]
    out_shape = [jax.ShapeDtypeStruct((nb * t, LANES), F32)] * 2
    args = [small, fb]
    if past:
        in_specs.append(pl.BlockSpec((None, past, LANES), lambda b, c: (b, 0, 0)))
        out_specs.append(pl.BlockSpec((None, past, LANES), lambda b, c: (b, 0, 0)))
        out_shape.append(jax.ShapeDtypeStruct((nb, past, LANES), F32))
        args.append(past_lf)
    return pl.pallas_call(
        functools.partial(_gate_kernel, tb=tb, past=past),
        grid=(nb, nc),
        in_specs=in_specs, out_specs=out_specs, out_shape=out_shape,
        scratch_shapes=[pltpu.VMEM((1, LANES), F32)],
        compiler_params=_cparams(("parallel", "arbitrary")),
        name="fox_gate",
    )(*args)


def _head_norm(x, gain, hsum):
    ms = _dot01_r(x * x, hsum) * (1.0 / FOX_HEAD_DIM)
    return x * lax.rsqrt(ms + EPS) * gain


def _foxprep_kernel(q_ref, k_ref, v_ref, hsum_ref, qn_ref, kn_ref, fq_ref, fk_ref, fv_ref, kb_ref, vb_ref):
    hsum = hsum_ref[...]
    fq_ref[...] = (_head_norm(q_ref[...], qn_ref[...], hsum) * (FOX_HEAD_DIM ** -0.5)).astype(BF16)
    fk = _head_norm(k_ref[...], kn_ref[...], hsum)
    fk_ref[...] = fk
    kb_ref[...] = fk.astype(BF16)
    v = v_ref[...]
    fv_ref[...] = v
    vb_ref[...] = v.astype(BF16)


def _foxprep(proj, hsum, qn, kn, tb):
    m = proj.shape[0]
    row = lambda col: pl.BlockSpec((tb, GROUP_W), lambda i: (i, col))
    full = lambda a: pl.BlockSpec(a.shape, lambda i: (0,) * a.ndim)
    out_row = pl.BlockSpec((tb, GROUP_W), lambda i: (i, 0))
    return pl.pallas_call(
        _foxprep_kernel,
        grid=(m // tb,),
        in_specs=[row(COL_C_Q), row(COL_C_K), row(COL_C_V), full(hsum), full(qn), full(kn)],
        out_specs=[out_row] * 5,
        out_shape=[jax.ShapeDtypeStruct((m, GROUP_W), BF16),
                   jax.ShapeDtypeStruct((m, GROUP_W), F32),
                   jax.ShapeDtypeStruct((m, GROUP_W), F32),
                   jax.ShapeDtypeStruct((m, GROUP_W), BF16),
                   jax.ShapeDtypeStruct((m, GROUP_W), BF16)],
        compiler_params=_cparams(("parallel",)),
        name="fox_prep",
    )(proj, proj, proj, hsum, qn, kn)


AUG_W = FOX_HEADS * LANES
N_PIECES = 3


def _aug_constants():
    pq = np.zeros((N_PIECES * LANES, AUG_W), np.float32)
    pk = np.zeros((N_PIECES * LANES, AUG_W), np.float32)
    ones_q = np.zeros((1, AUG_W), np.float32)
    ones_k = np.zeros((1, AUG_W), np.float32)
    ones_v = np.zeros((1, AUG_W), np.float32)
    head = np.zeros((1, AUG_W), np.float32)
    for h in range(FOX_HEADS):
        own = h * LANES + FOX_HEAD_DIM * (h % 2)
        other = h * LANES + FOX_HEAD_DIM * (1 - h % 2)
        head[0, own:own + FOX_HEAD_DIM] = 1.0
        for piece in range(N_PIECES):
            pq[piece * LANES + h, other + piece] = 1.0
            ones_k[0, other + piece] = 1.0
            ones_q[0, other + N_PIECES + piece] = 1.0
            pk[piece * LANES + h, other + N_PIECES + piece] = -1.0
        ones_v[0, other] = 1.0
    return dict(aug_pq=jnp.asarray(pq, BF16), aug_pk=jnp.asarray(pk, BF16), aug_ones_q=jnp.asarray(ones_q),
                aug_ones_k=jnp.asarray(ones_k), aug_ones_v=jnp.asarray(ones_v), aug_head=jnp.asarray(head))


def _foxprep_aug_kernel(q_ref, k_ref, v_ref, cum_ref, hsum_ref, qn_ref, kn_ref, pq_ref, pk_ref, oq_ref, ok_ref,
                        ov_ref, head_ref, fk_ref, fv_ref, qa_ref, ka_ref, va_ref):
    hsum = hsum_ref[...]
    fq = _head_norm(q_ref[...], qn_ref[...], hsum) * (FOX_HEAD_DIM ** -0.5)
    fk = _head_norm(k_ref[...], kn_ref[...], hsum)
    v = v_ref[...]
    fk_ref[...] = fk
    fv_ref[...] = v
    pieces = jnp.concatenate(_split3(cum_ref[...]), axis=1)
    own = head_ref[...] > 0.5

    def per_head(x):
        return jnp.concatenate([x[:, (h // 2) * LANES:(h // 2 + 1) * LANES] for h in range(FOX_HEADS)], axis=1)

    qa = jnp.where(own, per_head(fq), _dot(pieces, pq_ref[...]) + oq_ref[...]).astype(BF16)
    ka = jnp.where(own, per_head(fk), _dot(pieces, pk_ref[...]) + ok_ref[...]).astype(BF16)
    va = jnp.where(own, per_head(v), ov_ref[...]).astype(BF16)
    for h in range(FOX_HEADS):
        hs = slice(h * LANES, (h + 1) * LANES)
        qa_ref[h] = qa[:, hs]
        ka_ref[h] = ka[:, hs]
        va_ref[h] = va[:, hs]


def _foxprep_aug(proj, cum, hsum, qn, kn, aug, tb):
    m = proj.shape[0]
    row = lambda col: pl.BlockSpec((tb, GROUP_W), lambda i: (i, col))
    full = lambda a: pl.BlockSpec(a.shape, lambda i: (0,) * a.ndim)
    out_row = pl.BlockSpec((tb, GROUP_W), lambda i: (i, 0))
    out_aug = pl.BlockSpec((FOX_HEADS, tb, LANES), lambda i: (0, i, 0))
    consts = [aug['aug_pq'], aug['aug_pk'], aug['aug_ones_q'], aug['aug_ones_k'], aug['aug_ones_v'],
              aug['aug_head']]
    return pl.pallas_call(
        _foxprep_aug_kernel,
        grid=(m // tb,),
        in_specs=[row(COL_C_Q), row(COL_C_K), row(COL_C_V), pl.BlockSpec((tb, LANES), lambda i: (i, 0)),
                  full(hsum), full(qn), full(kn)] + [full(a) for a in consts],
        out_specs=[out_row, out_row, out_aug, out_aug, out_aug],
        out_shape=[jax.ShapeDtypeStruct((m, GROUP_W), F32),
                   jax.ShapeDtypeStruct((m, GROUP_W), F32)]
                  + [jax.ShapeDtypeStruct((FOX_HEADS, m, LANES), BF16)] * 3,
        compiler_params=_cparams(("parallel",)),
        name="fox_prep_aug",
    )(proj, proj, proj, cum, hsum, qn, kn, *consts)


def _attn_aug_kernel(q_ref, k_ref, v_ref, g_ref, y_ref, *, tq, hg):
    qi = pl.program_id(2)
    causal = _tri(tq)
    low = lax.broadcasted_iota(jnp.int32, (1, LANES), 1) < FOX_HEAD_DIM
    qs = [q_ref[i] for i in range(hg)]

    def step(j, carry, masked):
        rows = pl.ds(pl.multiple_of(j * tq, tq), tq)
        new = []
        for i in range(hg):
            m, acc = carry[i]
            s = _dot_nt(qs[i], k_ref[i, rows, :])
            if masked:
                s = jnp.where(causal, s, NEG)
            m_new = jnp.maximum(m, jnp.max(s, axis=-1, keepdims=True))
            p = jnp.exp(s - m_new).astype(BF16)
            acc = jnp.exp(m - m_new) * acc + _dot(p, v_ref[i, rows, :])
            new.append((m_new, acc))
        return tuple(new)

    init = tuple((jnp.full((tq, 1), NEG, F32), jnp.zeros((tq, LANES), F32)) for _ in range(hg))
    carry = lax.fori_loop(0, qi, lambda j, c: step(j, c, False), init)
    carry = step(qi, carry, True)
    outs = []
    for i in range(0, hg, 2):
        acc_e, acc_o = carry[i][1], carry[i + 1][1]
        out_e = acc_e / acc_e[:, FOX_HEAD_DIM:FOX_HEAD_DIM + 1]
        out_o = acc_o / acc_o[:, 0:1]
        outs.append(jnp.where(low, out_e, out_o))
    y = jnp.concatenate(outs, axis=1) * _sigmoid(g_ref[...])
    y_ref[...] = y.astype(y_ref.dtype)


def _attention_aug(qa, ka, va, proj, nb, t, tq, hg):
    nq = t // tq
    ngrp = FOX_HEADS // hg
    wout = hg * FOX_HEAD_DIM
    gate_col0 = COL_C_G * GROUP_W // wout
    return pl.pallas_call(
        functools.partial(_attn_aug_kernel, tq=tq, hg=hg),
        grid=(nb, ngrp, nq),
        in_specs=[pl.BlockSpec((hg, tq, LANES), lambda b, g, i: (g, b * nq + i, 0)),
                  pl.BlockSpec((hg, t, LANES), lambda b, g, i: (g, b, 0)),
                  pl.BlockSpec((hg, t, LANES), lambda b, g, i: (g, b, 0)),
                  pl.BlockSpec((tq, wout), lambda b, g, i: (b * nq + i, gate_col0 + g))],
        out_specs=pl.BlockSpec((tq, wout), lambda b, g, i: (b * nq + i, g)),
        out_shape=jax.ShapeDtypeStruct((nb * t, GROUP_W), BF16),
        compiler_params=_cparams(("parallel", "parallel", "arbitrary")),
        name="fox_attention_aug",
    )(qa, ka, va, proj)


def _attn_cached_kernel(q_ref, cq_ref, ck_ref, k_ref, v_ref, g_ref, pk_ref, pv_ref, y_ref, *, t, past):
    lane = lax.broadcasted_iota(jnp.int32, (1, LANES), 1)
    low = lane < FOX_HEAD_DIM
    causal = _tri(t)
    zero = jnp.zeros((), BF16)
    outs = []
    for pair in range(FOX_HEADS // 2):
        ps = slice(pair * LANES, (pair + 1) * LANES)
        qp = q_ref[:, ps]
        pk = pk_ref[:, ps].astype(BF16)
        pv = pv_ref[:, ps].astype(BF16)
        kc = k_ref[:, ps]
        vc = v_ref[:, ps]
        halves = []
        for half in range(2):
            h = 2 * pair + half
            qpad = jnp.where(low if half == 0 else ~low, qp, zero)
            cqh = cq_ref[:, h:h + 1]
            s_p = _dot_nt(qpad, pk) + cqh - ck_ref[h:h + 1, 0:past]
            s_c = jnp.where(causal, _dot_nt(qpad, kc) + cqh - ck_ref[h:h + 1, past:past + t], NEG)
            m = jnp.maximum(jnp.max(s_p, axis=-1, keepdims=True), jnp.max(s_c, axis=-1, keepdims=True))
            p_p = jnp.exp(s_p - m)
            p_c = jnp.exp(s_c - m)
            l = jnp.sum(p_p, axis=-1, keepdims=True) + jnp.sum(p_c, axis=-1, keepdims=True)
            acc = _dot(p_p.astype(BF16), pv) + _dot(p_c.astype(BF16), vc)
            halves.append(acc / l)
        outs.append(jnp.where(low, halves[0], halves[1]))
    y = jnp.concatenate(outs, axis=1) * _sigmoid(g_ref[...])
    y_ref[...] = y.astype(y_ref.dtype)


def _attention_cached(fq, cq, ck_t, kb, vb, proj, past_k, past_v, nb, t):
    past = past_k.shape[1]
    ltot = ck_t.shape[-1]
    seq = lambda rows: pl.BlockSpec((None, rows, GROUP_W), lambda b: (b, 0, 0))
    return pl.pallas_call(
        functools.partial(_attn_cached_kernel, t=t, past=past),
        grid=(nb,),
        in_specs=[pl.BlockSpec((t, GROUP_W), lambda b: (b, 0)),
                  pl.BlockSpec((None, t, LANES), lambda b: (b, 0, 0)),
                  pl.BlockSpec((None, FOX_HEADS, ltot), lambda b: (b, 0, 0)),
                  seq(t), seq(t),
                  pl.BlockSpec((t, GROUP_W), lambda b: (b, COL_C_G)),
                  seq(past), seq(past)],
        out_specs=pl.BlockSpec((t, GROUP_W), lambda b: (b, 0)),
        out_shape=jax.ShapeDtypeStruct((nb * t, GROUP_W), BF16),
        compiler_params=_cparams(("parallel",)),
        name="fox_attention_cached",
    )(fq, cq, ck_t, kb.reshape(nb, t, GROUP_W), vb.reshape(nb, t, GROUP_W), proj, past_k, past_v)


def _out_proj_kernel(x_ref, ya_ref, yb_ref, yc_ref, yd_ref, w_ref, o_ref):
    acc = x_ref[...]
    for n, y_ref in enumerate((ya_ref, yb_ref, yc_ref, yd_ref)):
        acc = acc + _dot(y_ref[...], w_ref[n * GROUP_W:(n + 1) * GROUP_W, :])
    o_ref[...] = acc


def _out_proj(x2d, ys, w_out, layer, tm):
    m = x2d.shape[0]
    yspec = pl.BlockSpec((tm, GROUP_W), lambda i: (i, 0))
    return pl.pallas_call(
        _out_proj_kernel,
        grid=(m // tm,),
        in_specs=[pl.BlockSpec((tm, D_MODEL), lambda i: (i, 0)), yspec, yspec, yspec, yspec,
                  pl.BlockSpec((None,) + w_out.shape[1:], lambda i: (layer, 0, 0))],
        out_specs=pl.BlockSpec((tm, D_MODEL), lambda i: (i, 0)),
        out_shape=jax.ShapeDtypeStruct((m, D_MODEL), F32),
        compiler_params=_cparams(("parallel",)),
        name="out_proj",
    )(x2d, *ys, w_out)


def _ffn_kernel(x_ref, g_ref, wg_ref, wu_ref, wd_ref, o_ref, h_scr, acc_scr):
    f = pl.program_id(1)

    @pl.when(f == 0)
    def _():
        x = x_ref[...]
        ms = jnp.mean(x * x, axis=-1, keepdims=True)
        h_scr[...] = (x * lax.rsqrt(ms + EPS) * g_ref[...]).astype(BF16)
        acc_scr[...] = x

    h = h_scr[...]
    a = _silu(_dot(h, wg_ref[...])) * _dot(h, wu_ref[...])
    acc_scr[...] += _dot(a.astype(BF16), wd_ref[...])

    @pl.when(f == pl.num_programs(1) - 1)
    def _():
        o_ref[...] = acc_scr[...]


def _ffn(x2d, ln, wg, wu, wd, layer, tm, tf):
    m = x2d.shape[0]
    d_ff = wg.shape[2]
    return pl.pallas_call(
        _ffn_kernel,
        grid=(m // tm, d_ff // tf),
        in_specs=[pl.BlockSpec((tm, D_MODEL), lambda i, f: (i, 0)),
                  pl.BlockSpec((1, D_MODEL), lambda i, f: (0, 0)),
                  pl.BlockSpec((None, D_MODEL, tf), lambda i, f: (layer, 0, f)),
                  pl.BlockSpec((None, D_MODEL, tf), lambda i, f: (layer, 0, f)),
                  pl.BlockSpec((None, tf, D_MODEL), lambda i, f: (layer, f, 0))],
        out_specs=pl.BlockSpec((tm, D_MODEL), lambda i, f: (i, 0)),
        out_shape=jax.ShapeDtypeStruct((m, D_MODEL), F32),
        scratch_shapes=[pltpu.VMEM((tm, D_MODEL), BF16), pltpu.VMEM((tm, D_MODEL), F32)],
        compiler_params=_cparams(("parallel", "arbitrary")),
        name="swiglu_ffn",
    )(x2d, ln, wg, wu, wd)


def _pad_lanes(v, offset=0):
    return jnp.zeros((1, LANES), F32).at[0, offset:offset + v.shape[0]].set(v.astype(F32))


def _ssm_state_to_groups(s):
    nb = s.shape[0]
    hpg = SSM_HEADS // SSM_GROUPS
    s = s.reshape(nb, SSM_GROUPS, hpg, SSM_STATE, SSM_HEAD_DIM)
    return jnp.transpose(s, (0, 1, 3, 2, 4)).reshape(nb, SSM_GROUPS, SSM_STATE, hpg * SSM_HEAD_DIM)


def _ssm_state_from_groups(s):
    nb = s.shape[0]
    hpg = SSM_HEADS // SSM_GROUPS
    s = s.reshape(nb, SSM_GROUPS, SSM_STATE, hpg, SSM_HEAD_DIM)
    return jnp.transpose(s, (0, 1, 3, 2, 4)).reshape(nb, SSM_HEADS, SSM_STATE, SSM_HEAD_DIM)


def _block_rows(t, want):
    return want if t % want == 0 else t


def _layer(x2d, nb, t, state, pr, layer):
    (ssm_conv, ssm_s, hgrn_s, fox_k, fox_v, fox_lf, lru_conv, lru_h) = state
    m = nb * t
    past = 0 if fox_k is None else fox_k.shape[1]
    tm = _block_rows(m, 512)
    tb = _block_rows(t, 256)

    proj, small = _in_proj(x2d, pr['ln1'], pr['w_main'], pr['w_small'], layer, _block_rows(m, 1024))

    ya, ssm_conv_new, ssm_g = _ssd(proj, small, ssm_conv, _ssm_state_to_groups(ssm_s), pr['ssm_conv_w'],
                                   pr['ssm_conv_b'], pr['ssm_dt_bias'], pr['ssm_a_log'], pr['ssm_d'],
                                   pr['ssm_norm'], nb, t, tb)
    ssm_s_new = _ssm_state_from_groups(ssm_g)

    yb, hgrn_s_new = _hgrn(proj, hgrn_s, pr['hgrn_lb_logits'], pr['hgrn_norm'], nb, t, tb, layer)

    yd, lru_conv_new, lru_h_new = _lru(proj, lru_conv, lru_h.reshape(nb, 1, GROUP_W), pr['lru_conv_w'],
                                       pr['lru_conv_b'], pr['lru_w_r'], pr['lru_b_r'], pr['lru_w_i'],
                                       pr['lru_b_i'], pr['lru_lambda'], nb, t, tb, reset_first=(past == 0))

    if past:
        past_lf = jnp.pad(fox_lf.astype(F32), ((0, 0), (0, 0), (0, LANES - FOX_HEADS)))
        lf, cum, cum_past = _gate(small, pr['fox_f_bias'], past_lf, nb, t, t)
        fq, fk, fv, kb, vb = _foxprep(proj, pr['head_sum'], pr['fox_q_norm'], pr['fox_k_norm'], tm)
        cq = cum.reshape(nb, t, LANES)
        ck_t = jnp.transpose(jnp.concatenate([cum_past, cq], axis=1)[:, :, :FOX_HEADS], (0, 2, 1))
        yc = _attention_cached(fq, cq, ck_t, kb, vb, proj, fox_k.reshape(nb, past, GROUP_W),
                               fox_v.reshape(nb, past, GROUP_W), nb, t)
    else:
        lf, cum = _gate(small, pr['fox_f_bias'], None, nb, t, _block_rows(t, 512))
        fk, fv, qa, ka, va = _foxprep_aug(proj, cum, pr['head_sum'], pr['fox_q_norm'], pr['fox_k_norm'],
                                          pr['aug'], tm)
        yc = _attention_aug(qa, ka, va, proj, nb, t, _block_rows(t, 512), FOX_HEADS)
    lf = lf.reshape(nb, t, LANES)

    x1 = _out_proj(x2d, (ya, yb, yc, yd), pr['w_out'], layer, tm)
    x2 = _ffn(x1, pr['ln2'], pr['w_gate'], pr['w_up'], pr['w_down'], layer, tm, 512)

    new_state = (ssm_conv_new, ssm_s_new, hgrn_s_new,
                 fk.reshape(nb, t, FOX_HEADS, FOX_HEAD_DIM), fv.reshape(nb, t, FOX_HEADS, FOX_HEAD_DIM),
                 lf[:, :, :FOX_HEADS], lru_conv_new, lru_h_new.reshape(nb, GROUP_W))
    return x2, new_state


def _prep_layer_params(l, ln1, ln2, ssm_conv_w, ssm_conv_b, ssm_dt_bias, ssm_a_log, ssm_d,
                       ssm_norm, hgrn_lb_logits, hgrn_norm, fox_q_norm, fox_k_norm, fox_f_bias, lru_conv_w,
                       lru_conv_b, lru_w_r, lru_b_r, lru_w_i, lru_b_i, lru_lambda):
    hid = jnp.arange(GROUP_W) // FOX_HEAD_DIM
    row = lambda v: v.astype(F32).reshape(1, -1)
    return dict(
        ln1=row(ln1[l]), ln2=row(ln2[l]),
        ssm_conv_w=ssm_conv_w[l], ssm_conv_b=row(ssm_conv_b[l]),
        ssm_dt_bias=_pad_lanes(ssm_dt_bias[l]), ssm_a_log=_pad_lanes(ssm_a_log[l]),
        ssm_d=row(jnp.repeat(ssm_d[l], SSM_HEAD_DIM)), ssm_norm=row(ssm_norm[l]),
        hgrn_lb_logits=hgrn_lb_logits.astype(F32), hgrn_norm=row(hgrn_norm[l]),
        head_sum=(hid[:, None] == hid[None, :]).astype(BF16), aug=_aug_constants(),
        fox_q_norm=row(jnp.tile(fox_q_norm[l], FOX_HEADS)), fox_k_norm=row(jnp.tile(fox_k_norm[l], FOX_HEADS)),
        fox_f_bias=_pad_lanes(fox_f_bias[l]),
        lru_conv_w=lru_conv_w[l], lru_conv_b=row(lru_conv_b[l]),
        lru_w_r=lru_w_r[l].astype(BF16), lru_b_r=row(lru_b_r[l]),
        lru_w_i=lru_w_i[l].astype(BF16), lru_b_i=row(lru_b_i[l]), lru_lambda=row(lru_lambda[l]))


def _prep_dense_weights(w_in, w_out, w_gate, w_up, w_down):
    depth = w_in.shape[0]
    o_dt = GROUP_W + SSM_CONV_DIM
    o_cf = o_dt + SSM_HEADS + 7 * GROUP_W
    w_main = jnp.concatenate([w_in[:, :, :o_dt], w_in[:, :, o_dt + SSM_HEADS:o_cf], w_in[:, :, o_cf + FOX_HEADS:]],
                             axis=2)
    w_small = jnp.zeros((depth, D_MODEL, 2 * LANES), F32)
    w_small = w_small.at[:, :, :SSM_HEADS].set(w_in[:, :, o_dt:o_dt + SSM_HEADS])
    w_small = w_small.at[:, :, LANES:LANES + FOX_HEADS].set(w_in[:, :, o_cf:o_cf + FOX_HEADS])
    return dict(w_main=w_main.astype(BF16), w_small=w_small.astype(BF16), w_out=w_out.astype(BF16),
                w_gate=w_gate.astype(BF16), w_up=w_up.astype(BF16), w_down=w_down.astype(BF16))


def kernel(x_prompt, x_sample, cache_fox_k, cache_fox_v, cache_fox_logf, state_ssm_conv, state_ssm, state_hgrn,
           state_lru_conv, state_lru, ln1, ln2, w_in, w_out, ssm_conv_w, ssm_conv_b, ssm_dt_bias, ssm_a_log,
           ssm_d, ssm_norm, hgrn_lb_logits, hgrn_norm, fox_q_norm, fox_k_norm, fox_f_bias, lru_conv_w,
           lru_conv_b, lru_w_r, lru_b_r, lru_w_i, lru_b_i, lru_lambda, w_gate, w_up, w_down):
    depth = ln1.shape[0]
    bp, tp, _ = x_prompt.shape
    bs, ts, _ = x_sample.shape
    fresh = (jnp.zeros((bp, SSM_CONV - 1, SSM_CONV_DIM), F32),
             jnp.zeros((bp, SSM_HEADS, SSM_STATE, SSM_HEAD_DIM), F32),
             jnp.zeros((bp, HGRN_HEADS, HGRN_HEAD_DIM, HGRN_HEAD_DIM), F32),
             None, None, None,
             jnp.zeros((bp, LRU_CONV - 1, GROUP_W), F32),
             jnp.zeros((bp, GROUP_W), F32))
    yp = x_prompt.reshape(bp * tp, D_MODEL)
    ys = x_sample.reshape(bs * ts, D_MODEL)
    p_states, s_states = [], []
    dense = _prep_dense_weights(w_in, w_out, w_gate, w_up, w_down)
    for l in range(depth):
        pr = _prep_layer_params(l, ln1, ln2, ssm_conv_w, ssm_conv_b, ssm_dt_bias, ssm_a_log, ssm_d,
                                ssm_norm, hgrn_lb_logits, hgrn_norm, fox_q_norm, fox_k_norm, fox_f_bias,
                                lru_conv_w, lru_conv_b, lru_w_r, lru_b_r, lru_w_i, lru_b_i, lru_lambda)
        pr.update(dense)
        yp, st_p = _layer(yp, bp, tp, fresh, pr, l)
        p_states.append(st_p)
        past = (state_ssm_conv[l], state_ssm[l], state_hgrn[l], cache_fox_k[l], cache_fox_v[l],
                cache_fox_logf[l], state_lru_conv[l], state_lru[l])
        ys, st_s = _layer(ys, bs, ts, past, pr, l)
        s_states.append(st_s)
    stack = lambda states: [jnp.stack(parts, axis=0) for parts in zip(*states)]
    return (yp.reshape(bp, tp, D_MODEL), ys.reshape(bs, ts, D_MODEL), *stack(p_states), *stack(s_states))
```

```python
import functools
import math

import jax
import jax.numpy as jnp
import numpy as np
from jax import lax
from jax.experimental import pallas as pl
from jax.experimental.pallas import tpu as pltpu

F32 = jnp.float32
BF16 = jnp.bfloat16

D_MODEL = 2048
GROUP_W = 512
CHUNK = 64
SUB = 16
SSM_HEADS = 8
SSM_HEAD_DIM = 64
SSM_GROUPS = 2
SSM_STATE = 128
SSM_CONV = 4
SSM_CONV_DIM = 1024
HGRN_HEADS = 4
HGRN_HEAD_DIM = 128
HGRN_F_FLOOR = 1e-30
FOX_HEADS = 8
FOX_HEAD_DIM = 64
LRU_BLOCKS = 4
LRU_BLOCK_W = 128
LRU_CONV = 4
LRU_C = 8.0
EPS = 1e-6
NEG = -1e30

LANES = 128
CONV_PAD = 8
VMEM_LIMIT = 56 * 1024 * 1024

COL_A_Z, COL_A_X, COL_A_BC, COL_B_Q, COL_B_F, COL_B_I, COL_B_G = 0, 1, 2, 3, 4, 5, 6
COL_C_Q, COL_C_K, COL_C_V, COL_C_G, COL_D_X, COL_D_G = 7, 8, 9, 10, 11, 12
N_MAIN_BLOCKS = 13


def _cparams(sem):
    return pltpu.CompilerParams(dimension_semantics=sem, vmem_limit_bytes=VMEM_LIMIT)


def _dot(a, b):
    return jnp.dot(a, b, preferred_element_type=F32)


def _dot_nt(a, b):
    return lax.dot_general(a, b, (((1,), (1,)), ((), ())), preferred_element_type=F32)


def _dot_tn(a, b):
    return lax.dot_general(a, b, (((0,), (0,)), ((), ())), preferred_element_type=F32)


def _split3(x):
    hi = x.astype(BF16)
    r = x - hi.astype(F32)
    mid = r.astype(BF16)
    lo = (r - mid.astype(F32)).astype(BF16)
    return hi, mid, lo


def _dot01(m01, x):
    hi, mid, lo = _split3(x)
    return _dot(m01, hi) + _dot(m01, mid) + _dot(m01, lo)


def _dot01_r(x, m01):
    hi, mid, lo = _split3(x)
    return _dot(hi, m01) + _dot(mid, m01) + _dot(lo, m01)


def _tri(n, lower=True):
    r = lax.broadcasted_iota(jnp.int32, (n, n), 0)
    c = lax.broadcasted_iota(jnp.int32, (n, n), 1)
    return (r >= c) if lower else (r <= c)


def _sigmoid(x):
    return jax.nn.sigmoid(x)


def _silu(x):
    return x * jax.nn.sigmoid(x)


def _softplus(x):
    return jnp.maximum(x, 0.0) + jnp.log1p(jnp.exp(-jnp.abs(x)))


def _log_sigmoid(x):
    return -_softplus(-x)


def _in_proj_kernel(x_ref, g_ref, w_ref, ws_ref, o_ref, os_ref, h_scr):
    @pl.when(pl.program_id(1) == 0)
    def _():
        x = x_ref[...]
        ms = jnp.mean(x * x, axis=-1, keepdims=True)
        h = (x * lax.rsqrt(ms + EPS) * g_ref[...]).astype(BF16)
        h_scr[...] = h
        os_ref[...] = _dot(h, ws_ref[...])

    o_ref[...] = _dot(h_scr[...], w_ref[...])


def _in_proj(x2d, ln, w_main, w_small, layer, tm):
    m = x2d.shape[0]
    n_main, n_small = w_main.shape[2], w_small.shape[2]
    return pl.pallas_call(
        _in_proj_kernel,
        grid=(m // tm, n_main // GROUP_W),
        in_specs=[
            pl.BlockSpec((tm, D_MODEL), lambda i, j: (i, 0)),
            pl.BlockSpec((1, D_MODEL), lambda i, j: (0, 0)),
            pl.BlockSpec((None, D_MODEL, GROUP_W), lambda i, j: (layer, 0, j)),
            pl.BlockSpec((None, D_MODEL, n_small), lambda i, j: (layer, 0, 0)),
        ],
        out_specs=[
            pl.BlockSpec((tm, GROUP_W), lambda i, j: (i, j)),
            pl.BlockSpec((tm, n_small), lambda i, j: (i, 0)),
        ],
        out_shape=[jax.ShapeDtypeStruct((m, n_main), F32), jax.ShapeDtypeStruct((m, n_small), F32)],
        scratch_shapes=[pltpu.VMEM((tm, D_MODEL), BF16)],
        compiler_params=_cparams(("parallel", "arbitrary")),
        name="in_proj",
    )(x2d, ln, w_main, w_small)


def _conv_block(xp_scr, cw_ref, cb_ref, tb, kw):
    first = CONV_PAD - (kw - 1)
    y = cb_ref[...]
    for j in range(kw):
        y = y + cw_ref[j:j + 1, :] * xp_scr[pl.ds(first + j, tb), :]
    tail = xp_scr[pl.ds(first + tb, kw - 1), :]
    xp_scr[pl.ds(first, kw - 1), :] = tail
    return y, tail


def _ssd_kernel(z_ref, xlo_ref, xhi_ref, sm_ref, conv0_ref, s0_ref, cw_ref, cb_ref, dtb_ref, alog_ref,
                dexp_ref, nrm_ref, y_ref, convo_ref, so_ref,
                xp_scr, xbc_scr, g_scr, dt_scr, y_scr, s_scr, *, tb):
    c = pl.program_id(1)
    last = pl.num_programs(1) - 1
    hpg = SSM_HEADS // SSM_GROUPS
    gw = hpg * SSM_HEAD_DIM

    @pl.when(c == 0)
    def _():
        s_scr[...] = s0_ref[...]
        xp_scr[0:CONV_PAD, :] = jnp.zeros((CONV_PAD, SSM_CONV_DIM), F32)
        xp_scr[CONV_PAD - (SSM_CONV - 1):CONV_PAD, :] = conv0_ref[...]

    xp_scr[CONV_PAD:CONV_PAD + tb, 0:GROUP_W] = xlo_ref[...]
    xp_scr[CONV_PAD:CONV_PAD + tb, GROUP_W:SSM_CONV_DIM] = xhi_ref[...]
    conv, tail = _conv_block(xp_scr, cw_ref, cb_ref, tb, SSM_CONV)

    @pl.when(c == last)
    def _():
        convo_ref[...] = tail

    xbc_scr[...] = _silu(conv)
    dt = _softplus(sm_ref[...] + dtb_ref[...])
    dt_scr[...] = dt
    g_scr[...] = dt * (-jnp.exp(alog_ref[...]))

    tri_mask = _tri(CHUNK)
    tri_l = tri_mask.astype(BF16)
    tri_u = _tri(CHUNK, lower=False).astype(BF16)
    b_off = GROUP_W
    c_off = GROUP_W + SSM_GROUPS * SSM_STATE

    def chunk(ci, carry):
        rows = pl.ds(pl.multiple_of(ci * CHUNK, CHUNK), CHUNK)
        gc = g_scr[rows, :]
        dtc = dt_scr[rows, :]
        cum = _dot01(tri_l, gc)
        cum_t = _dot01_r(gc.T, tri_u)
        tot = cum[CHUNK - 1:CHUNK, :]
        ys = []
        for grp in range(SSM_GROUPS):
            bm = xbc_scr[rows, b_off + grp * SSM_STATE:b_off + (grp + 1) * SSM_STATE].astype(BF16)
            cm = xbc_scr[rows, c_off + grp * SSM_STATE:c_off + (grp + 1) * SSM_STATE].astype(BF16)
            gmat = _dot_nt(cm, bm)
            s_g = s_scr[grp]
            cs = _dot(cm, s_g.astype(BF16))
            vdec, etot = [], []
            for hh in range(hpg):
                h = grp * hpg + hh
                col = cum[:, h:h + 1]
                row = cum_t[h:h + 1, :]
                dec = jnp.where(tri_mask, jnp.exp(jnp.minimum(col - row, 0.0)), 0.0)
                att = (gmat * dec).astype(BF16)
                xh = xbc_scr[rows, h * SSM_HEAD_DIM:(h + 1) * SSM_HEAD_DIM]
                vh = xh * dtc[:, h:h + 1]
                o = _dot(att, vh.astype(BF16)) + jnp.exp(col) * cs[:, hh * SSM_HEAD_DIM:(hh + 1) * SSM_HEAD_DIM]
                ys.append(o + dexp_ref[:, h * SSM_HEAD_DIM:(h + 1) * SSM_HEAD_DIM] * xh)
                toth = tot[:, h:h + 1]
                vdec.append(vh * jnp.exp(toth - col))
                etot.append(jnp.broadcast_to(jnp.exp(toth), (1, SSM_HEAD_DIM)))
            vdec = jnp.concatenate(vdec, axis=1).astype(BF16)
            etot = jnp.concatenate(etot, axis=1)
            s_scr[grp] = etot * s_g + _dot_tn(bm, vdec)
        y_scr[rows, :] = jnp.concatenate(ys, axis=1)
        return carry

    lax.fori_loop(0, tb // CHUNK, chunk, 0)

    y = y_scr[...] * _silu(z_ref[...])
    outs = []
    for grp in range(SSM_GROUPS):
        yg = y[:, grp * gw:(grp + 1) * gw]
        ms = jnp.mean(yg * yg, axis=-1, keepdims=True)
        outs.append(yg * lax.rsqrt(ms + EPS))
    y_ref[...] = (jnp.concatenate(outs, axis=1) * nrm_ref[...]).astype(y_ref.dtype)

    @pl.when(c == last)
    def _():
        so_ref[...] = s_scr[...]


def _ssd(proj, small, conv0, s0, cw, cb, dtb, alog, dexp, nrm, nb, t, tb):
    nc = t // tb
    hpg = SSM_HEADS // SSM_GROUPS
    sshape = (SSM_GROUPS, SSM_STATE, hpg * SSM_HEAD_DIM)
    row = lambda col: pl.BlockSpec((tb, GROUP_W), lambda b, c: (b * nc + c, col))
    full = lambda a: pl.BlockSpec(a.shape, lambda b, c: (0,) * a.ndim)
    return pl.pallas_call(
        functools.partial(_ssd_kernel, tb=tb),
        grid=(nb, nc),
        in_specs=[row(COL_A_Z), row(COL_A_X), row(COL_A_BC),
                  pl.BlockSpec((tb, LANES), lambda b, c: (b * nc + c, 0)),
                  pl.BlockSpec((None, SSM_CONV - 1, SSM_CONV_DIM), lambda b, c: (b, 0, 0)),
                  pl.BlockSpec((None,) + sshape, lambda b, c: (b, 0, 0, 0)),
                  full(cw), full(cb), full(dtb), full(alog), full(dexp), full(nrm)],
        out_specs=[pl.BlockSpec((tb, GROUP_W), lambda b, c: (b * nc + c, 0)),
                   pl.BlockSpec((None, SSM_CONV - 1, SSM_CONV_DIM), lambda b, c: (b, 0, 0)),
                   pl.BlockSpec((None,) + sshape, lambda b, c: (b, 0, 0, 0))],
        out_shape=[jax.ShapeDtypeStruct((nb * t, GROUP_W), BF16),
                   jax.ShapeDtypeStruct((nb, SSM_CONV - 1, SSM_CONV_DIM), F32),
                   jax.ShapeDtypeStruct((nb,) + sshape, F32)],
        scratch_shapes=[pltpu.VMEM((CONV_PAD + tb, SSM_CONV_DIM), F32),
                        pltpu.VMEM((tb, SSM_CONV_DIM), F32),
                        pltpu.VMEM((tb, LANES), F32),
                        pltpu.VMEM((tb, LANES), F32),
                        pltpu.VMEM((tb, GROUP_W), F32),
                        pltpu.VMEM(sshape, F32)],
        compiler_params=_cparams(("parallel", "arbitrary")),
        name="ssd",
    )(proj, proj, proj, small, conv0, s0, cw, cb, dtb, alog, dexp, nrm)


def _hgrn_kernel(q_ref, f_ref, i_ref, g_ref, s0_ref, lbl_ref, nrm_ref, y_ref, so_ref,
                 c_scr, k_scr, st_scr, *, tb, layer):
    c = pl.program_id(1)
    last = pl.num_programs(1) - 1
    hd = HGRN_HEAD_DIM
    nsub = CHUNK // SUB

    @pl.when(c == 0)
    def _():
        for h in range(HGRN_HEADS):
            st_scr[h] = s0_ref[h].T

    ll = lbl_ref[...]
    e = jnp.exp(ll - jnp.max(ll, axis=0, keepdims=True))
    p = e / jnp.sum(e, axis=0, keepdims=True)
    cs = p[0:1, :]
    for l in range(1, layer + 1):
        cs = cs + p[l:l + 1, :]
    lb = cs - p[0:1, :]

    tri_l = _tri(CHUNK).astype(BF16)
    rowid = lax.broadcasted_iota(jnp.int32, (CHUNK, 1), 0)
    subid = lax.broadcasted_iota(jnp.int32, (SUB, 1), 0)

    def chunk(ci, carry):
        r0 = pl.multiple_of(ci * CHUNK, CHUNK)
        rows = pl.ds(r0, CHUNK)
        fr = f_ref[rows, :]
        fg = lb + (1.0 - lb) * _sigmoid(fr)
        logf = jnp.log(jnp.maximum(fg, HGRN_F_FLOOR))
        hk = (1.0 - lb) * _sigmoid(-fr)
        hq = _silu(q_ref[rows, :])
        v = i_ref[rows, :]
        vb = v.astype(BF16)
        cum = _dot01(tri_l, logf)
        tot = cum[CHUNK - 1:CHUNK, :]
        c_scr[...] = cum
        k_scr[...] = hk
        qe = (hq * jnp.exp(cum)).astype(BF16)
        kd = (hk * jnp.exp(tot - cum)).astype(BF16)
        etot = jnp.exp(tot)

        qcat, kcat = [], []
        for j in range(nsub - 1):
            ej = cum[(j + 1) * SUB - 1:(j + 1) * SUB, :]
            qcat.append(jnp.where(rowid >= (j + 1) * SUB, hq * jnp.exp(jnp.minimum(cum - ej, 0.0)), 0.0).astype(BF16))
            in_j = (rowid >= j * SUB) & (rowid < (j + 1) * SUB)
            kcat.append(jnp.where(in_j, hk * jnp.exp(jnp.minimum(ej - cum, 0.0)), 0.0).astype(BF16))

        o_heads = []
        for h in range(HGRN_HEADS):
            hs = slice(h * hd, (h + 1) * hd)
            qc = jnp.concatenate([q[:, hs] for q in qcat], axis=1)
            kc = jnp.concatenate([k[:, hs] for k in kcat], axis=1)
            a_off = _dot_nt(qc, kc)
            st = st_scr[h]
            o_heads.append(_dot(a_off.astype(BF16), vb[:, hs]) + _dot_nt(qe[:, hs], st.astype(BF16)))
            st_scr[h] = st * etot[:, hs] + _dot_tn(vb[:, hs], kd[:, hs])
        o = jnp.concatenate(o_heads, axis=1)

        o_sub = []
        for i in range(nsub):
            base = i * SUB
            ci_ = cum[base:base + SUB, :]
            qi = hq[base:base + SUB, :]
            acc = [jnp.zeros((SUB, hd), F32) for _ in range(HGRN_HEADS)]
            for j in range(SUB):
                crow = c_scr[pl.ds(base + j, 1), :]
                krow = k_scr[pl.ds(base + j, 1), :]
                vrow = i_ref[pl.ds(r0 + base + j, 1), :]
                x = jnp.where(subid >= j, qi * jnp.exp(jnp.minimum(ci_ - crow, 0.0)) * krow, 0.0)
                for h in range(HGRN_HEADS):
                    hs = slice(h * hd, (h + 1) * hd)
                    a = jnp.sum(x[:, hs], axis=-1, keepdims=True)
                    acc[h] = acc[h] + a * vrow[:, hs]
            o_sub.append(jnp.concatenate(acc, axis=1))
        o = o + jnp.concatenate(o_sub, axis=0)

        outs = []
        for h in range(HGRN_HEADS):
            oh = o[:, h * hd:(h + 1) * hd]
            ms = jnp.mean(oh * oh, axis=-1, keepdims=True)
            outs.append(oh * lax.rsqrt(ms + EPS))
        y = jnp.concatenate(outs, axis=1) * nrm_ref[...] * _silu(g_ref[rows, :])
        y_ref[rows, :] = y.astype(y_ref.dtype)
        return carry

    lax.fori_loop(0, tb // CHUNK, chunk, 0)

    @pl.when(c == last)
    def _():
        for h in range(HGRN_HEADS):
            so_ref[h] = st_scr[h].T


def _hgrn(proj, s0, lbl, nrm, nb, t, tb, layer):
    nc = t // tb
    sshape = (HGRN_HEADS, HGRN_HEAD_DIM, HGRN_HEAD_DIM)
    row = lambda col: pl.BlockSpec((tb, GROUP_W), lambda b, c: (b * nc + c, col))
    full = lambda a: pl.BlockSpec(a.shape, lambda b, c: (0,) * a.ndim)
    return pl.pallas_call(
        functools.partial(_hgrn_kernel, tb=tb, layer=layer),
        grid=(nb, nc),
        in_specs=[row(COL_B_Q), row(COL_B_F), row(COL_B_I), row(COL_B_G),
                  pl.BlockSpec((None,) + sshape, lambda b, c: (b, 0, 0, 0)),
                  full(lbl), full(nrm)],
        out_specs=[pl.BlockSpec((tb, GROUP_W), lambda b, c: (b * nc + c, 0)),
                   pl.BlockSpec((None,) + sshape, lambda b, c: (b, 0, 0, 0))],
        out_shape=[jax.ShapeDtypeStruct((nb * t, GROUP_W), BF16),
                   jax.ShapeDtypeStruct((nb,) + sshape, F32)],
        scratch_shapes=[pltpu.VMEM((CHUNK, GROUP_W), F32),
                        pltpu.VMEM((CHUNK, GROUP_W), F32),
                        pltpu.VMEM(sshape, F32)],
        compiler_params=_cparams(("parallel", "arbitrary")),
        name="hgrn2",
    )(proj, proj, proj, proj, s0, lbl, nrm)


def _lru_kernel(x_ref, g_ref, conv0_ref, h0_ref, cw_ref, cb_ref, wr_ref, br_ref, wi_ref, bi_ref, lam_ref,
                y_ref, convo_ref, ho_ref, xp_scr, h_scr, *, tb, reset_first):
    c = pl.program_id(1)
    last = pl.num_programs(1) - 1

    @pl.when(c == 0)
    def _():
        h_scr[...] = h0_ref[...]
        xp_scr[0:CONV_PAD, :] = jnp.zeros((CONV_PAD, GROUP_W), F32)
        xp_scr[CONV_PAD - (LRU_CONV - 1):CONV_PAD, :] = conv0_ref[...]

    xp_scr[CONV_PAD:CONV_PAD + tb, :] = x_ref[...]
    xc, tail = _conv_block(xp_scr, cw_ref, cb_ref, tb, LRU_CONV)

    @pl.when(c == last)
    def _():
        convo_ref[...] = tail

    xcb = xc.astype(BF16)

    def blockdiag(w_ref):
        return jnp.concatenate(
            [_dot(xcb[:, n * LRU_BLOCK_W:(n + 1) * LRU_BLOCK_W], w_ref[n]) for n in range(LRU_BLOCKS)], axis=1)

    r = _sigmoid(blockdiag(wr_ref) + br_ref[...])
    gi = _sigmoid(blockdiag(wi_ref) + bi_ref[...])
    log_a = LRU_C * r * _log_sigmoid(lam_ref[...])
    a = jnp.exp(log_a)
    mult = jnp.sqrt(-jnp.tanh(log_a) * (a * a + 1.0))
    rowid = lax.broadcasted_iota(jnp.int32, (tb, 1), 0)
    first_row = rowid == 0
    if reset_first:
        rst = first_row & (c == 0)
        a = jnp.where(rst, 0.0, a)
        mult = jnp.where(rst, 1.0, mult)
    u = mult * gi * xc
    u = u + jnp.where(first_row, a * h_scr[...], 0.0)

    av, bv = a, u
    d = 1
    while d < tb:
        m = rowid >= d
        a_sh = pltpu.roll(av, d, axis=0)
        b_sh = pltpu.roll(bv, d, axis=0)
        bv = jnp.where(m, av * b_sh + bv, bv)
        av = jnp.where(m, av * a_sh, av)
        d *= 2
    h_new = bv[tb - 1:tb, :]
    h_scr[...] = h_new
    y_ref[...] = (bv * jax.nn.gelu(g_ref[...])).astype(y_ref.dtype)

    @pl.when(c == last)
    def _():
        ho_ref[...] = h_new


def _lru(proj, conv0, h0, cw, cb, wr, br, wi, bi, lam, nb, t, tb, reset_first):
    nc = t // tb
    row = lambda col: pl.BlockSpec((tb, GROUP_W), lambda b, c: (b * nc + c, col))
    full = lambda a: pl.BlockSpec(a.shape, lambda b, c: (0,) * a.ndim)
    return pl.pallas_call(
        functools.partial(_lru_kernel, tb=tb, reset_first=reset_first),
        grid=(nb, nc),
        in_specs=[row(COL_D_X), row(COL_D_G),
                  pl.BlockSpec((None, LRU_CONV - 1, GROUP_W), lambda b, c: (b, 0, 0)),
                  pl.BlockSpec((None, 1, GROUP_W), lambda b, c: (b, 0, 0)),
                  full(cw), full(cb), full(wr), full(br), full(wi), full(bi), full(lam)],
        out_specs=[pl.BlockSpec((tb, GROUP_W), lambda b, c: (b * nc + c, 0)),
                   pl.BlockSpec((None, LRU_CONV - 1, GROUP_W), lambda b, c: (b, 0, 0)),
                   pl.BlockSpec((None, 1, GROUP_W), lambda b, c: (b, 0, 0))],
        out_shape=[jax.ShapeDtypeStruct((nb * t, GROUP_W), BF16),
                   jax.ShapeDtypeStruct((nb, LRU_CONV - 1, GROUP_W), F32),
                   jax.ShapeDtypeStruct((nb, 1, GROUP_W), F32)],
        scratch_shapes=[pltpu.VMEM((CONV_PAD + tb, GROUP_W), F32),
                        pltpu.VMEM((1, GROUP_W), F32)],
        compiler_params=_cparams(("parallel", "arbitrary")),
        name="rglru",
    )(proj, proj, conv0, h0, cw, cb, wr, br, wi, bi, lam)


def _gate_kernel(*refs, tb, past):
    if past:
        sm_ref, fb_ref, plf_ref, lf_ref, cum_ref, cump_ref, carry_scr = refs
    else:
        sm_ref, fb_ref, lf_ref, cum_ref, carry_scr = refs

    @pl.when(pl.program_id(1) == 0)
    def _():
        if past:
            cp = _dot01(_tri(past).astype(BF16), plf_ref[...])
            cump_ref[...] = cp
            carry_scr[...] = cp[past - 1:past, :]
        else:
            carry_scr[...] = jnp.zeros_like(carry_scr)

    lf = _log_sigmoid(sm_ref[...] + fb_ref[...])
    lf_ref[...] = lf
    cum = _dot01(_tri(tb).astype(BF16), lf) + carry_scr[...]
    cum_ref[...] = cum
    carry_scr[...] = cum[tb - 1:tb, :]


def _gate(small, fb, past_lf, nb, t, tb):
    nc = t // tb
    past = 0 if past_lf is None else past_lf.shape[1]
    rows = pl.BlockSpec((tb, LANES), lambda b, c: (b * nc + c, 0))
    in_specs = [pl.BlockSpec((tb, LANES), lambda b, c: (b * nc + c, 1)),
                pl.BlockSpec(fb.shape, lambda b, c: (0, 0))]
    out_specs = [rows, rows]
    out_shape = [jax.ShapeDtypeStruct((nb * t, LANES), F32)] * 2
    args = [small, fb]
    if past:
        in_specs.append(pl.BlockSpec((None, past, LANES), lambda b, c: (b, 0, 0)))
        out_specs.append(pl.BlockSpec((None, past, LANES), lambda b, c: (b, 0, 0)))
        out_shape.append(jax.ShapeDtypeStruct((nb, past, LANES), F32))
        args.append(past_lf)
    return pl.pallas_call(
        functools.partial(_gate_kernel, tb=tb, past=past),
        grid=(nb, nc),
        in_specs=in_specs, out_specs=out_specs, out_shape=out_shape,
        scratch_shapes=[pltpu.VMEM((1, LANES), F32)],
        compiler_params=_cparams(("parallel", "arbitrary")),
        name="fox_gate",
    )(*args)


def _head_norm(x, gain, hsum):
    ms = _dot01_r(x * x, hsum) * (1.0 / FOX_HEAD_DIM)
    return x * lax.rsqrt(ms + EPS) * gain


def _store_heads(ref, x):
    tb = x.shape[0]
    for h in range(FOX_HEADS):
        ref[pl.ds(h, tb, stride=FOX_HEADS), :] = x[:, h * FOX_HEAD_DIM:(h + 1) * FOX_HEAD_DIM]


def _foxprep_kernel(q_ref, k_ref, v_ref, hsum_ref, qn_ref, kn_ref, fq_ref, fk_ref, fv_ref, kb_ref, vb_ref):
    hsum = hsum_ref[...]
    fq_ref[...] = (_head_norm(q_ref[...], qn_ref[...], hsum) * (FOX_HEAD_DIM ** -0.5)).astype(BF16)
    fk = _head_norm(k_ref[...], kn_ref[...], hsum)
    _store_heads(fk_ref, fk)
    kb_ref[...] = fk.astype(BF16)
    v = v_ref[...]
    _store_heads(fv_ref, v)
    vb_ref[...] = v.astype(BF16)


def _heads_spec(tb):
    return pl.BlockSpec((tb * FOX_HEADS, FOX_HEAD_DIM), lambda i: (i, 0))


def _foxprep(proj, hsum, qn, kn, tb):
    m = proj.shape[0]
    row = lambda col: pl.BlockSpec((tb, GROUP_W), lambda i: (i, col))
    full = lambda a: pl.BlockSpec(a.shape, lambda i: (0,) * a.ndim)
    out_row = pl.BlockSpec((tb, GROUP_W), lambda i: (i, 0))
    return pl.pallas_call(
        _foxprep_kernel,
        grid=(m // tb,),
        in_specs=[row(COL_C_Q), row(COL_C_K), row(COL_C_V), full(hsum), full(qn), full(kn)],
        out_specs=[out_row, _heads_spec(tb), _heads_spec(tb), out_row, out_row],
        out_shape=[jax.ShapeDtypeStruct((m, GROUP_W), BF16),
                   jax.ShapeDtypeStruct((m * FOX_HEADS, FOX_HEAD_DIM), F32),
                   jax.ShapeDtypeStruct((m * FOX_HEADS, FOX_HEAD_DIM), F32),
                   jax.ShapeDtypeStruct((m, GROUP_W), BF16),
                   jax.ShapeDtypeStruct((m, GROUP_W), BF16)],
        compiler_params=_cparams(("parallel",)),
        name="fox_prep",
    )(proj, proj, proj, hsum, qn, kn)


AUG_W = FOX_HEADS * LANES
N_PIECES = 3


def _aug_constants():
    pq = np.zeros((N_PIECES * LANES, AUG_W), np.float32)
    pk = np.zeros((N_PIECES * LANES, AUG_W), np.float32)
    ones_q = np.zeros((1, AUG_W), np.float32)
    ones_k = np.zeros((1, AUG_W), np.float32)
    ones_v = np.zeros((1, AUG_W), np.float32)
    head = np.zeros((1, AUG_W), np.float32)
    for h in range(FOX_HEADS):
        own = h * LANES + FOX_HEAD_DIM * (h % 2)
        other = h * LANES + FOX_HEAD_DIM * (1 - h % 2)
        head[0, own:own + FOX_HEAD_DIM] = 1.0
        for piece in range(N_PIECES):
            pq[piece * LANES + h, other + piece] = 1.0
            ones_k[0, other + piece] = 1.0
            ones_q[0, other + N_PIECES + piece] = 1.0
            pk[piece * LANES + h, other + N_PIECES + piece] = -1.0
        ones_v[0, other] = 1.0
    return dict(aug_pq=jnp.asarray(pq, BF16), aug_pk=jnp.asarray(pk, BF16), aug_ones_q=jnp.asarray(ones_q),
                aug_ones_k=jnp.asarray(ones_k), aug_ones_v=jnp.asarray(ones_v), aug_head=jnp.asarray(head))


def _foxprep_aug_kernel(q_ref, k_ref, v_ref, cum_ref, hsum_ref, qn_ref, kn_ref, pq_ref, pk_ref, oq_ref, ok_ref,
                        ov_ref, head_ref, fk_ref, fv_ref, qa_ref, ka_ref, va_ref):
    hsum = hsum_ref[...]
    fq = _head_norm(q_ref[...], qn_ref[...], hsum) * (FOX_HEAD_DIM ** -0.5)
    fk = _head_norm(k_ref[...], kn_ref[...], hsum)
    v = v_ref[...]
    _store_heads(fk_ref, fk)
    _store_heads(fv_ref, v)
    pieces = jnp.concatenate(_split3(cum_ref[...]), axis=1)
    own = head_ref[...] > 0.5

    def per_head(x):
        return jnp.concatenate([x[:, (h // 2) * LANES:(h // 2 + 1) * LANES] for h in range(FOX_HEADS)], axis=1)

    qa = jnp.where(own, per_head(fq), _dot(pieces, pq_ref[...]) + oq_ref[...]).astype(BF16)
    ka = jnp.where(own, per_head(fk), _dot(pieces, pk_ref[...]) + ok_ref[...]).astype(BF16)
    va = jnp.where(own, per_head(v), ov_ref[...]).astype(BF16)
    for h in range(FOX_HEADS):
        hs = slice(h * LANES, (h + 1) * LANES)
        qa_ref[h] = qa[:, hs]
        ka_ref[h] = ka[:, hs]
        va_ref[h] = va[:, hs]


def _foxprep_aug(proj, cum, hsum, qn, kn, aug, tb):
    m = proj.shape[0]
    row = lambda col: pl.BlockSpec((tb, GROUP_W), lambda i: (i, col))
    full = lambda a: pl.BlockSpec(a.shape, lambda i: (0,) * a.ndim)
    out_row = pl.BlockSpec((tb, GROUP_W), lambda i: (i, 0))
    out_aug = pl.BlockSpec((FOX_HEADS, tb, LANES), lambda i: (0, i, 0))
    consts = [aug['aug_pq'], aug['aug_pk'], aug['aug_ones_q'], aug['aug_ones_k'], aug['aug_ones_v'],
              aug['aug_head']]
    return pl.pallas_call(
        _foxprep_aug_kernel,
        grid=(m // tb,),
        in_specs=[row(COL_C_Q), row(COL_C_K), row(COL_C_V), pl.BlockSpec((tb, LANES), lambda i: (i, 0)),
                  full(hsum), full(qn), full(kn)] + [full(a) for a in consts],
        out_specs=[_heads_spec(tb), _heads_spec(tb), out_aug, out_aug, out_aug],
        out_shape=[jax.ShapeDtypeStruct((m * FOX_HEADS, FOX_HEAD_DIM), F32),
                   jax.ShapeDtypeStruct((m * FOX_HEADS, FOX_HEAD_DIM), F32)]
                  + [jax.ShapeDtypeStruct((FOX_HEADS, m, LANES), BF16)] * 3,
        compiler_params=_cparams(("parallel",)),
        name="fox_prep_aug",
    )(proj, proj, proj, cum, hsum, qn, kn, *consts)


def _attn_aug_kernel(q_ref, k_ref, v_ref, g_ref, y_ref, *, tq, hg):
    qi = pl.program_id(2)
    causal = _tri(tq)
    low = lax.broadcasted_iota(jnp.int32, (1, LANES), 1) < FOX_HEAD_DIM
    qs = [q_ref[i] for i in range(hg)]

    def step(j, carry, masked):
        rows = pl.ds(pl.multiple_of(j * tq, tq), tq)
        new = []
        for i in range(hg):
            m, acc = carry[i]
            s = _dot_nt(qs[i], k_ref[i, rows, :])
            if masked:
                s = jnp.where(causal, s, NEG)
            m_new = jnp.maximum(m, jnp.max(s, axis=-1, keepdims=True))
            p = jnp.exp(s - m_new).astype(BF16)
            acc = jnp.exp(m - m_new) * acc + _dot(p, v_ref[i, rows, :])
            new.append((m_new, acc))
        return tuple(new)

    init = tuple((jnp.full((tq, 1), NEG, F32), jnp.zeros((tq, LANES), F32)) for _ in range(hg))
    carry = lax.fori_loop(0, qi, lambda j, c: step(j, c, False), init)
    carry = step(qi, carry, True)
    outs = []
    for i in range(0, hg, 2):
        acc_e, acc_o = carry[i][1], carry[i + 1][1]
        out_e = acc_e / acc_e[:, FOX_HEAD_DIM:FOX_HEAD_DIM + 1]
        out_o = acc_o / acc_o[:, 0:1]
        outs.append(jnp.where(low, out_e, out_o))
    y = jnp.concatenate(outs, axis=1) * _sigmoid(g_ref[...])
    y_ref[...] = y.astype(y_ref.dtype)


def _attention_aug(qa, ka, va, proj, nb, t, tq, hg):
    nq = t // tq
    ngrp = FOX_HEADS // hg
    wout = hg * FOX_HEAD_DIM
    gate_col0 = COL_C_G * GROUP_W // wout
    return pl.pallas_call(
        functools.partial(_attn_aug_kernel, tq=tq, hg=hg),
        grid=(nb, ngrp, nq),
        in_specs=[pl.BlockSpec((hg, tq, LANES), lambda b, g, i: (g, b * nq + i, 0)),
                  pl.BlockSpec((hg, t, LANES), lambda b, g, i: (g, b, 0)),
                  pl.BlockSpec((hg, t, LANES), lambda b, g, i: (g, b, 0)),
                  pl.BlockSpec((tq, wout), lambda b, g, i: (b * nq + i, gate_col0 + g))],
        out_specs=pl.BlockSpec((tq, wout), lambda b, g, i: (b * nq + i, g)),
        out_shape=jax.ShapeDtypeStruct((nb * t, GROUP_W), BF16),
        compiler_params=_cparams(("parallel", "parallel", "arbitrary")),
        name="fox_attention_aug",
    )(qa, ka, va, proj)


def _attn_cached_kernel(q_ref, cq_ref, ck_ref, k_ref, v_ref, g_ref, pk_ref, pv_ref, y_ref, *, t, past):
    causal = _tri(t)
    outs = []
    for h in range(FOX_HEADS):
        hs = slice(h * FOX_HEAD_DIM, (h + 1) * FOX_HEAD_DIM)
        qh = q_ref[:, hs]
        pk = pk_ref[pl.ds(h, past, stride=FOX_HEADS), :].astype(BF16)
        pv = pv_ref[pl.ds(h, past, stride=FOX_HEADS), :].astype(BF16)
        cqh = cq_ref[:, h:h + 1]
        s_p = _dot_nt(qh, pk) + cqh - ck_ref[h:h + 1, 0:past]
        s_c = jnp.where(causal, _dot_nt(qh, k_ref[:, hs]) + cqh - ck_ref[h:h + 1, past:past + t], NEG)
        m = jnp.maximum(jnp.max(s_p, axis=-1, keepdims=True), jnp.max(s_c, axis=-1, keepdims=True))
        p_p = jnp.exp(s_p - m)
        p_c = jnp.exp(s_c - m)
        l = jnp.sum(p_p, axis=-1, keepdims=True) + jnp.sum(p_c, axis=-1, keepdims=True)
        acc = _dot(p_p.astype(BF16), pv) + _dot(p_c.astype(BF16), v_ref[:, hs])
        outs.append(acc / l)
    y = jnp.concatenate(outs, axis=1) * _sigmoid(g_ref[...])
    y_ref[...] = y.astype(y_ref.dtype)


def _attention_cached(fq, cq, ck_t, kb, vb, proj, past_k, past_v, layer, nb, t):
    past = past_k.shape[1] // FOX_HEADS
    ltot = ck_t.shape[-1]
    seq = lambda rows: pl.BlockSpec((None, rows, GROUP_W), lambda b: (b, 0, 0))
    cache = pl.BlockSpec((None, past * FOX_HEADS, FOX_HEAD_DIM), lambda b: (layer * nb + b, 0, 0))
    return pl.pallas_call(
        functools.partial(_attn_cached_kernel, t=t, past=past),
        grid=(nb,),
        in_specs=[pl.BlockSpec((t, GROUP_W), lambda b: (b, 0)),
                  pl.BlockSpec((None, t, LANES), lambda b: (b, 0, 0)),
                  pl.BlockSpec((None, FOX_HEADS, ltot), lambda b: (b, 0, 0)),
                  seq(t), seq(t),
                  pl.BlockSpec((t, GROUP_W), lambda b: (b, COL_C_G)),
                  cache, cache],
        out_specs=pl.BlockSpec((t, GROUP_W), lambda b: (b, 0)),
        out_shape=jax.ShapeDtypeStruct((nb * t, GROUP_W), BF16),
        compiler_params=_cparams(("parallel",)),
        name="fox_attention_cached",
    )(fq, cq, ck_t, kb.reshape(nb, t, GROUP_W), vb.reshape(nb, t, GROUP_W), proj, past_k, past_v)


def _out_proj_kernel(x_ref, ya_ref, yb_ref, yc_ref, yd_ref, w_ref, o_ref):
    acc = x_ref[...]
    for n, y_ref in enumerate((ya_ref, yb_ref, yc_ref, yd_ref)):
        acc = acc + _dot(y_ref[...], w_ref[n * GROUP_W:(n + 1) * GROUP_W, :])
    o_ref[...] = acc


def _out_proj(x2d, ys, w_out, layer, tm):
    m = x2d.shape[0]
    yspec = pl.BlockSpec((tm, GROUP_W), lambda i: (i, 0))
    return pl.pallas_call(
        _out_proj_kernel,
        grid=(m // tm,),
        in_specs=[pl.BlockSpec((tm, D_MODEL), lambda i: (i, 0)), yspec, yspec, yspec, yspec,
                  pl.BlockSpec((None,) + w_out.shape[1:], lambda i: (layer, 0, 0))],
        out_specs=pl.BlockSpec((tm, D_MODEL), lambda i: (i, 0)),
        out_shape=jax.ShapeDtypeStruct((m, D_MODEL), F32),
        compiler_params=_cparams(("parallel",)),
        name="out_proj",
    )(x2d, *ys, w_out)


def _ffn_kernel(x_ref, g_ref, wg_ref, wu_ref, wd_ref, o_ref, h_scr, acc_scr):
    f = pl.program_id(1)

    @pl.when(f == 0)
    def _():
        x = x_ref[...]
        ms = jnp.mean(x * x, axis=-1, keepdims=True)
        h_scr[...] = (x * lax.rsqrt(ms + EPS) * g_ref[...]).astype(BF16)
        acc_scr[...] = x

    h = h_scr[...]
    a = _silu(_dot(h, wg_ref[...])) * _dot(h, wu_ref[...])
    acc_scr[...] += _dot(a.astype(BF16), wd_ref[...])

    @pl.when(f == pl.num_programs(1) - 1)
    def _():
        o_ref[...] = acc_scr[...]


def _ffn(x2d, ln, wg, wu, wd, layer, tm, tf):
    m = x2d.shape[0]
    d_ff = wg.shape[2]
    return pl.pallas_call(
        _ffn_kernel,
        grid=(m // tm, d_ff // tf),
        in_specs=[pl.BlockSpec((tm, D_MODEL), lambda i, f: (i, 0)),
                  pl.BlockSpec((1, D_MODEL), lambda i, f: (0, 0)),
                  pl.BlockSpec((None, D_MODEL, tf), lambda i, f: (layer, 0, f)),
                  pl.BlockSpec((None, D_MODEL, tf), lambda i, f: (layer, 0, f)),
                  pl.BlockSpec((None, tf, D_MODEL), lambda i, f: (layer, f, 0))],
        out_specs=pl.BlockSpec((tm, D_MODEL), lambda i, f: (i, 0)),
        out_shape=jax.ShapeDtypeStruct((m, D_MODEL), F32),
        scratch_shapes=[pltpu.VMEM((tm, D_MODEL), BF16), pltpu.VMEM((tm, D_MODEL), F32)],
        compiler_params=_cparams(("parallel", "arbitrary")),
        name="swiglu_ffn",
    )(x2d, ln, wg, wu, wd)


def _pad_lanes(v, offset=0):
    return jnp.zeros((1, LANES), F32).at[0, offset:offset + v.shape[0]].set(v.astype(F32))


def _ssm_state_to_groups(s):
    nb = s.shape[0]
    hpg = SSM_HEADS // SSM_GROUPS
    s = s.reshape(nb, SSM_GROUPS, hpg, SSM_STATE, SSM_HEAD_DIM)
    return jnp.transpose(s, (0, 1, 3, 2, 4)).reshape(nb, SSM_GROUPS, SSM_STATE, hpg * SSM_HEAD_DIM)


def _ssm_state_from_groups(s):
    nb = s.shape[0]
    hpg = SSM_HEADS // SSM_GROUPS
    s = s.reshape(nb, SSM_GROUPS, SSM_STATE, hpg, SSM_HEAD_DIM)
    return jnp.transpose(s, (0, 1, 3, 2, 4)).reshape(nb, SSM_HEADS, SSM_STATE, SSM_HEAD_DIM)


def _block_rows(t, want):
    return want if t % want == 0 else t


def _layer(x2d, nb, t, state, pr, layer):
    (ssm_conv, ssm_s, hgrn_s, fox_k, fox_v, fox_lf, lru_conv, lru_h) = state
    m = nb * t
    past = 0 if fox_k is None else fox_k.shape[2]
    tm = _block_rows(m, 512)
    tb = _block_rows(t, 256)

    proj, small = _in_proj(x2d, pr['ln1'], pr['w_main'], pr['w_small'], layer, _block_rows(m, 1024))

    ya, ssm_conv_new, ssm_g = _ssd(proj, small, ssm_conv, _ssm_state_to_groups(ssm_s), pr['ssm_conv_w'],
                                   pr['ssm_conv_b'], pr['ssm_dt_bias'], pr['ssm_a_log'], pr['ssm_d'],
                                   pr['ssm_norm'], nb, t, tb)
    ssm_s_new = _ssm_state_from_groups(ssm_g)

    yb, hgrn_s_new = _hgrn(proj, hgrn_s, pr['hgrn_lb_logits'], pr['hgrn_norm'], nb, t, tb, layer)

    yd, lru_conv_new, lru_h_new = _lru(proj, lru_conv, lru_h.reshape(nb, 1, GROUP_W), pr['lru_conv_w'],
                                       pr['lru_conv_b'], pr['lru_w_r'], pr['lru_b_r'], pr['lru_w_i'],
                                       pr['lru_b_i'], pr['lru_lambda'], nb, t, tb, reset_first=(past == 0))

    if past:
        past_lf = jnp.pad(fox_lf.astype(F32), ((0, 0), (0, 0), (0, LANES - FOX_HEADS)))
        lf, cum, cum_past = _gate(small, pr['fox_f_bias'], past_lf, nb, t, t)
        fq, fk, fv, kb, vb = _foxprep(proj, pr['head_sum'], pr['fox_q_norm'], pr['fox_k_norm'], tm)
        cq = cum.reshape(nb, t, LANES)
        ck_t = jnp.transpose(jnp.concatenate([cum_past, cq], axis=1)[:, :, :FOX_HEADS], (0, 2, 1))
        rows = lambda c: c.reshape(c.shape[0] * nb, past * FOX_HEADS, FOX_HEAD_DIM)
        yc = _attention_cached(fq, cq, ck_t, kb, vb, proj, rows(fox_k), rows(fox_v), layer, nb, t)
    else:
        lf, cum = _gate(small, pr['fox_f_bias'], None, nb, t, _block_rows(t, 512))
        fk, fv, qa, ka, va = _foxprep_aug(proj, cum, pr['head_sum'], pr['fox_q_norm'], pr['fox_k_norm'],
                                          pr['aug'], tm)
        yc = _attention_aug(qa, ka, va, proj, nb, t, _block_rows(t, 512), FOX_HEADS)
    lf = lf.reshape(nb, t, LANES)

    x1 = _out_proj(x2d, (ya, yb, yc, yd), pr['w_out'], layer, tm)
    x2 = _ffn(x1, pr['ln2'], pr['w_gate'], pr['w_up'], pr['w_down'], layer, tm, 512)

    new_state = (ssm_conv_new, ssm_s_new, hgrn_s_new,
                 fk.reshape(nb, t, FOX_HEADS, FOX_HEAD_DIM), fv.reshape(nb, t, FOX_HEADS, FOX_HEAD_DIM),
                 lf[:, :, :FOX_HEADS], lru_conv_new, lru_h_new.reshape(nb, GROUP_W))
    return x2, new_state


def _prep_layer_params(l, ln1, ln2, ssm_conv_w, ssm_conv_b, ssm_dt_bias, ssm_a_log, ssm_d,
                       ssm_norm, hgrn_lb_logits, hgrn_norm, fox_q_norm, fox_k_norm, fox_f_bias, lru_conv_w,
                       lru_conv_b, lru_w_r, lru_b_r, lru_w_i, lru_b_i, lru_lambda):
    hid = jnp.arange(GROUP_W) // FOX_HEAD_DIM
    row = lambda v: v.astype(F32).reshape(1, -1)
    return dict(
        ln1=row(ln1[l]), ln2=row(ln2[l]),
        ssm_conv_w=ssm_conv_w[l], ssm_conv_b=row(ssm_conv_b[l]),
        ssm_dt_bias=_pad_lanes(ssm_dt_bias[l]), ssm_a_log=_pad_lanes(ssm_a_log[l]),
        ssm_d=row(jnp.repeat(ssm_d[l], SSM_HEAD_DIM)), ssm_norm=row(ssm_norm[l]),
        hgrn_lb_logits=hgrn_lb_logits.astype(F32), hgrn_norm=row(hgrn_norm[l]),
        head_sum=(hid[:, None] == hid[None, :]).astype(BF16), aug=_aug_constants(),
        fox_q_norm=row(jnp.tile(fox_q_norm[l], FOX_HEADS)), fox_k_norm=row(jnp.tile(fox_k_norm[l], FOX_HEADS)),
        fox_f_bias=_pad_lanes(fox_f_bias[l]),
        lru_conv_w=lru_conv_w[l], lru_conv_b=row(lru_conv_b[l]),
        lru_w_r=lru_w_r[l].astype(BF16), lru_b_r=row(lru_b_r[l]),
        lru_w_i=lru_w_i[l].astype(BF16), lru_b_i=row(lru_b_i[l]), lru_lambda=row(lru_lambda[l]))


def _prep_dense_weights(w_in, w_out, w_gate, w_up, w_down):
    depth = w_in.shape[0]
    o_dt = GROUP_W + SSM_CONV_DIM
    o_cf = o_dt + SSM_HEADS + 7 * GROUP_W
    w_main = jnp.concatenate([w_in[:, :, :o_dt], w_in[:, :, o_dt + SSM_HEADS:o_cf], w_in[:, :, o_cf + FOX_HEADS:]],
                             axis=2)
    w_small = jnp.zeros((depth, D_MODEL, 2 * LANES), F32)
    w_small = w_small.at[:, :, :SSM_HEADS].set(w_in[:, :, o_dt:o_dt + SSM_HEADS])
    w_small = w_small.at[:, :, LANES:LANES + FOX_HEADS].set(w_in[:, :, o_cf:o_cf + FOX_HEADS])
    return dict(w_main=w_main.astype(BF16), w_small=w_small.astype(BF16), w_out=w_out.astype(BF16),
                w_gate=w_gate.astype(BF16), w_up=w_up.astype(BF16), w_down=w_down.astype(BF16))


def kernel(x_prompt, x_sample, cache_fox_k, cache_fox_v, cache_fox_logf, state_ssm_conv, state_ssm, state_hgrn,
           state_lru_conv, state_lru, ln1, ln2, w_in, w_out, ssm_conv_w, ssm_conv_b, ssm_dt_bias, ssm_a_log,
           ssm_d, ssm_norm, hgrn_lb_logits, hgrn_norm, fox_q_norm, fox_k_norm, fox_f_bias, lru_conv_w,
           lru_conv_b, lru_w_r, lru_b_r, lru_w_i, lru_b_i, lru_lambda, w_gate, w_up, w_down):
    depth = ln1.shape[0]
    bp, tp, _ = x_prompt.shape
    bs, ts, _ = x_sample.shape
    fresh = (jnp.zeros((bp, SSM_CONV - 1, SSM_CONV_DIM), F32),
             jnp.zeros((bp, SSM_HEADS, SSM_STATE, SSM_HEAD_DIM), F32),
             jnp.zeros((bp, HGRN_HEADS, HGRN_HEAD_DIM, HGRN_HEAD_DIM), F32),
             None, None, None,
             jnp.zeros((bp, LRU_CONV - 1, GROUP_W), F32),
             jnp.zeros((bp, GROUP_W), F32))
    yp = x_prompt.reshape(bp * tp, D_MODEL)
    ys = x_sample.reshape(bs * ts, D_MODEL)
    p_states, s_states = [], []
    dense = _prep_dense_weights(w_in, w_out, w_gate, w_up, w_down)
    for l in range(depth):
        pr = _prep_layer_params(l, ln1, ln2, ssm_conv_w, ssm_conv_b, ssm_dt_bias, ssm_a_log, ssm_d,
                                ssm_norm, hgrn_lb_logits, hgrn_norm, fox_q_norm, fox_k_norm, fox_f_bias,
                                lru_conv_w, lru_conv_b, lru_w_r, lru_b_r, lru_w_i, lru_b_i, lru_lambda)
        pr.update(dense)
        yp, st_p = _layer(yp, bp, tp, fresh, pr, l)
        p_states.append(st_p)
        past = (state_ssm_conv[l], state_ssm[l], state_hgrn[l], cache_fox_k, cache_fox_v,
                cache_fox_logf[l], state_lru_conv[l], state_lru[l])
        ys, st_s = _layer(ys, bs, ts, past, pr, l)
        s_states.append(st_s)
    stack = lambda states: [jnp.stack(parts, axis=0) for parts in zip(*states)]
    return (yp.reshape(bp, tp, D_MODEL), ys.reshape(bs, ts, D_MODEL), *stack(p_states), *stack(s_states))
```

```python
import functools
import math

import jax
import jax.numpy as jnp
import numpy as np
from jax import lax
from jax.experimental import pallas as pl
from jax.experimental.pallas import tpu as pltpu

F32 = jnp.float32
BF16 = jnp.bfloat16

D_MODEL = 2048
GROUP_W = 512
CHUNK = 64
SUB = 16
SSM_HEADS = 8
SSM_HEAD_DIM = 64
SSM_GROUPS = 2
SSM_STATE = 128
SSM_CONV = 4
SSM_CONV_DIM = 1024
HGRN_HEADS = 4
HGRN_HEAD_DIM = 128
HGRN_F_FLOOR = 1e-30
FOX_HEADS = 8
FOX_HEAD_DIM = 64
LRU_BLOCKS = 4
LRU_BLOCK_W = 128
LRU_CONV = 4
LRU_C = 8.0
EPS = 1e-6
NEG = -1e30

LANES = 128
CONV_PAD = 8
VMEM_LIMIT = 56 * 1024 * 1024

COL_A_Z, COL_A_X, COL_A_BC, COL_B_Q, COL_B_F, COL_B_I, COL_B_G = 0, 1, 2, 3, 4, 5, 6
COL_C_Q, COL_C_K, COL_C_V, COL_C_G, COL_D_X, COL_D_G = 7, 8, 9, 10, 11, 12
N_MAIN_BLOCKS = 13


def _cparams(sem):
    return pltpu.CompilerParams(dimension_semantics=sem, vmem_limit_bytes=VMEM_LIMIT)


def _dot(a, b):
    return jnp.dot(a, b, preferred_element_type=F32)


def _dot_nt(a, b):
    return lax.dot_general(a, b, (((1,), (1,)), ((), ())), preferred_element_type=F32)


def _dot_tn(a, b):
    return lax.dot_general(a, b, (((0,), (0,)), ((), ())), preferred_element_type=F32)


def _split3(x):
    hi = x.astype(BF16)
    r = x - hi.astype(F32)
    mid = r.astype(BF16)
    lo = (r - mid.astype(F32)).astype(BF16)
    return hi, mid, lo


def _dot01(m01, x):
    hi, mid, lo = _split3(x)
    return _dot(m01, hi) + _dot(m01, mid) + _dot(m01, lo)


def _dot01_r(x, m01):
    hi, mid, lo = _split3(x)
    return _dot(hi, m01) + _dot(mid, m01) + _dot(lo, m01)


def _tri(n, lower=True):
    r = lax.broadcasted_iota(jnp.int32, (n, n), 0)
    c = lax.broadcasted_iota(jnp.int32, (n, n), 1)
    return (r >= c) if lower else (r <= c)


def _sigmoid(x):
    return jax.nn.sigmoid(x)


def _silu(x):
    return x * jax.nn.sigmoid(x)


def _softplus(x):
    return jnp.maximum(x, 0.0) + jnp.log1p(jnp.exp(-jnp.abs(x)))


def _log_sigmoid(x):
    return -_softplus(-x)


def _in_proj_kernel(x_ref, g_ref, w_ref, ws_ref, o_ref, os_ref, h_scr):
    @pl.when(pl.program_id(1) == 0)
    def _():
        x = x_ref[...]
        ms = jnp.mean(x * x, axis=-1, keepdims=True)
        h = (x * lax.rsqrt(ms + EPS) * g_ref[...]).astype(BF16)
        h_scr[...] = h
        os_ref[...] = _dot(h, ws_ref[...])

    o_ref[...] = _dot(h_scr[...], w_ref[...])


def _in_proj(x2d, ln, w_main, w_small, layer, tm):
    m = x2d.shape[0]
    n_main, n_small = w_main.shape[2], w_small.shape[2]
    return pl.pallas_call(
        _in_proj_kernel,
        grid=(m // tm, n_main // GROUP_W),
        in_specs=[
            pl.BlockSpec((tm, D_MODEL), lambda i, j: (i, 0)),
            pl.BlockSpec((1, D_MODEL), lambda i, j: (0, 0)),
            pl.BlockSpec((None, D_MODEL, GROUP_W), lambda i, j: (layer, 0, j)),
            pl.BlockSpec((None, D_MODEL, n_small), lambda i, j: (layer, 0, 0)),
        ],
        out_specs=[
            pl.BlockSpec((tm, GROUP_W), lambda i, j: (i, j)),
            pl.BlockSpec((tm, n_small), lambda i, j: (i, 0)),
        ],
        out_shape=[jax.ShapeDtypeStruct((m, n_main), F32), jax.ShapeDtypeStruct((m, n_small), F32)],
        scratch_shapes=[pltpu.VMEM((tm, D_MODEL), BF16)],
        compiler_params=_cparams(("parallel", "arbitrary")),
        name="in_proj",
    )(x2d, ln, w_main, w_small)


def _conv_block(xp_scr, cw_ref, cb_ref, tb, kw):
    first = CONV_PAD - (kw - 1)
    y = cb_ref[...]
    for j in range(kw):
        y = y + cw_ref[j:j + 1, :] * xp_scr[pl.ds(first + j, tb), :]
    tail = xp_scr[pl.ds(first + tb, kw - 1), :]
    xp_scr[pl.ds(first, kw - 1), :] = tail
    return y, tail


def _ssd_kernel(z_ref, xlo_ref, xhi_ref, sm_ref, conv0_ref, s0_ref, cw_ref, cb_ref, dtb_ref, alog_ref,
                dexp_ref, nrm_ref, y_ref, convo_ref, so_ref,
                xp_scr, xbc_scr, g_scr, dt_scr, y_scr, s_scr, *, tb):
    c = pl.program_id(1)
    last = pl.num_programs(1) - 1
    hpg = SSM_HEADS // SSM_GROUPS
    gw = hpg * SSM_HEAD_DIM

    @pl.when(c == 0)
    def _():
        s_scr[...] = s0_ref[...]
        xp_scr[0:CONV_PAD, :] = jnp.zeros((CONV_PAD, SSM_CONV_DIM), F32)
        xp_scr[CONV_PAD - (SSM_CONV - 1):CONV_PAD, :] = conv0_ref[...]

    xp_scr[CONV_PAD:CONV_PAD + tb, 0:GROUP_W] = xlo_ref[...]
    xp_scr[CONV_PAD:CONV_PAD + tb, GROUP_W:SSM_CONV_DIM] = xhi_ref[...]
    conv, tail = _conv_block(xp_scr, cw_ref, cb_ref, tb, SSM_CONV)

    @pl.when(c == last)
    def _():
        convo_ref[...] = tail

    xbc_scr[...] = _silu(conv)
    dt = _softplus(sm_ref[...] + dtb_ref[...])
    dt_scr[...] = dt
    g_scr[...] = dt * (-jnp.exp(alog_ref[...]))

    tri_mask = _tri(CHUNK)
    tri_l = tri_mask.astype(BF16)
    tri_u = _tri(CHUNK, lower=False).astype(BF16)
    b_off = GROUP_W
    c_off = GROUP_W + SSM_GROUPS * SSM_STATE

    def chunk(ci, carry):
        rows = pl.ds(pl.multiple_of(ci * CHUNK, CHUNK), CHUNK)
        gc = g_scr[rows, :]
        dtc = dt_scr[rows, :]
        cum = _dot01(tri_l, gc)
        cum_t = _dot01_r(gc.T, tri_u)
        tot = cum[CHUNK - 1:CHUNK, :]
        ys = []
        for grp in range(SSM_GROUPS):
            bm = xbc_scr[rows, b_off + grp * SSM_STATE:b_off + (grp + 1) * SSM_STATE].astype(BF16)
            cm = xbc_scr[rows, c_off + grp * SSM_STATE:c_off + (grp + 1) * SSM_STATE].astype(BF16)
            gmat = _dot_nt(cm, bm)
            s_g = s_scr[grp]
            cs = _dot(cm, s_g.astype(BF16))
            vdec, etot = [], []
            for hh in range(hpg):
                h = grp * hpg + hh
                col = cum[:, h:h + 1]
                row = cum_t[h:h + 1, :]
                dec = jnp.where(tri_mask, jnp.exp(jnp.minimum(col - row, 0.0)), 0.0)
                att = (gmat * dec).astype(BF16)
                xh = xbc_scr[rows, h * SSM_HEAD_DIM:(h + 1) * SSM_HEAD_DIM]
                vh = xh * dtc[:, h:h + 1]
                o = _dot(att, vh.astype(BF16)) + jnp.exp(col) * cs[:, hh * SSM_HEAD_DIM:(hh + 1) * SSM_HEAD_DIM]
                ys.append(o + dexp_ref[:, h * SSM_HEAD_DIM:(h + 1) * SSM_HEAD_DIM] * xh)
                toth = tot[:, h:h + 1]
                vdec.append(vh * jnp.exp(toth - col))
                etot.append(jnp.broadcast_to(jnp.exp(toth), (1, SSM_HEAD_DIM)))
            vdec = jnp.concatenate(vdec, axis=1).astype(BF16)
            etot = jnp.concatenate(etot, axis=1)
            s_scr[grp] = etot * s_g + _dot_tn(bm, vdec)
        y_scr[rows, :] = jnp.concatenate(ys, axis=1)
        return carry

    lax.fori_loop(0, tb // CHUNK, chunk, 0)

    y = y_scr[...] * _silu(z_ref[...])
    outs = []
    for grp in range(SSM_GROUPS):
        yg = y[:, grp * gw:(grp + 1) * gw]
        ms = jnp.mean(yg * yg, axis=-1, keepdims=True)
        outs.append(yg * lax.rsqrt(ms + EPS))
    y_ref[...] = (jnp.concatenate(outs, axis=1) * nrm_ref[...]).astype(y_ref.dtype)

    @pl.when(c == last)
    def _():
        so_ref[...] = s_scr[...]


def _ssd(proj, small, conv0, s0, cw, cb, dtb, alog, dexp, nrm, nb, t, tb):
    nc = t // tb
    hpg = SSM_HEADS // SSM_GROUPS
    sshape = (SSM_GROUPS, SSM_STATE, hpg * SSM_HEAD_DIM)
    row = lambda col: pl.BlockSpec((tb, GROUP_W), lambda b, c: (b * nc + c, col))
    full = lambda a: pl.BlockSpec(a.shape, lambda b, c: (0,) * a.ndim)
    return pl.pallas_call(
        functools.partial(_ssd_kernel, tb=tb),
        grid=(nb, nc),
        in_specs=[row(COL_A_Z), row(COL_A_X), row(COL_A_BC),
                  pl.BlockSpec((tb, LANES), lambda b, c: (b * nc + c, 0)),
                  pl.BlockSpec((None, SSM_CONV - 1, SSM_CONV_DIM), lambda b, c: (b, 0, 0)),
                  pl.BlockSpec((None,) + sshape, lambda b, c: (b, 0, 0, 0)),
                  full(cw), full(cb), full(dtb), full(alog), full(dexp), full(nrm)],
        out_specs=[pl.BlockSpec((tb, GROUP_W), lambda b, c: (b * nc + c, 0)),
                   pl.BlockSpec((None, SSM_CONV - 1, SSM_CONV_DIM), lambda b, c: (b, 0, 0)),
                   pl.BlockSpec((None,) + sshape, lambda b, c: (b, 0, 0, 0))],
        out_shape=[jax.ShapeDtypeStruct((nb * t, GROUP_W), BF16),
                   jax.ShapeDtypeStruct((nb, SSM_CONV - 1, SSM_CONV_DIM), F32),
                   jax.ShapeDtypeStruct((nb,) + sshape, F32)],
        scratch_shapes=[pltpu.VMEM((CONV_PAD + tb, SSM_CONV_DIM), F32),
                        pltpu.VMEM((tb, SSM_CONV_DIM), F32),
                        pltpu.VMEM((tb, LANES), F32),
                        pltpu.VMEM((tb, LANES), F32),
                        pltpu.VMEM((tb, GROUP_W), F32),
                        pltpu.VMEM(sshape, F32)],
        compiler_params=_cparams(("parallel", "arbitrary")),
        name="ssd",
    )(proj, proj, proj, small, conv0, s0, cw, cb, dtb, alog, dexp, nrm)


def _hgrn_kernel(q_ref, f_ref, i_ref, g_ref, s0_ref, lbl_ref, nrm_ref, y_ref, so_ref,
                 c_scr, k_scr, st_scr, *, tb, layer):
    c = pl.program_id(1)
    last = pl.num_programs(1) - 1
    hd = HGRN_HEAD_DIM
    nsub = CHUNK // SUB

    @pl.when(c == 0)
    def _():
        for h in range(HGRN_HEADS):
            st_scr[h] = s0_ref[h].T

    ll = lbl_ref[...]
    e = jnp.exp(ll - jnp.max(ll, axis=0, keepdims=True))
    p = e / jnp.sum(e, axis=0, keepdims=True)
    cs = p[0:1, :]
    for l in range(1, layer + 1):
        cs = cs + p[l:l + 1, :]
    lb = cs - p[0:1, :]

    tri_l = _tri(CHUNK).astype(BF16)
    rowid = lax.broadcasted_iota(jnp.int32, (CHUNK, 1), 0)
    subid = lax.broadcasted_iota(jnp.int32, (SUB, 1), 0)

    def chunk(ci, carry):
        r0 = pl.multiple_of(ci * CHUNK, CHUNK)
        rows = pl.ds(r0, CHUNK)
        fr = f_ref[rows, :]
        fg = lb + (1.0 - lb) * _sigmoid(fr)
        logf = jnp.log(jnp.maximum(fg, HGRN_F_FLOOR))
        hk = (1.0 - lb) * _sigmoid(-fr)
        hq = _silu(q_ref[rows, :])
        v = i_ref[rows, :]
        vb = v.astype(BF16)
        cum = _dot01(tri_l, logf)
        tot = cum[CHUNK - 1:CHUNK, :]
        c_scr[...] = cum
        k_scr[...] = hk
        qe = (hq * jnp.exp(cum)).astype(BF16)
        kd = (hk * jnp.exp(tot - cum)).astype(BF16)
        etot = jnp.exp(tot)

        qcat, kcat = [], []
        for j in range(nsub - 1):
            ej = cum[(j + 1) * SUB - 1:(j + 1) * SUB, :]
            qcat.append(jnp.where(rowid >= (j + 1) * SUB, hq * jnp.exp(jnp.minimum(cum - ej, 0.0)), 0.0).astype(BF16))
            in_j = (rowid >= j * SUB) & (rowid < (j + 1) * SUB)
            kcat.append(jnp.where(in_j, hk * jnp.exp(jnp.minimum(ej - cum, 0.0)), 0.0).astype(BF16))

        o_heads = []
        for h in range(HGRN_HEADS):
            hs = slice(h * hd, (h + 1) * hd)
            qc = jnp.concatenate([q[:, hs] for q in qcat], axis=1)
            kc = jnp.concatenate([k[:, hs] for k in kcat], axis=1)
            a_off = _dot_nt(qc, kc)
            st = st_scr[h]
            o_heads.append(_dot(a_off.astype(BF16), vb[:, hs]) + _dot_nt(qe[:, hs], st.astype(BF16)))
            st_scr[h] = st * etot[:, hs] + _dot_tn(vb[:, hs], kd[:, hs])
        o = jnp.concatenate(o_heads, axis=1)

        o_sub = []
        for i in range(nsub):
            base = i * SUB
            ci_ = cum[base:base + SUB, :]
            qi = hq[base:base + SUB, :]
            acc = [jnp.zeros((SUB, hd), F32) for _ in range(HGRN_HEADS)]
            for j in range(SUB):
                crow = c_scr[pl.ds(base + j, 1), :]
                krow = k_scr[pl.ds(base + j, 1), :]
                vrow = i_ref[pl.ds(r0 + base + j, 1), :]
                x = jnp.where(subid >= j, qi * jnp.exp(jnp.minimum(ci_ - crow, 0.0)) * krow, 0.0)
                for h in range(HGRN_HEADS):
                    hs = slice(h * hd, (h + 1) * hd)
                    a = jnp.sum(x[:, hs], axis=-1, keepdims=True)
                    acc[h] = acc[h] + a * vrow[:, hs]
            o_sub.append(jnp.concatenate(acc, axis=1))
        o = o + jnp.concatenate(o_sub, axis=0)

        outs = []
        for h in range(HGRN_HEADS):
            oh = o[:, h * hd:(h + 1) * hd]
            ms = jnp.mean(oh * oh, axis=-1, keepdims=True)
            outs.append(oh * lax.rsqrt(ms + EPS))
        y = jnp.concatenate(outs, axis=1) * nrm_ref[...] * _silu(g_ref[rows, :])
        y_ref[rows, :] = y.astype(y_ref.dtype)
        return carry

    lax.fori_loop(0, tb // CHUNK, chunk, 0)

    @pl.when(c == last)
    def _():
        for h in range(HGRN_HEADS):
            so_ref[h] = st_scr[h].T


def _hgrn(proj, s0, lbl, nrm, nb, t, tb, layer):
    nc = t // tb
    sshape = (HGRN_HEADS, HGRN_HEAD_DIM, HGRN_HEAD_DIM)
    row = lambda col: pl.BlockSpec((tb, GROUP_W), lambda b, c: (b * nc + c, col))
    full = lambda a: pl.BlockSpec(a.shape, lambda b, c: (0,) * a.ndim)
    return pl.pallas_call(
        functools.partial(_hgrn_kernel, tb=tb, layer=layer),
        grid=(nb, nc),
        in_specs=[row(COL_B_Q), row(COL_B_F), row(COL_B_I), row(COL_B_G),
                  pl.BlockSpec((None,) + sshape, lambda b, c: (b, 0, 0, 0)),
                  full(lbl), full(nrm)],
        out_specs=[pl.BlockSpec((tb, GROUP_W), lambda b, c: (b * nc + c, 0)),
                   pl.BlockSpec((None,) + sshape, lambda b, c: (b, 0, 0, 0))],
        out_shape=[jax.ShapeDtypeStruct((nb * t, GROUP_W), BF16),
                   jax.ShapeDtypeStruct((nb,) + sshape, F32)],
        scratch_shapes=[pltpu.VMEM((CHUNK, GROUP_W), F32),
                        pltpu.VMEM((CHUNK, GROUP_W), F32),
                        pltpu.VMEM(sshape, F32)],
        compiler_params=_cparams(("parallel", "arbitrary")),
        name="hgrn2",
    )(proj, proj, proj, proj, s0, lbl, nrm)


def _lru_kernel(x_ref, g_ref, conv0_ref, h0_ref, cw_ref, cb_ref, wr_ref, br_ref, wi_ref, bi_ref, lam_ref,
                y_ref, convo_ref, ho_ref, xp_scr, h_scr, *, tb, reset_first):
    c = pl.program_id(1)
    last = pl.num_programs(1) - 1

    @pl.when(c == 0)
    def _():
        h_scr[...] = h0_ref[...]
        xp_scr[0:CONV_PAD, :] = jnp.zeros((CONV_PAD, GROUP_W), F32)
        xp_scr[CONV_PAD - (LRU_CONV - 1):CONV_PAD, :] = conv0_ref[...]

    xp_scr[CONV_PAD:CONV_PAD + tb, :] = x_ref[...]
    xc, tail = _conv_block(xp_scr, cw_ref, cb_ref, tb, LRU_CONV)

    @pl.when(c == last)
    def _():
        convo_ref[...] = tail

    xcb = xc.astype(BF16)

    def blockdiag(w_ref):
        return jnp.concatenate(
            [_dot(xcb[:, n * LRU_BLOCK_W:(n + 1) * LRU_BLOCK_W], w_ref[n]) for n in range(LRU_BLOCKS)], axis=1)

    r = _sigmoid(blockdiag(wr_ref) + br_ref[...])
    gi = _sigmoid(blockdiag(wi_ref) + bi_ref[...])
    log_a = LRU_C * r * _log_sigmoid(lam_ref[...])
    a = jnp.exp(log_a)
    mult = jnp.sqrt(-jnp.tanh(log_a) * (a * a + 1.0))
    rowid = lax.broadcasted_iota(jnp.int32, (tb, 1), 0)
    first_row = rowid == 0
    if reset_first:
        rst = first_row & (c == 0)
        a = jnp.where(rst, 0.0, a)
        mult = jnp.where(rst, 1.0, mult)
    u = mult * gi * xc
    u = u + jnp.where(first_row, a * h_scr[...], 0.0)

    av, bv = a, u
    d = 1
    while d < tb:
        m = rowid >= d
        a_sh = pltpu.roll(av, d, axis=0)
        b_sh = pltpu.roll(bv, d, axis=0)
        bv = jnp.where(m, av * b_sh + bv, bv)
        av = jnp.where(m, av * a_sh, av)
        d *= 2
    h_new = bv[tb - 1:tb, :]
    h_scr[...] = h_new
    y_ref[...] = (bv * jax.nn.gelu(g_ref[...])).astype(y_ref.dtype)

    @pl.when(c == last)
    def _():
        ho_ref[...] = h_new


def _lru(proj, conv0, h0, cw, cb, wr, br, wi, bi, lam, nb, t, tb, reset_first):
    nc = t // tb
    row = lambda col: pl.BlockSpec((tb, GROUP_W), lambda b, c: (b * nc + c, col))
    full = lambda a: pl.BlockSpec(a.shape, lambda b, c: (0,) * a.ndim)
    return pl.pallas_call(
        functools.partial(_lru_kernel, tb=tb, reset_first=reset_first),
        grid=(nb, nc),
        in_specs=[row(COL_D_X), row(COL_D_G),
                  pl.BlockSpec((None, LRU_CONV - 1, GROUP_W), lambda b, c: (b, 0, 0)),
                  pl.BlockSpec((None, 1, GROUP_W), lambda b, c: (b, 0, 0)),
                  full(cw), full(cb), full(wr), full(br), full(wi), full(bi), full(lam)],
        out_specs=[pl.BlockSpec((tb, GROUP_W), lambda b, c: (b * nc + c, 0)),
                   pl.BlockSpec((None, LRU_CONV - 1, GROUP_W), lambda b, c: (b, 0, 0)),
                   pl.BlockSpec((None, 1, GROUP_W), lambda b, c: (b, 0, 0))],
        out_shape=[jax.ShapeDtypeStruct((nb * t, GROUP_W), BF16),
                   jax.ShapeDtypeStruct((nb, LRU_CONV - 1, GROUP_W), F32),
                   jax.ShapeDtypeStruct((nb, 1, GROUP_W), F32)],
        scratch_shapes=[pltpu.VMEM((CONV_PAD + tb, GROUP_W), F32),
                        pltpu.VMEM((1, GROUP_W), F32)],
        compiler_params=_cparams(("parallel", "arbitrary")),
        name="rglru",
    )(proj, proj, conv0, h0, cw, cb, wr, br, wi, bi, lam)


def _gate_kernel(*refs, tb, past):
    if past:
        sm_ref, fb_ref, plf_ref, lf_ref, cum_ref, cump_ref, carry_scr = refs
    else:
        sm_ref, fb_ref, lf_ref, cum_ref, carry_scr = refs

    @pl.when(pl.program_id(1) == 0)
    def _():
        if past:
            cp = _dot01(_tri(past).astype(BF16), plf_ref[...])
            cump_ref[...] = cp
            carry_scr[...] = cp[past - 1:past, :]
        else:
            carry_scr[...] = jnp.zeros_like(carry_scr)

    lf = _log_sigmoid(sm_ref[...] + fb_ref[...])
    lf_ref[...] = lf
    cum = _dot01(_tri(tb).astype(BF16), lf) + carry_scr[...]
    cum_ref[...] = cum
    carry_scr[...] = cum[tb - 1:tb, :]


def _gate(small, fb, past_lf, nb, t, tb):
    nc = t // tb
    past = 0 if past_lf is None else past_lf.shape[1]
    rows = pl.BlockSpec((tb, LANES), lambda b, c: (b * nc + c, 0))
    in_specs = [pl.BlockSpec((tb, LANES), lambda b, c: (b * nc + c, 1)),
                pl.BlockSpec(fb.shape, lambda b, c: (0, 0))]
    out_specs = [rows, rows]
    out_shape = [jax.ShapeDtypeStruct((nb * t, LANES), F32)] * 2
    args = [small, fb]
    if past:
        in_specs.append(pl.BlockSpec((None, past, LANES), lambda b, c: (b, 0, 0)))
        out_specs.append(pl.BlockSpec((None, past, LANES), lambda b, c: (b, 0, 0)))
        out_shape.append(jax.ShapeDtypeStruct((nb, past, LANES), F32))
        args.append(past_lf)
    return pl.pallas_call(
        functools.partial(_gate_kernel, tb=tb, past=past),
        grid=(nb, nc),
        in_specs=in_specs, out_specs=out_specs, out_shape=out_shape,
        scratch_shapes=[pltpu.VMEM((1, LANES), F32)],
        compiler_params=_cparams(("parallel", "arbitrary")),
        name="fox_gate",
    )(*args)


def _head_norm(x, gain, hsum):
    ms = _dot01_r(x * x, hsum) * (1.0 / FOX_HEAD_DIM)
    return x * lax.rsqrt(ms + EPS) * gain


def _store_heads(ref, x):
    tb = x.shape[0]
    for h in range(FOX_HEADS):
        ref[pl.ds(h, tb, stride=FOX_HEADS), :] = x[:, h * FOX_HEAD_DIM:(h + 1) * FOX_HEAD_DIM]


def _foxprep_kernel(kbuf_ref, vbuf_ref, q_ref, k_ref, v_ref, hsum_ref, qn_ref, kn_ref,
                    fq_ref, fk_ref, fv_ref, kb_ref, vb_ref):
    del kbuf_ref, vbuf_ref
    hsum = hsum_ref[...]
    fq_ref[...] = (_head_norm(q_ref[...], qn_ref[...], hsum) * (FOX_HEAD_DIM ** -0.5)).astype(BF16)
    fk = _head_norm(k_ref[...], kn_ref[...], hsum)
    _store_heads(fk_ref, fk)
    kb_ref[...] = fk.astype(BF16)
    v = v_ref[...]
    _store_heads(fv_ref, v)
    vb_ref[...] = v.astype(BF16)


def _heads_spec(tb, first_block):
    return pl.BlockSpec((tb * FOX_HEADS, FOX_HEAD_DIM), lambda i: (first_block + i, 0))


_ANY = pl.BlockSpec(memory_space=pl.ANY)


def _foxprep(kv, layer, proj, hsum, qn, kn, tb):
    m = proj.shape[0]
    row = lambda col: pl.BlockSpec((tb, GROUP_W), lambda i: (i, col))
    full = lambda a: pl.BlockSpec(a.shape, lambda i: (0,) * a.ndim)
    out_row = pl.BlockSpec((tb, GROUP_W), lambda i: (i, 0))
    heads = _heads_spec(tb, layer * (m // tb))
    return pl.pallas_call(
        _foxprep_kernel,
        grid=(m // tb,),
        in_specs=[_ANY, _ANY, row(COL_C_Q), row(COL_C_K), row(COL_C_V), full(hsum), full(qn), full(kn)],
        out_specs=[out_row, heads, heads, out_row, out_row],
        out_shape=[jax.ShapeDtypeStruct((m, GROUP_W), BF16),
                   jax.ShapeDtypeStruct(kv[0].shape, F32),
                   jax.ShapeDtypeStruct(kv[1].shape, F32),
                   jax.ShapeDtypeStruct((m, GROUP_W), BF16),
                   jax.ShapeDtypeStruct((m, GROUP_W), BF16)],
        input_output_aliases={0: 1, 1: 2},
        compiler_params=_cparams(("parallel",)),
        name="fox_prep",
    )(kv[0], kv[1], proj, proj, proj, hsum, qn, kn)


AUG_W = FOX_HEADS * LANES
N_PIECES = 3


def _aug_constants():
    pq = np.zeros((N_PIECES * LANES, AUG_W), np.float32)
    pk = np.zeros((N_PIECES * LANES, AUG_W), np.float32)
    ones_q = np.zeros((1, AUG_W), np.float32)
    ones_k = np.zeros((1, AUG_W), np.float32)
    ones_v = np.zeros((1, AUG_W), np.float32)
    head = np.zeros((1, AUG_W), np.float32)
    for h in range(FOX_HEADS):
        own = h * LANES + FOX_HEAD_DIM * (h % 2)
        other = h * LANES + FOX_HEAD_DIM * (1 - h % 2)
        head[0, own:own + FOX_HEAD_DIM] = 1.0
        for piece in range(N_PIECES):
            pq[piece * LANES + h, other + piece] = 1.0
            ones_k[0, other + piece] = 1.0
            ones_q[0, other + N_PIECES + piece] = 1.0
            pk[piece * LANES + h, other + N_PIECES + piece] = -1.0
        ones_v[0, other] = 1.0
    return dict(aug_pq=jnp.asarray(pq, BF16), aug_pk=jnp.asarray(pk, BF16), aug_ones_q=jnp.asarray(ones_q),
                aug_ones_k=jnp.asarray(ones_k), aug_ones_v=jnp.asarray(ones_v), aug_head=jnp.asarray(head))


def _foxprep_aug_kernel(kbuf_ref, vbuf_ref, q_ref, k_ref, v_ref, cum_ref, hsum_ref, qn_ref, kn_ref, pq_ref, pk_ref,
                        oq_ref, ok_ref, ov_ref, head_ref, fk_ref, fv_ref, qa_ref, ka_ref, va_ref):
    del kbuf_ref, vbuf_ref
    hsum = hsum_ref[...]
    fq = _head_norm(q_ref[...], qn_ref[...], hsum) * (FOX_HEAD_DIM ** -0.5)
    fk = _head_norm(k_ref[...], kn_ref[...], hsum)
    v = v_ref[...]
    _store_heads(fk_ref, fk)
    _store_heads(fv_ref, v)
    pieces = jnp.concatenate(_split3(cum_ref[...]), axis=1)
    own = head_ref[...] > 0.5

    def per_head(x):
        return jnp.concatenate([x[:, (h // 2) * LANES:(h // 2 + 1) * LANES] for h in range(FOX_HEADS)], axis=1)

    qa = jnp.where(own, per_head(fq), _dot(pieces, pq_ref[...]) + oq_ref[...]).astype(BF16)
    ka = jnp.where(own, per_head(fk), _dot(pieces, pk_ref[...]) + ok_ref[...]).astype(BF16)
    va = jnp.where(own, per_head(v), ov_ref[...]).astype(BF16)
    for h in range(FOX_HEADS):
        hs = slice(h * LANES, (h + 1) * LANES)
        qa_ref[h] = qa[:, hs]
        ka_ref[h] = ka[:, hs]
        va_ref[h] = va[:, hs]


def _foxprep_aug(kv, layer, proj, cum, hsum, qn, kn, aug, tb):
    m = proj.shape[0]
    row = lambda col: pl.BlockSpec((tb, GROUP_W), lambda i: (i, col))
    full = lambda a: pl.BlockSpec(a.shape, lambda i: (0,) * a.ndim)
    out_aug = pl.BlockSpec((FOX_HEADS, tb, LANES), lambda i: (0, i, 0))
    heads = _heads_spec(tb, layer * (m // tb))
    consts = [aug['aug_pq'], aug['aug_pk'], aug['aug_ones_q'], aug['aug_ones_k'], aug['aug_ones_v'],
              aug['aug_head']]
    return pl.pallas_call(
        _foxprep_aug_kernel,
        grid=(m // tb,),
        in_specs=[_ANY, _ANY, row(COL_C_Q), row(COL_C_K), row(COL_C_V),
                  pl.BlockSpec((tb, LANES), lambda i: (i, 0)),
                  full(hsum), full(qn), full(kn)] + [full(a) for a in consts],
        out_specs=[heads, heads, out_aug, out_aug, out_aug],
        out_shape=[jax.ShapeDtypeStruct(kv[0].shape, F32), jax.ShapeDtypeStruct(kv[1].shape, F32)]
                  + [jax.ShapeDtypeStruct((FOX_HEADS, m, LANES), BF16)] * 3,
        input_output_aliases={0: 0, 1: 1},
        compiler_params=_cparams(("parallel",)),
        name="fox_prep_aug",
    )(kv[0], kv[1], proj, proj, proj, cum, hsum, qn, kn, *consts)


def _attn_aug_kernel(q_ref, k_ref, v_ref, g_ref, y_ref, *, tq, hg):
    qi = pl.program_id(2)
    causal = _tri(tq)
    low = lax.broadcasted_iota(jnp.int32, (1, LANES), 1) < FOX_HEAD_DIM
    qs = [q_ref[i] for i in range(hg)]

    def step(j, carry, masked):
        rows = pl.ds(pl.multiple_of(j * tq, tq), tq)
        new = []
        for i in range(hg):
            m, acc = carry[i]
            s = _dot_nt(qs[i], k_ref[i, rows, :])
            if masked:
                s = jnp.where(causal, s, NEG)
            m_new = jnp.maximum(m, jnp.max(s, axis=-1, keepdims=True))
            p = jnp.exp(s - m_new).astype(BF16)
            acc = jnp.exp(m - m_new) * acc + _dot(p, v_ref[i, rows, :])
            new.append((m_new, acc))
        return tuple(new)

    init = tuple((jnp.full((tq, 1), NEG, F32), jnp.zeros((tq, LANES), F32)) for _ in range(hg))
    carry = lax.fori_loop(0, qi, lambda j, c: step(j, c, False), init)
    carry = step(qi, carry, True)
    outs = []
    for i in range(0, hg, 2):
        acc_e, acc_o = carry[i][1], carry[i + 1][1]
        out_e = acc_e / acc_e[:, FOX_HEAD_DIM:FOX_HEAD_DIM + 1]
        out_o = acc_o / acc_o[:, 0:1]
        outs.append(jnp.where(low, out_e, out_o))
    y = jnp.concatenate(outs, axis=1) * _sigmoid(g_ref[...])
    y_ref[...] = y.astype(y_ref.dtype)


def _attention_aug(qa, ka, va, proj, nb, t, tq, hg):
    nq = t // tq
    ngrp = FOX_HEADS // hg
    wout = hg * FOX_HEAD_DIM
    gate_col0 = COL_C_G * GROUP_W // wout
    return pl.pallas_call(
        functools.partial(_attn_aug_kernel, tq=tq, hg=hg),
        grid=(nb, ngrp, nq),
        in_specs=[pl.BlockSpec((hg, tq, LANES), lambda b, g, i: (g, b * nq + i, 0)),
                  pl.BlockSpec((hg, t, LANES), lambda b, g, i: (g, b, 0)),
                  pl.BlockSpec((hg, t, LANES), lambda b, g, i: (g, b, 0)),
                  pl.BlockSpec((tq, wout), lambda b, g, i: (b * nq + i, gate_col0 + g))],
        out_specs=pl.BlockSpec((tq, wout), lambda b, g, i: (b * nq + i, g)),
        out_shape=jax.ShapeDtypeStruct((nb * t, GROUP_W), BF16),
        compiler_params=_cparams(("parallel", "parallel", "arbitrary")),
        name="fox_attention_aug",
    )(qa, ka, va, proj)


def _attn_cached_kernel(q_ref, cq_ref, ck_ref, k_ref, v_ref, g_ref, pk_ref, pv_ref, y_ref, *, t, past):
    causal = _tri(t)
    outs = []
    for h in range(FOX_HEADS):
        hs = slice(h * FOX_HEAD_DIM, (h + 1) * FOX_HEAD_DIM)
        qh = q_ref[:, hs]
        pk = pk_ref[pl.ds(h, past, stride=FOX_HEADS), :].astype(BF16)
        pv = pv_ref[pl.ds(h, past, stride=FOX_HEADS), :].astype(BF16)
        cqh = cq_ref[:, h:h + 1]
        s_p = _dot_nt(qh, pk) + cqh - ck_ref[h:h + 1, 0:past]
        s_c = jnp.where(causal, _dot_nt(qh, k_ref[:, hs]) + cqh - ck_ref[h:h + 1, past:past + t], NEG)
        m = jnp.maximum(jnp.max(s_p, axis=-1, keepdims=True), jnp.max(s_c, axis=-1, keepdims=True))
        p_p = jnp.exp(s_p - m)
        p_c = jnp.exp(s_c - m)
        l = jnp.sum(p_p, axis=-1, keepdims=True) + jnp.sum(p_c, axis=-1, keepdims=True)
        acc = _dot(p_p.astype(BF16), pv) + _dot(p_c.astype(BF16), v_ref[:, hs])
        outs.append(acc / l)
    y = jnp.concatenate(outs, axis=1) * _sigmoid(g_ref[...])
    y_ref[...] = y.astype(y_ref.dtype)


def _attention_cached(fq, cq, ck_t, kb, vb, proj, past_k, past_v, layer, nb, t):
    past = past_k.shape[1] // FOX_HEADS
    ltot = ck_t.shape[-1]
    seq = lambda rows: pl.BlockSpec((None, rows, GROUP_W), lambda b: (b, 0, 0))
    cache = pl.BlockSpec((None, past * FOX_HEADS, FOX_HEAD_DIM), lambda b: (layer * nb + b, 0, 0))
    return pl.pallas_call(
        functools.partial(_attn_cached_kernel, t=t, past=past),
        grid=(nb,),
        in_specs=[pl.BlockSpec((t, GROUP_W), lambda b: (b, 0)),
                  pl.BlockSpec((None, t, LANES), lambda b: (b, 0, 0)),
                  pl.BlockSpec((None, FOX_HEADS, ltot), lambda b: (b, 0, 0)),
                  seq(t), seq(t),
                  pl.BlockSpec((t, GROUP_W), lambda b: (b, COL_C_G)),
                  cache, cache],
        out_specs=pl.BlockSpec((t, GROUP_W), lambda b: (b, 0)),
        out_shape=jax.ShapeDtypeStruct((nb * t, GROUP_W), BF16),
        compiler_params=_cparams(("parallel",)),
        name="fox_attention_cached",
    )(fq, cq, ck_t, kb.reshape(nb, t, GROUP_W), vb.reshape(nb, t, GROUP_W), proj, past_k, past_v)


def _out_proj_kernel(x_ref, ya_ref, yb_ref, yc_ref, yd_ref, w_ref, o_ref):
    acc = x_ref[...]
    for n, y_ref in enumerate((ya_ref, yb_ref, yc_ref, yd_ref)):
        acc = acc + _dot(y_ref[...], w_ref[n * GROUP_W:(n + 1) * GROUP_W, :])
    o_ref[...] = acc


def _out_proj(x2d, ys, w_out, layer, tm):
    m = x2d.shape[0]
    yspec = pl.BlockSpec((tm, GROUP_W), lambda i: (i, 0))
    return pl.pallas_call(
        _out_proj_kernel,
        grid=(m // tm,),
        in_specs=[pl.BlockSpec((tm, D_MODEL), lambda i: (i, 0)), yspec, yspec, yspec, yspec,
                  pl.BlockSpec((None,) + w_out.shape[1:], lambda i: (layer, 0, 0))],
        out_specs=pl.BlockSpec((tm, D_MODEL), lambda i: (i, 0)),
        out_shape=jax.ShapeDtypeStruct((m, D_MODEL), F32),
        compiler_params=_cparams(("parallel",)),
        name="out_proj",
    )(x2d, *ys, w_out)


def _ffn_kernel(x_ref, g_ref, wg_ref, wu_ref, wd_ref, o_ref, h_scr, acc_scr):
    f = pl.program_id(1)

    @pl.when(f == 0)
    def _():
        x = x_ref[...]
        ms = jnp.mean(x * x, axis=-1, keepdims=True)
        h_scr[...] = (x * lax.rsqrt(ms + EPS) * g_ref[...]).astype(BF16)
        acc_scr[...] = x

    h = h_scr[...]
    a = _silu(_dot(h, wg_ref[...])) * _dot(h, wu_ref[...])
    acc_scr[...] += _dot(a.astype(BF16), wd_ref[...])

    @pl.when(f == pl.num_programs(1) - 1)
    def _():
        o_ref[...] = acc_scr[...]


def _ffn(x2d, ln, wg, wu, wd, layer, tm, tf):
    m = x2d.shape[0]
    d_ff = wg.shape[2]
    return pl.pallas_call(
        _ffn_kernel,
        grid=(m // tm, d_ff // tf),
        in_specs=[pl.BlockSpec((tm, D_MODEL), lambda i, f: (i, 0)),
                  pl.BlockSpec((1, D_MODEL), lambda i, f: (0, 0)),
                  pl.BlockSpec((None, D_MODEL, tf), lambda i, f: (layer, 0, f)),
                  pl.BlockSpec((None, D_MODEL, tf), lambda i, f: (layer, 0, f)),
                  pl.BlockSpec((None, tf, D_MODEL), lambda i, f: (layer, f, 0))],
        out_specs=pl.BlockSpec((tm, D_MODEL), lambda i, f: (i, 0)),
        out_shape=jax.ShapeDtypeStruct((m, D_MODEL), F32),
        scratch_shapes=[pltpu.VMEM((tm, D_MODEL), BF16), pltpu.VMEM((tm, D_MODEL), F32)],
        compiler_params=_cparams(("parallel", "arbitrary")),
        name="swiglu_ffn",
    )(x2d, ln, wg, wu, wd)


def _pad_lanes(v, offset=0):
    return jnp.zeros((1, LANES), F32).at[0, offset:offset + v.shape[0]].set(v.astype(F32))


def _ssm_state_to_groups(s):
    nb = s.shape[0]
    hpg = SSM_HEADS // SSM_GROUPS
    s = s.reshape(nb, SSM_GROUPS, hpg, SSM_STATE, SSM_HEAD_DIM)
    return jnp.transpose(s, (0, 1, 3, 2, 4)).reshape(nb, SSM_GROUPS, SSM_STATE, hpg * SSM_HEAD_DIM)


def _ssm_state_from_groups(s):
    nb = s.shape[0]
    hpg = SSM_HEADS // SSM_GROUPS
    s = s.reshape(nb, SSM_GROUPS, SSM_STATE, hpg, SSM_HEAD_DIM)
    return jnp.transpose(s, (0, 1, 3, 2, 4)).reshape(nb, SSM_HEADS, SSM_STATE, SSM_HEAD_DIM)


def _block_rows(t, want):
    return want if t % want == 0 else t


def _layer(x2d, nb, t, state, pr, layer, kv):
    (ssm_conv, ssm_s, hgrn_s, fox_k, fox_v, fox_lf, lru_conv, lru_h) = state
    m = nb * t
    past = 0 if fox_k is None else fox_k.shape[2]
    tm = _block_rows(m, 512)
    tb = _block_rows(t, 256)

    proj, small = _in_proj(x2d, pr['ln1'], pr['w_main'], pr['w_small'], layer, _block_rows(m, 1024))

    ya, ssm_conv_new, ssm_g = _ssd(proj, small, ssm_conv, _ssm_state_to_groups(ssm_s), pr['ssm_conv_w'],
                                   pr['ssm_conv_b'], pr['ssm_dt_bias'], pr['ssm_a_log'], pr['ssm_d'],
                                   pr['ssm_norm'], nb, t, tb)
    ssm_s_new = _ssm_state_from_groups(ssm_g)

    yb, hgrn_s_new = _hgrn(proj, hgrn_s, pr['hgrn_lb_logits'], pr['hgrn_norm'], nb, t, tb, layer)

    yd, lru_conv_new, lru_h_new = _lru(proj, lru_conv, lru_h.reshape(nb, 1, GROUP_W), pr['lru_conv_w'],
                                       pr['lru_conv_b'], pr['lru_w_r'], pr['lru_b_r'], pr['lru_w_i'],
                                       pr['lru_b_i'], pr['lru_lambda'], nb, t, tb, reset_first=(past == 0))

    if past:
        past_lf = jnp.pad(fox_lf.astype(F32), ((0, 0), (0, 0), (0, LANES - FOX_HEADS)))
        lf, cum, cum_past = _gate(small, pr['fox_f_bias'], past_lf, nb, t, t)
        fq, fk, fv, kb, vb = _foxprep(kv, layer, proj, pr['head_sum'], pr['fox_q_norm'], pr['fox_k_norm'], tm)
        cq = cum.reshape(nb, t, LANES)
        ck_t = jnp.transpose(jnp.concatenate([cum_past, cq], axis=1)[:, :, :FOX_HEADS], (0, 2, 1))
        rows = lambda c: c.reshape(c.shape[0] * nb, past * FOX_HEADS, FOX_HEAD_DIM)
        yc = _attention_cached(fq, cq, ck_t, kb, vb, proj, rows(fox_k), rows(fox_v), layer, nb, t)
    else:
        lf, cum = _gate(small, pr['fox_f_bias'], None, nb, t, _block_rows(t, 512))
        fk, fv, qa, ka, va = _foxprep_aug(kv, layer, proj, cum, pr['head_sum'], pr['fox_q_norm'],
                                          pr['fox_k_norm'], pr['aug'], tm)
        yc = _attention_aug(qa, ka, va, proj, nb, t, _block_rows(t, 512), FOX_HEADS)
    lf = lf.reshape(nb, t, LANES)

    x1 = _out_proj(x2d, (ya, yb, yc, yd), pr['w_out'], layer, tm)
    x2 = _ffn(x1, pr['ln2'], pr['w_gate'], pr['w_up'], pr['w_down'], layer, tm, 512)

    new_state = (ssm_conv_new, ssm_s_new, hgrn_s_new, lf[:, :, :FOX_HEADS], lru_conv_new,
                 lru_h_new.reshape(nb, GROUP_W))
    return x2, new_state, (fk, fv)


def _prep_layer_params(l, ln1, ln2, ssm_conv_w, ssm_conv_b, ssm_dt_bias, ssm_a_log, ssm_d,
                       ssm_norm, hgrn_lb_logits, hgrn_norm, fox_q_norm, fox_k_norm, fox_f_bias, lru_conv_w,
                       lru_conv_b, lru_w_r, lru_b_r, lru_w_i, lru_b_i, lru_lambda):
    hid = jnp.arange(GROUP_W) // FOX_HEAD_DIM
    row = lambda v: v.astype(F32).reshape(1, -1)
    return dict(
        ln1=row(ln1[l]), ln2=row(ln2[l]),
        ssm_conv_w=ssm_conv_w[l], ssm_conv_b=row(ssm_conv_b[l]),
        ssm_dt_bias=_pad_lanes(ssm_dt_bias[l]), ssm_a_log=_pad_lanes(ssm_a_log[l]),
        ssm_d=row(jnp.repeat(ssm_d[l], SSM_HEAD_DIM)), ssm_norm=row(ssm_norm[l]),
        hgrn_lb_logits=hgrn_lb_logits.astype(F32), hgrn_norm=row(hgrn_norm[l]),
        head_sum=(hid[:, None] == hid[None, :]).astype(BF16), aug=_aug_constants(),
        fox_q_norm=row(jnp.tile(fox_q_norm[l], FOX_HEADS)), fox_k_norm=row(jnp.tile(fox_k_norm[l], FOX_HEADS)),
        fox_f_bias=_pad_lanes(fox_f_bias[l]),
        lru_conv_w=lru_conv_w[l], lru_conv_b=row(lru_conv_b[l]),
        lru_w_r=lru_w_r[l].astype(BF16), lru_b_r=row(lru_b_r[l]),
        lru_w_i=lru_w_i[l].astype(BF16), lru_b_i=row(lru_b_i[l]), lru_lambda=row(lru_lambda[l]))


def _prep_dense_weights(w_in, w_out, w_gate, w_up, w_down):
    depth = w_in.shape[0]
    o_dt = GROUP_W + SSM_CONV_DIM
    o_cf = o_dt + SSM_HEADS + 7 * GROUP_W
    w_main = jnp.concatenate([w_in[:, :, :o_dt], w_in[:, :, o_dt + SSM_HEADS:o_cf], w_in[:, :, o_cf + FOX_HEADS:]],
                             axis=2)
    w_small = jnp.zeros((depth, D_MODEL, 2 * LANES), F32)
    w_small = w_small.at[:, :, :SSM_HEADS].set(w_in[:, :, o_dt:o_dt + SSM_HEADS])
    w_small = w_small.at[:, :, LANES:LANES + FOX_HEADS].set(w_in[:, :, o_cf:o_cf + FOX_HEADS])
    return dict(w_main=w_main.astype(BF16), w_small=w_small.astype(BF16), w_out=w_out.astype(BF16),
                w_gate=w_gate.astype(BF16), w_up=w_up.astype(BF16), w_down=w_down.astype(BF16))


def kernel(x_prompt, x_sample, cache_fox_k, cache_fox_v, cache_fox_logf, state_ssm_conv, state_ssm, state_hgrn,
           state_lru_conv, state_lru, ln1, ln2, w_in, w_out, ssm_conv_w, ssm_conv_b, ssm_dt_bias, ssm_a_log,
           ssm_d, ssm_norm, hgrn_lb_logits, hgrn_norm, fox_q_norm, fox_k_norm, fox_f_bias, lru_conv_w,
           lru_conv_b, lru_w_r, lru_b_r, lru_w_i, lru_b_i, lru_lambda, w_gate, w_up, w_down):
    depth = ln1.shape[0]
    bp, tp, _ = x_prompt.shape
    bs, ts, _ = x_sample.shape
    fresh = (jnp.zeros((bp, SSM_CONV - 1, SSM_CONV_DIM), F32),
             jnp.zeros((bp, SSM_HEADS, SSM_STATE, SSM_HEAD_DIM), F32),
             jnp.zeros((bp, HGRN_HEADS, HGRN_HEAD_DIM, HGRN_HEAD_DIM), F32),
             None, None, None,
             jnp.zeros((bp, LRU_CONV - 1, GROUP_W), F32),
             jnp.zeros((bp, GROUP_W), F32))
    yp = x_prompt.reshape(bp * tp, D_MODEL)
    ys = x_sample.reshape(bs * ts, D_MODEL)
    p_states, s_states = [], []
    new_kv = lambda rows: tuple(jnp.zeros((depth * rows * FOX_HEADS, FOX_HEAD_DIM), F32) for _ in range(2))
    kv_p, kv_s = new_kv(bp * tp), new_kv(bs * ts)
    dense =_prep_dense_weights(w_in, w_out, w_gate, w_up, w_down)
    for l in range(depth):
        pr = _prep_layer_params(l, ln1, ln2, ssm_conv_w, ssm_conv_b, ssm_dt_bias, ssm_a_log, ssm_d,
                                ssm_norm, hgrn_lb_logits, hgrn_norm, fox_q_norm, fox_k_norm, fox_f_bias,
                                lru_conv_w, lru_conv_b, lru_w_r, lru_b_r, lru_w_i, lru_b_i, lru_lambda)
        pr.update(dense)
        yp, st_p, kv_p = _layer(yp, bp, tp, fresh, pr, l, kv_p)
        p_states.append(st_p)
        past = (state_ssm_conv[l], state_ssm[l], state_hgrn[l], cache_fox_k, cache_fox_v,
                cache_fox_logf[l], state_lru_conv[l], state_lru[l])
        ys, st_s, kv_s = _layer(ys, bs, ts, past, pr, l, kv_s)
        s_states.append(st_s)

    def group(states, kv, nb, t):
        conv, ssm, hgrn, lf, lru_conv, lru = [jnp.stack(parts, axis=0) for parts in zip(*states)]
        k, v = [a.reshape(depth, nb, t, FOX_HEADS, FOX_HEAD_DIM) for a in kv]
        return conv, ssm, hgrn, k, v, lf, lru_conv, lru

    return (yp.reshape(bp, tp, D_MODEL), ys.reshape(bs, ts, D_MODEL),
            *group(p_states, kv_p, bp, tp), *group(s_states, kv_s, bs, ts))
```

```python
import functools
import math

import jax
import jax.numpy as jnp
import numpy as np
from jax import lax
from jax.experimental import pallas as pl
from jax.experimental.pallas import tpu as pltpu

F32 = jnp.float32
BF16 = jnp.bfloat16

D_MODEL = 2048
GROUP_W = 512
CHUNK = 64
SUB = 16
SSM_HEADS = 8
SSM_HEAD_DIM = 64
SSM_GROUPS = 2
SSM_STATE = 128
SSM_CONV = 4
SSM_CONV_DIM = 1024
HGRN_HEADS = 4
HGRN_HEAD_DIM = 128
HGRN_F_FLOOR = 1e-30
FOX_HEADS = 8
FOX_HEAD_DIM = 64
LRU_BLOCKS = 4
LRU_BLOCK_W = 128
LRU_CONV = 4
LRU_C = 8.0
EPS = 1e-6
NEG = -1e30

LANES = 128
CONV_PAD = 8
VMEM_LIMIT = 56 * 1024 * 1024

COL_A_Z, COL_A_X, COL_A_BC, COL_B_Q, COL_B_F, COL_B_I, COL_B_G = 0, 1, 2, 3, 4, 5, 6
COL_C_Q, COL_C_K, COL_C_V, COL_C_G, COL_D_X, COL_D_G = 7, 8, 9, 10, 11, 12
N_MAIN_BLOCKS = 13


def _cparams(sem):
    return pltpu.CompilerParams(dimension_semantics=sem, vmem_limit_bytes=VMEM_LIMIT)


def _dot(a, b):
    return jnp.dot(a, b, preferred_element_type=F32)


def _dot_nt(a, b):
    return lax.dot_general(a, b, (((1,), (1,)), ((), ())), preferred_element_type=F32)


def _dot_tn(a, b):
    return lax.dot_general(a, b, (((0,), (0,)), ((), ())), preferred_element_type=F32)


def _split3(x):
    hi = x.astype(BF16)
    r = x - hi.astype(F32)
    mid = r.astype(BF16)
    lo = (r - mid.astype(F32)).astype(BF16)
    return hi, mid, lo


def _dot01(m01, x):
    hi, mid, lo = _split3(x)
    return _dot(m01, hi) + _dot(m01, mid) + _dot(m01, lo)


def _dot01_r(x, m01):
    hi, mid, lo = _split3(x)
    return _dot(hi, m01) + _dot(mid, m01) + _dot(lo, m01)


def _tri(n, lower=True):
    r = lax.broadcasted_iota(jnp.int32, (n, n), 0)
    c = lax.broadcasted_iota(jnp.int32, (n, n), 1)
    return (r >= c) if lower else (r <= c)


def _sigmoid(x):
    return jax.nn.sigmoid(x)


def _silu(x):
    return x * jax.nn.sigmoid(x)


def _softplus(x):
    return jnp.maximum(x, 0.0) + jnp.log1p(jnp.exp(-jnp.abs(x)))


def _log_sigmoid(x):
    return -_softplus(-x)


def _in_proj_kernel(x_ref, g_ref, w_ref, ws_ref, o_ref, os_ref, h_scr):
    @pl.when(pl.program_id(1) == 0)
    def _():
        x = x_ref[...]
        ms = jnp.mean(x * x, axis=-1, keepdims=True)
        h = (x * lax.rsqrt(ms + EPS) * g_ref[...]).astype(BF16)
        h_scr[...] = h
        os_ref[...] = _dot(h, ws_ref[...])

    o_ref[...] = _dot(h_scr[...], w_ref[...])


def _in_proj(x2d, ln, w_main, w_small, layer, tm):
    m = x2d.shape[0]
    n_main, n_small = w_main.shape[2], w_small.shape[2]
    return pl.pallas_call(
        _in_proj_kernel,
        grid=(m // tm, n_main // GROUP_W),
        in_specs=[
            pl.BlockSpec((tm, D_MODEL), lambda i, j: (i, 0)),
            pl.BlockSpec((1, D_MODEL), lambda i, j: (0, 0)),
            pl.BlockSpec((None, D_MODEL, GROUP_W), lambda i, j: (layer, 0, j)),
            pl.BlockSpec((None, D_MODEL, n_small), lambda i, j: (layer, 0, 0)),
        ],
        out_specs=[
            pl.BlockSpec((tm, GROUP_W), lambda i, j: (i, j)),
            pl.BlockSpec((tm, n_small), lambda i, j: (i, 0)),
        ],
        out_shape=[jax.ShapeDtypeStruct((m, n_main), F32), jax.ShapeDtypeStruct((m, n_small), F32)],
        scratch_shapes=[pltpu.VMEM((tm, D_MODEL), BF16)],
        compiler_params=_cparams(("parallel", "arbitrary")),
        name="in_proj",
    )(x2d, ln, w_main, w_small)


def _conv_block(xp_scr, cw_ref, cb_ref, tb, kw):
    first = CONV_PAD - (kw - 1)
    y = cb_ref[...]
    for j in range(kw):
        y = y + cw_ref[j:j + 1, :] * xp_scr[pl.ds(first + j, tb), :]
    tail = xp_scr[pl.ds(first + tb, kw - 1), :]
    xp_scr[pl.ds(first, kw - 1), :] = tail
    return y, tail


def _ssd_kernel(z_ref, xlo_ref, xhi_ref, sm_ref, conv0_ref, s0_ref, cw_ref, cb_ref, dtb_ref, alog_ref,
                dexp_ref, nrm_ref, y_ref, convo_ref, so_ref,
                xp_scr, xbc_scr, g_scr, dt_scr, y_scr, s_scr, *, tb):
    c = pl.program_id(1)
    last = pl.num_programs(1) - 1
    hpg = SSM_HEADS // SSM_GROUPS
    gw = hpg * SSM_HEAD_DIM

    @pl.when(c == 0)
    def _():
        s_scr[...] = s0_ref[...]
        xp_scr[0:CONV_PAD, :] = jnp.zeros((CONV_PAD, SSM_CONV_DIM), F32)
        xp_scr[CONV_PAD - (SSM_CONV - 1):CONV_PAD, :] = conv0_ref[...]

    xp_scr[CONV_PAD:CONV_PAD + tb, 0:GROUP_W] = xlo_ref[...]
    xp_scr[CONV_PAD:CONV_PAD + tb, GROUP_W:SSM_CONV_DIM] = xhi_ref[...]
    conv, tail = _conv_block(xp_scr, cw_ref, cb_ref, tb, SSM_CONV)

    @pl.when(c == last)
    def _():
        convo_ref[...] = tail

    xbc_scr[...] = _silu(conv)
    dt = _softplus(sm_ref[...] + dtb_ref[...])
    dt_scr[...] = dt
    g_scr[...] = dt * (-jnp.exp(alog_ref[...]))

    tri_mask = _tri(CHUNK)
    tri_l = tri_mask.astype(BF16)
    tri_u = _tri(CHUNK, lower=False).astype(BF16)
    b_off = GROUP_W
    c_off = GROUP_W + SSM_GROUPS * SSM_STATE

    def chunk(ci, carry):
        rows = pl.ds(pl.multiple_of(ci * CHUNK, CHUNK), CHUNK)
        gc = g_scr[rows, :]
        dtc = dt_scr[rows, :]
        cum = _dot01(tri_l, gc)
        cum_t = _dot01_r(gc.T, tri_u)
        tot = cum[CHUNK - 1:CHUNK, :]
        ys = []
        for grp in range(SSM_GROUPS):
            bm = xbc_scr[rows, b_off + grp * SSM_STATE:b_off + (grp + 1) * SSM_STATE].astype(BF16)
            cm = xbc_scr[rows, c_off + grp * SSM_STATE:c_off + (grp + 1) * SSM_STATE].astype(BF16)
            gmat = _dot_nt(cm, bm)
            s_g = s_scr[grp]
            cs = _dot(cm, s_g.astype(BF16))
            vdec, etot = [], []
            for hh in range(hpg):
                h = grp * hpg + hh
                col = cum[:, h:h + 1]
                row = cum_t[h:h + 1, :]
                dec = jnp.where(tri_mask, jnp.exp(jnp.minimum(col - row, 0.0)), 0.0)
                att = (gmat * dec).astype(BF16)
                xh = xbc_scr[rows, h * SSM_HEAD_DIM:(h + 1) * SSM_HEAD_DIM]
                vh = xh * dtc[:, h:h + 1]
                o = _dot(att, vh.astype(BF16)) + jnp.exp(col) * cs[:, hh * SSM_HEAD_DIM:(hh + 1) * SSM_HEAD_DIM]
                ys.append(o + dexp_ref[:, h * SSM_HEAD_DIM:(h + 1) * SSM_HEAD_DIM] * xh)
                toth = tot[:, h:h + 1]
                vdec.append(vh * jnp.exp(toth - col))
                etot.append(jnp.broadcast_to(jnp.exp(toth), (1, SSM_HEAD_DIM)))
            vdec = jnp.concatenate(vdec, axis=1).astype(BF16)
            etot = jnp.concatenate(etot, axis=1)
            s_scr[grp] = etot * s_g + _dot_tn(bm, vdec)
        y_scr[rows, :] = jnp.concatenate(ys, axis=1)
        return carry

    lax.fori_loop(0, tb // CHUNK, chunk, 0)

    y = y_scr[...] * _silu(z_ref[...])
    outs = []
    for grp in range(SSM_GROUPS):
        yg = y[:, grp * gw:(grp + 1) * gw]
        ms = jnp.mean(yg * yg, axis=-1, keepdims=True)
        outs.append(yg * lax.rsqrt(ms + EPS))
    y_ref[...] = (jnp.concatenate(outs, axis=1) * nrm_ref[...]).astype(y_ref.dtype)

    @pl.when(c == last)
    def _():
        so_ref[...] = s_scr[...]


def _ssd(proj, small, conv0, s0, cw, cb, dtb, alog, dexp, nrm, nb, t, tb):
    nc = t // tb
    hpg = SSM_HEADS // SSM_GROUPS
    sshape = (SSM_GROUPS, SSM_STATE, hpg * SSM_HEAD_DIM)
    row = lambda col: pl.BlockSpec((tb, GROUP_W), lambda b, c: (b * nc + c, col))
    full = lambda a: pl.BlockSpec(a.shape, lambda b, c: (0,) * a.ndim)
    return pl.pallas_call(
        functools.partial(_ssd_kernel, tb=tb),
        grid=(nb, nc),
        in_specs=[row(COL_A_Z), row(COL_A_X), row(COL_A_BC),
                  pl.BlockSpec((tb, LANES), lambda b, c: (b * nc + c, 0)),
                  pl.BlockSpec((None, SSM_CONV - 1, SSM_CONV_DIM), lambda b, c: (b, 0, 0)),
                  pl.BlockSpec((None,) + sshape, lambda b, c: (b, 0, 0, 0)),
                  full(cw), full(cb), full(dtb), full(alog), full(dexp), full(nrm)],
        out_specs=[pl.BlockSpec((tb, GROUP_W), lambda b, c: (b * nc + c, 0)),
                   pl.BlockSpec((None, SSM_CONV - 1, SSM_CONV_DIM), lambda b, c: (b, 0, 0)),
                   pl.BlockSpec((None,) + sshape, lambda b, c: (b, 0, 0, 0))],
        out_shape=[jax.ShapeDtypeStruct((nb * t, GROUP_W), BF16),
                   jax.ShapeDtypeStruct((nb, SSM_CONV - 1, SSM_CONV_DIM), F32),
                   jax.ShapeDtypeStruct((nb,) + sshape, F32)],
        scratch_shapes=[pltpu.VMEM((CONV_PAD + tb, SSM_CONV_DIM), F32),
                        pltpu.VMEM((tb, SSM_CONV_DIM), F32),
                        pltpu.VMEM((tb, LANES), F32),
                        pltpu.VMEM((tb, LANES), F32),
                        pltpu.VMEM((tb, GROUP_W), F32),
                        pltpu.VMEM(sshape, F32)],
        compiler_params=_cparams(("parallel", "arbitrary")),
        name="ssd",
    )(proj, proj, proj, small, conv0, s0, cw, cb, dtb, alog, dexp, nrm)


def _hgrn_kernel(q_ref, f_ref, i_ref, g_ref, s0_ref, lbl_ref, nrm_ref, y_ref, so_ref,
                 c_scr, k_scr, st_scr, *, tb, layer):
    c = pl.program_id(1)
    last = pl.num_programs(1) - 1
    hd = HGRN_HEAD_DIM
    nsub = CHUNK // SUB

    @pl.when(c == 0)
    def _():
        for h in range(HGRN_HEADS):
            st_scr[h] = s0_ref[h].T

    ll = lbl_ref[...]
    e = jnp.exp(ll - jnp.max(ll, axis=0, keepdims=True))
    p = e / jnp.sum(e, axis=0, keepdims=True)
    cs = p[0:1, :]
    for l in range(1, layer + 1):
        cs = cs + p[l:l + 1, :]
    lb = cs - p[0:1, :]

    tri_l = _tri(CHUNK).astype(BF16)
    rowid = lax.broadcasted_iota(jnp.int32, (CHUNK, 1), 0)
    subid = lax.broadcasted_iota(jnp.int32, (SUB, 1), 0)

    def chunk(ci, carry):
        r0 = pl.multiple_of(ci * CHUNK, CHUNK)
        rows = pl.ds(r0, CHUNK)
        fr = f_ref[rows, :]
        fg = lb + (1.0 - lb) * _sigmoid(fr)
        logf = jnp.log(jnp.maximum(fg, HGRN_F_FLOOR))
        hk = (1.0 - lb) * _sigmoid(-fr)
        hq = _silu(q_ref[rows, :])
        v = i_ref[rows, :]
        vb = v.astype(BF16)
        cum = _dot01(tri_l, logf)
        tot = cum[CHUNK - 1:CHUNK, :]
        c_scr[...] = cum
        k_scr[...] = hk
        qe = (hq * jnp.exp(cum)).astype(BF16)
        kd = (hk * jnp.exp(tot - cum)).astype(BF16)
        etot = jnp.exp(tot)

        qcat, kcat = [], []
        for j in range(nsub - 1):
            ej = cum[(j + 1) * SUB - 1:(j + 1) * SUB, :]
            qcat.append(jnp.where(rowid >= (j + 1) * SUB, hq * jnp.exp(jnp.minimum(cum - ej, 0.0)), 0.0).astype(BF16))
            in_j = (rowid >= j * SUB) & (rowid < (j + 1) * SUB)
            kcat.append(jnp.where(in_j, hk * jnp.exp(jnp.minimum(ej - cum, 0.0)), 0.0).astype(BF16))

        o_heads = []
        for h in range(HGRN_HEADS):
            hs = slice(h * hd, (h + 1) * hd)
            qc = jnp.concatenate([q[:, hs] for q in qcat], axis=1)
            kc = jnp.concatenate([k[:, hs] for k in kcat], axis=1)
            a_off = _dot_nt(qc, kc)
            st = st_scr[h]
            o_heads.append(_dot(a_off.astype(BF16), vb[:, hs]) + _dot_nt(qe[:, hs], st.astype(BF16)))
            st_scr[h] = st * etot[:, hs] + _dot_tn(vb[:, hs], kd[:, hs])
        o = jnp.concatenate(o_heads, axis=1)

        o_sub = []
        for i in range(nsub):
            base = i * SUB
            ci_ = cum[base:base + SUB, :]
            qi = hq[base:base + SUB, :]
            acc = [jnp.zeros((SUB, hd), F32) for _ in range(HGRN_HEADS)]
            for j in range(SUB):
                crow = c_scr[pl.ds(base + j, 1), :]
                krow = k_scr[pl.ds(base + j, 1), :]
                vrow = i_ref[pl.ds(r0 + base + j, 1), :]
                x = jnp.where(subid >= j, qi * jnp.exp(jnp.minimum(ci_ - crow, 0.0)) * krow, 0.0)
                for h in range(HGRN_HEADS):
                    hs = slice(h * hd, (h + 1) * hd)
                    a = jnp.sum(x[:, hs], axis=-1, keepdims=True)
                    acc[h] = acc[h] + a * vrow[:, hs]
            o_sub.append(jnp.concatenate(acc, axis=1))
        o = o + jnp.concatenate(o_sub, axis=0)

        outs = []
        for h in range(HGRN_HEADS):
            oh = o[:, h * hd:(h + 1) * hd]
            ms = jnp.mean(oh * oh, axis=-1, keepdims=True)
            outs.append(oh * lax.rsqrt(ms + EPS))
        y = jnp.concatenate(outs, axis=1) * nrm_ref[...] * _silu(g_ref[rows, :])
        y_ref[rows, :] = y.astype(y_ref.dtype)
        return carry

    lax.fori_loop(0, tb // CHUNK, chunk, 0)

    @pl.when(c == last)
    def _():
        for h in range(HGRN_HEADS):
            so_ref[h] = st_scr[h].T


def _hgrn(proj, s0, lbl, nrm, nb, t, tb, layer):
    nc = t // tb
    sshape = (HGRN_HEADS, HGRN_HEAD_DIM, HGRN_HEAD_DIM)
    row = lambda col: pl.BlockSpec((tb, GROUP_W), lambda b, c: (b * nc + c, col))
    full = lambda a: pl.BlockSpec(a.shape, lambda b, c: (0,) * a.ndim)
    return pl.pallas_call(
        functools.partial(_hgrn_kernel, tb=tb, layer=layer),
        grid=(nb, nc),
        in_specs=[row(COL_B_Q), row(COL_B_F), row(COL_B_I), row(COL_B_G),
                  pl.BlockSpec((None,) + sshape, lambda b, c: (b, 0, 0, 0)),
                  full(lbl), full(nrm)],
        out_specs=[pl.BlockSpec((tb, GROUP_W), lambda b, c: (b * nc + c, 0)),
                   pl.BlockSpec((None,) + sshape, lambda b, c: (b, 0, 0, 0))],
        out_shape=[jax.ShapeDtypeStruct((nb * t, GROUP_W), BF16),
                   jax.ShapeDtypeStruct((nb,) + sshape, F32)],
        scratch_shapes=[pltpu.VMEM((CHUNK, GROUP_W), F32),
                        pltpu.VMEM((CHUNK, GROUP_W), F32),
                        pltpu.VMEM(sshape, F32)],
        compiler_params=_cparams(("parallel", "arbitrary")),
        name="hgrn2",
    )(proj, proj, proj, proj, s0, lbl, nrm)


def _lru_kernel(x_ref, g_ref, conv0_ref, h0_ref, cw_ref, cb_ref, wr_ref, br_ref, wi_ref, bi_ref, lam_ref,
                y_ref, convo_ref, ho_ref, xp_scr, h_scr, *, tb, reset_first):
    c = pl.program_id(1)
    last = pl.num_programs(1) - 1

    @pl.when(c == 0)
    def _():
        h_scr[...] = h0_ref[...]
        xp_scr[0:CONV_PAD, :] = jnp.zeros((CONV_PAD, GROUP_W), F32)
        xp_scr[CONV_PAD - (LRU_CONV - 1):CONV_PAD, :] = conv0_ref[...]

    xp_scr[CONV_PAD:CONV_PAD + tb, :] = x_ref[...]
    xc, tail = _conv_block(xp_scr, cw_ref, cb_ref, tb, LRU_CONV)

    @pl.when(c == last)
    def _():
        convo_ref[...] = tail

    xcb = xc.astype(BF16)

    def blockdiag(w_ref):
        return jnp.concatenate(
            [_dot(xcb[:, n * LRU_BLOCK_W:(n + 1) * LRU_BLOCK_W], w_ref[n]) for n in range(LRU_BLOCKS)], axis=1)

    r = _sigmoid(blockdiag(wr_ref) + br_ref[...])
    gi = _sigmoid(blockdiag(wi_ref) + bi_ref[...])
    log_a = LRU_C * r * _log_sigmoid(lam_ref[...])
    a = jnp.exp(log_a)
    mult = jnp.sqrt(-jnp.tanh(log_a) * (a * a + 1.0))
    rowid = lax.broadcasted_iota(jnp.int32, (tb, 1), 0)
    first_row = rowid == 0
    if reset_first:
        rst = first_row & (c == 0)
        a = jnp.where(rst, 0.0, a)
        mult = jnp.where(rst, 1.0, mult)
    u = mult * gi * xc
    u = u + jnp.where(first_row, a * h_scr[...], 0.0)

    av, bv = a, u
    d = 1
    while d < tb:
        m = rowid >= d
        a_sh = pltpu.roll(av, d, axis=0)
        b_sh = pltpu.roll(bv, d, axis=0)
        bv = jnp.where(m, av * b_sh + bv, bv)
        av = jnp.where(m, av * a_sh, av)
        d *= 2
    h_new = bv[tb - 1:tb, :]
    h_scr[...] = h_new
    y_ref[...] = (bv * jax.nn.gelu(g_ref[...])).astype(y_ref.dtype)

    @pl.when(c == last)
    def _():
        ho_ref[...] = h_new


def _lru(proj, conv0, h0, cw, cb, wr, br, wi, bi, lam, nb, t, tb, reset_first):
    nc = t // tb
    row = lambda col: pl.BlockSpec((tb, GROUP_W), lambda b, c: (b * nc + c, col))
    full = lambda a: pl.BlockSpec(a.shape, lambda b, c: (0,) * a.ndim)
    return pl.pallas_call(
        functools.partial(_lru_kernel, tb=tb, reset_first=reset_first),
        grid=(nb, nc),
        in_specs=[row(COL_D_X), row(COL_D_G),
                  pl.BlockSpec((None, LRU_CONV - 1, GROUP_W), lambda b, c: (b, 0, 0)),
                  pl.BlockSpec((None, 1, GROUP_W), lambda b, c: (b, 0, 0)),
                  full(cw), full(cb), full(wr), full(br), full(wi), full(bi), full(lam)],
        out_specs=[pl.BlockSpec((tb, GROUP_W), lambda b, c: (b * nc + c, 0)),
                   pl.BlockSpec((None, LRU_CONV - 1, GROUP_W), lambda b, c: (b, 0, 0)),
                   pl.BlockSpec((None, 1, GROUP_W), lambda b, c: (b, 0, 0))],
        out_shape=[jax.ShapeDtypeStruct((nb * t, GROUP_W), BF16),
                   jax.ShapeDtypeStruct((nb, LRU_CONV - 1, GROUP_W), F32),
                   jax.ShapeDtypeStruct((nb, 1, GROUP_W), F32)],
        scratch_shapes=[pltpu.VMEM((CONV_PAD + tb, GROUP_W), F32),
                        pltpu.VMEM((1, GROUP_W), F32)],
        compiler_params=_cparams(("parallel", "arbitrary")),
        name="rglru",
    )(proj, proj, conv0, h0, cw, cb, wr, br, wi, bi, lam)


def _gate_kernel(*refs, tb, past):
    if past:
        sm_ref, fb_ref, plf_ref, lf_ref, cum_ref, cump_ref, carry_scr = refs
    else:
        sm_ref, fb_ref, lf_ref, cum_ref, carry_scr = refs

    @pl.when(pl.program_id(1) == 0)
    def _():
        if past:
            cp = _dot01(_tri(past).astype(BF16), plf_ref[...])
            cump_ref[...] = cp
            carry_scr[...] = cp[past - 1:past, :]
        else:
            carry_scr[...] = jnp.zeros_like(carry_scr)

    lf = _log_sigmoid(sm_ref[...] + fb_ref[...])
    lf_ref[...] = lf
    cum = _dot01(_tri(tb).astype(BF16), lf) + carry_scr[...]
    cum_ref[...] = cum
    carry_scr[...] = cum[tb - 1:tb, :]


def _gate(small, fb, past_lf, nb, t, tb):
    nc = t // tb
    past = 0 if past_lf is None else past_lf.shape[1]
    rows = pl.BlockSpec((tb, LANES), lambda b, c: (b * nc + c, 0))
    in_specs = [pl.BlockSpec((tb, LANES), lambda b, c: (b * nc + c, 1)),
                pl.BlockSpec(fb.shape, lambda b, c: (0, 0))]
    out_specs = [rows, rows]
    out_shape = [jax.ShapeDtypeStruct((nb * t, LANES), F32)] * 2
    args = [small, fb]
    if past:
        in_specs.append(pl.BlockSpec((None, past, LANES), lambda b, c: (b, 0, 0)))
        out_specs.append(pl.BlockSpec((None, past, LANES), lambda b, c: (b, 0, 0)))
        out_shape.append(jax.ShapeDtypeStruct((nb, past, LANES), F32))
        args.append(past_lf)
    return pl.pallas_call(
        functools.partial(_gate_kernel, tb=tb, past=past),
        grid=(nb, nc),
        in_specs=in_specs, out_specs=out_specs, out_shape=out_shape,
        scratch_shapes=[pltpu.VMEM((1, LANES), F32)],
        compiler_params=_cparams(("parallel", "arbitrary")),
        name="fox_gate",
    )(*args)


def _head_norm(x, gain, hsum):
    ms = _dot01_r(x * x, hsum) * (1.0 / FOX_HEAD_DIM)
    return x * lax.rsqrt(ms + EPS) * gain


def _store_heads(ref, x):
    xt = x.T
    for h in range(FOX_HEADS):
        ref[h] = xt[h * FOX_HEAD_DIM:(h + 1) * FOX_HEAD_DIM, :]


def _foxprep_kernel(kbuf_ref, vbuf_ref, q_ref, k_ref, v_ref, hsum_ref, qn_ref, kn_ref,
                    fq_ref, fk_ref, fv_ref, kb_ref, vb_ref):
    del kbuf_ref, vbuf_ref
    hsum = hsum_ref[...]
    fq_ref[...] = (_head_norm(q_ref[...], qn_ref[...], hsum) * (FOX_HEAD_DIM ** -0.5)).astype(BF16)
    fk = _head_norm(k_ref[...], kn_ref[...], hsum)
    _store_heads(fk_ref, fk)
    kb_ref[...] = fk.astype(BF16)
    v = v_ref[...]
    _store_heads(fv_ref, v)
    vb_ref[...] = v.astype(BF16)


def _heads_spec(tb, t, first_stream):
    nc = t // tb
    return pl.BlockSpec((None, FOX_HEADS, FOX_HEAD_DIM, tb), lambda i: (first_stream + i // nc, 0, 0, i % nc))


_ANY = pl.BlockSpec(memory_space=pl.ANY)


def _foxprep(kv, layer, proj, hsum, qn, kn, t, tb):
    m = proj.shape[0]
    row = lambda col: pl.BlockSpec((tb, GROUP_W), lambda i: (i, col))
    full = lambda a: pl.BlockSpec(a.shape, lambda i: (0,) * a.ndim)
    out_row = pl.BlockSpec((tb, GROUP_W), lambda i: (i, 0))
    heads = _heads_spec(tb, t, layer * (m // t))
    return pl.pallas_call(
        _foxprep_kernel,
        grid=(m // tb,),
        in_specs=[_ANY, _ANY, row(COL_C_Q), row(COL_C_K), row(COL_C_V), full(hsum), full(qn), full(kn)],
        out_specs=[out_row, heads, heads, out_row, out_row],
        out_shape=[jax.ShapeDtypeStruct((m, GROUP_W), BF16),
                   jax.ShapeDtypeStruct(kv[0].shape, F32),
                   jax.ShapeDtypeStruct(kv[1].shape, F32),
                   jax.ShapeDtypeStruct((m, GROUP_W), BF16),
                   jax.ShapeDtypeStruct((m, GROUP_W), BF16)],
        input_output_aliases={0: 1, 1: 2},
        compiler_params=_cparams(("parallel",)),
        name="fox_prep",
    )(kv[0], kv[1], proj, proj, proj, hsum, qn, kn)


AUG_W = FOX_HEADS * LANES
N_PIECES = 3


def _aug_constants():
    pq = np.zeros((N_PIECES * LANES, AUG_W), np.float32)
    pk = np.zeros((N_PIECES * LANES, AUG_W), np.float32)
    ones_q = np.zeros((1, AUG_W), np.float32)
    ones_k = np.zeros((1, AUG_W), np.float32)
    ones_v = np.zeros((1, AUG_W), np.float32)
    head = np.zeros((1, AUG_W), np.float32)
    for h in range(FOX_HEADS):
        own = h * LANES + FOX_HEAD_DIM * (h % 2)
        other = h * LANES + FOX_HEAD_DIM * (1 - h % 2)
        head[0, own:own + FOX_HEAD_DIM] = 1.0
        for piece in range(N_PIECES):
            pq[piece * LANES + h, other + piece] = 1.0
            ones_k[0, other + piece] = 1.0
            ones_q[0, other + N_PIECES + piece] = 1.0
            pk[piece * LANES + h, other + N_PIECES + piece] = -1.0
        ones_v[0, other] = 1.0
    return dict(aug_pq=jnp.asarray(pq, BF16), aug_pk=jnp.asarray(pk, BF16), aug_ones_q=jnp.asarray(ones_q),
                aug_ones_k=jnp.asarray(ones_k), aug_ones_v=jnp.asarray(ones_v), aug_head=jnp.asarray(head))


def _foxprep_aug_kernel(kbuf_ref, vbuf_ref, q_ref, k_ref, v_ref, cum_ref, hsum_ref, qn_ref, kn_ref, pq_ref, pk_ref,
                        oq_ref, ok_ref, ov_ref, head_ref, fk_ref, fv_ref, qa_ref, ka_ref, va_ref):
    del kbuf_ref, vbuf_ref
    hsum = hsum_ref[...]
    fq = _head_norm(q_ref[...], qn_ref[...], hsum) * (FOX_HEAD_DIM ** -0.5)
    fk = _head_norm(k_ref[...], kn_ref[...], hsum)
    v = v_ref[...]
    _store_heads(fk_ref, fk)
    _store_heads(fv_ref, v)
    pieces = jnp.concatenate(_split3(cum_ref[...]), axis=1)
    own = head_ref[...] > 0.5

    def per_head(x):
        return jnp.concatenate([x[:, (h // 2) * LANES:(h // 2 + 1) * LANES] for h in range(FOX_HEADS)], axis=1)

    qa = jnp.where(own, per_head(fq), _dot(pieces, pq_ref[...]) + oq_ref[...]).astype(BF16)
    ka = jnp.where(own, per_head(fk), _dot(pieces, pk_ref[...]) + ok_ref[...]).astype(BF16)
    va = jnp.where(own, per_head(v), ov_ref[...]).astype(BF16)
    for h in range(FOX_HEADS):
        hs = slice(h * LANES, (h + 1) * LANES)
        qa_ref[h] = qa[:, hs]
        ka_ref[h] = ka[:, hs]
        va_ref[h] = va[:, hs]


def _foxprep_aug(kv, layer, proj, cum, hsum, qn, kn, aug, t, tb):
    m = proj.shape[0]
    row = lambda col: pl.BlockSpec((tb, GROUP_W), lambda i: (i, col))
    full = lambda a: pl.BlockSpec(a.shape, lambda i: (0,) * a.ndim)
    out_aug = pl.BlockSpec((FOX_HEADS, tb, LANES), lambda i: (0, i, 0))
    heads = _heads_spec(tb, t, layer * (m // t))
    consts = [aug['aug_pq'], aug['aug_pk'], aug['aug_ones_q'], aug['aug_ones_k'], aug['aug_ones_v'],
              aug['aug_head']]
    return pl.pallas_call(
        _foxprep_aug_kernel,
        grid=(m // tb,),
        in_specs=[_ANY, _ANY, row(COL_C_Q), row(COL_C_K), row(COL_C_V),
                  pl.BlockSpec((tb, LANES), lambda i: (i, 0)),
                  full(hsum), full(qn), full(kn)] + [full(a) for a in consts],
        out_specs=[heads, heads, out_aug, out_aug, out_aug],
        out_shape=[jax.ShapeDtypeStruct(kv[0].shape, F32), jax.ShapeDtypeStruct(kv[1].shape, F32)]
                  + [jax.ShapeDtypeStruct((FOX_HEADS, m, LANES), BF16)] * 3,
        input_output_aliases={0: 0, 1: 1},
        compiler_params=_cparams(("parallel",)),
        name="fox_prep_aug",
    )(kv[0], kv[1], proj, proj, proj, cum, hsum, qn, kn, *consts)


def _attn_aug_kernel(q_ref, k_ref, v_ref, g_ref, y_ref, *, tq, hg):
    qi = pl.program_id(2)
    causal = _tri(tq)
    low = lax.broadcasted_iota(jnp.int32, (1, LANES), 1) < FOX_HEAD_DIM
    qs = [q_ref[i] for i in range(hg)]

    def step(j, carry, masked):
        rows = pl.ds(pl.multiple_of(j * tq, tq), tq)
        new = []
        for i in range(hg):
            m, acc = carry[i]
            s = _dot_nt(qs[i], k_ref[i, rows, :])
            if masked:
                s = jnp.where(causal, s, NEG)
            m_new = jnp.maximum(m, jnp.max(s, axis=-1, keepdims=True))
            p = jnp.exp(s - m_new).astype(BF16)
            acc = jnp.exp(m - m_new) * acc + _dot(p, v_ref[i, rows, :])
            new.append((m_new, acc))
        return tuple(new)

    init = tuple((jnp.full((tq, 1), NEG, F32), jnp.zeros((tq, LANES), F32)) for _ in range(hg))
    carry = lax.fori_loop(0, qi, lambda j, c: step(j, c, False), init)
    carry = step(qi, carry, True)
    outs = []
    for i in range(0, hg, 2):
        acc_e, acc_o = carry[i][1], carry[i + 1][1]
        out_e = acc_e / acc_e[:, FOX_HEAD_DIM:FOX_HEAD_DIM + 1]
        out_o = acc_o / acc_o[:, 0:1]
        outs.append(jnp.where(low, out_e, out_o))
    y = jnp.concatenate(outs, axis=1) * _sigmoid(g_ref[...])
    y_ref[...] = y.astype(y_ref.dtype)


def _attention_aug(qa, ka, va, proj, nb, t, tq, hg):
    nq = t // tq
    ngrp = FOX_HEADS // hg
    wout = hg * FOX_HEAD_DIM
    gate_col0 = COL_C_G * GROUP_W // wout
    return pl.pallas_call(
        functools.partial(_attn_aug_kernel, tq=tq, hg=hg),
        grid=(nb, ngrp, nq),
        in_specs=[pl.BlockSpec((hg, tq, LANES), lambda b, g, i: (g, b * nq + i, 0)),
                  pl.BlockSpec((hg, t, LANES), lambda b, g, i: (g, b, 0)),
                  pl.BlockSpec((hg, t, LANES), lambda b, g, i: (g, b, 0)),
                  pl.BlockSpec((tq, wout), lambda b, g, i: (b * nq + i, gate_col0 + g))],
        out_specs=pl.BlockSpec((tq, wout), lambda b, g, i: (b * nq + i, g)),
        out_shape=jax.ShapeDtypeStruct((nb * t, GROUP_W), BF16),
        compiler_params=_cparams(("parallel", "parallel", "arbitrary")),
        name="fox_attention_aug",
    )(qa, ka, va, proj)


def _attn_cached_kernel(q_ref, cq_ref, ck_ref, k_ref, v_ref, g_ref, pk_ref, pv_ref, y_ref, *, t, past):
    causal = _tri(t)
    outs = []
    for h in range(FOX_HEADS):
        hs = slice(h * FOX_HEAD_DIM, (h + 1) * FOX_HEAD_DIM)
        qh = q_ref[:, hs]
        pk = pk_ref[h].astype(BF16)
        pv = pv_ref[h].astype(BF16)
        cqh = cq_ref[:, h:h + 1]
        s_p = _dot(qh, pk) + cqh - ck_ref[h:h + 1, 0:past]
        s_c = jnp.where(causal, _dot_nt(qh, k_ref[:, hs]) + cqh - ck_ref[h:h + 1, past:past + t], NEG)
        m = jnp.maximum(jnp.max(s_p, axis=-1, keepdims=True), jnp.max(s_c, axis=-1, keepdims=True))
        p_p = jnp.exp(s_p - m)
        p_c = jnp.exp(s_c - m)
        l = jnp.sum(p_p, axis=-1, keepdims=True) + jnp.sum(p_c, axis=-1, keepdims=True)
        acc = _dot_nt(p_p.astype(BF16), pv) + _dot(p_c.astype(BF16), v_ref[:, hs])
        outs.append(acc / l)
    y = jnp.concatenate(outs, axis=1) * _sigmoid(g_ref[...])
    y_ref[...] = y.astype(y_ref.dtype)


def _attention_cached(fq, cq, ck_t, kb, vb, proj, past_k, past_v, layer, nb, t):
    past = past_k.shape[-1]
    ltot = ck_t.shape[-1]
    seq = lambda rows: pl.BlockSpec((None, rows, GROUP_W), lambda b: (b, 0, 0))
    cache = pl.BlockSpec((None, FOX_HEADS, FOX_HEAD_DIM, past), lambda b: (layer * nb + b, 0, 0, 0))
    return pl.pallas_call(
        functools.partial(_attn_cached_kernel, t=t, past=past),
        grid=(nb,),
        in_specs=[pl.BlockSpec((t, GROUP_W), lambda b: (b, 0)),
                  pl.BlockSpec((None, t, LANES), lambda b: (b, 0, 0)),
                  pl.BlockSpec((None, FOX_HEADS, ltot), lambda b: (b, 0, 0)),
                  seq(t), seq(t),
                  pl.BlockSpec((t, GROUP_W), lambda b: (b, COL_C_G)),
                  cache, cache],
        out_specs=pl.BlockSpec((t, GROUP_W), lambda b: (b, 0)),
        out_shape=jax.ShapeDtypeStruct((nb * t, GROUP_W), BF16),
        compiler_params=_cparams(("parallel",)),
        name="fox_attention_cached",
    )(fq, cq, ck_t, kb.reshape(nb, t, GROUP_W), vb.reshape(nb, t, GROUP_W), proj, past_k, past_v)


def _out_proj_kernel(x_ref, ya_ref, yb_ref, yc_ref, yd_ref, w_ref, o_ref):
    acc = x_ref[...]
    for n, y_ref in enumerate((ya_ref, yb_ref, yc_ref, yd_ref)):
        acc = acc + _dot(y_ref[...], w_ref[n * GROUP_W:(n + 1) * GROUP_W, :])
    o_ref[...] = acc


def _out_proj(x2d, ys, w_out, layer, tm):
    m = x2d.shape[0]
    yspec = pl.BlockSpec((tm, GROUP_W), lambda i: (i, 0))
    return pl.pallas_call(
        _out_proj_kernel,
        grid=(m // tm,),
        in_specs=[pl.BlockSpec((tm, D_MODEL), lambda i: (i, 0)), yspec, yspec, yspec, yspec,
                  pl.BlockSpec((None,) + w_out.shape[1:], lambda i: (layer, 0, 0))],
        out_specs=pl.BlockSpec((tm, D_MODEL), lambda i: (i, 0)),
        out_shape=jax.ShapeDtypeStruct((m, D_MODEL), F32),
        compiler_params=_cparams(("parallel",)),
        name="out_proj",
    )(x2d, *ys, w_out)


def _ffn_kernel(x_ref, g_ref, wg_ref, wu_ref, wd_ref, o_ref, h_scr, acc_scr):
    f = pl.program_id(1)

    @pl.when(f == 0)
    def _():
        x = x_ref[...]
        ms = jnp.mean(x * x, axis=-1, keepdims=True)
        h_scr[...] = (x * lax.rsqrt(ms + EPS) * g_ref[...]).astype(BF16)
        acc_scr[...] = x

    h = h_scr[...]
    a = _silu(_dot(h, wg_ref[...])) * _dot(h, wu_ref[...])
    acc_scr[...] += _dot(a.astype(BF16), wd_ref[...])

    @pl.when(f == pl.num_programs(1) - 1)
    def _():
        o_ref[...] = acc_scr[...]


def _ffn(x2d, ln, wg, wu, wd, layer, tm, tf):
    m = x2d.shape[0]
    d_ff = wg.shape[2]
    return pl.pallas_call(
        _ffn_kernel,
        grid=(m // tm, d_ff // tf),
        in_specs=[pl.BlockSpec((tm, D_MODEL), lambda i, f: (i, 0)),
                  pl.BlockSpec((1, D_MODEL), lambda i, f: (0, 0)),
                  pl.BlockSpec((None, D_MODEL, tf), lambda i, f: (layer, 0, f)),
                  pl.BlockSpec((None, D_MODEL, tf), lambda i, f: (layer, 0, f)),
                  pl.BlockSpec((None, tf, D_MODEL), lambda i, f: (layer, f, 0))],
        out_specs=pl.BlockSpec((tm, D_MODEL), lambda i, f: (i, 0)),
        out_shape=jax.ShapeDtypeStruct((m, D_MODEL), F32),
        scratch_shapes=[pltpu.VMEM((tm, D_MODEL), BF16), pltpu.VMEM((tm, D_MODEL), F32)],
        compiler_params=_cparams(("parallel", "arbitrary")),
        name="swiglu_ffn",
    )(x2d, ln, wg, wu, wd)


def _pad_lanes(v, offset=0):
    return jnp.zeros((1, LANES), F32).at[0, offset:offset + v.shape[0]].set(v.astype(F32))


def _ssm_state_to_groups(s):
    nb = s.shape[0]
    hpg = SSM_HEADS // SSM_GROUPS
    s = s.reshape(nb, SSM_GROUPS, hpg, SSM_STATE, SSM_HEAD_DIM)
    return jnp.transpose(s, (0, 1, 3, 2, 4)).reshape(nb, SSM_GROUPS, SSM_STATE, hpg * SSM_HEAD_DIM)


def _ssm_state_from_groups(s):
    nb = s.shape[0]
    hpg = SSM_HEADS // SSM_GROUPS
    s = s.reshape(nb, SSM_GROUPS, SSM_STATE, hpg, SSM_HEAD_DIM)
    return jnp.transpose(s, (0, 1, 3, 2, 4)).reshape(nb, SSM_HEADS, SSM_STATE, SSM_HEAD_DIM)


def _block_rows(t, want):
    return want if t % want == 0 else t


def _layer(x2d, nb, t, state, pr, layer, kv):
    (ssm_conv, ssm_s, hgrn_s, fox_k, fox_v, fox_lf, lru_conv, lru_h) = state
    m = nb * t
    past = 0 if fox_k is None else fox_k.shape[2]
    tm = _block_rows(m, 512)
    tb = _block_rows(t, 256)

    proj, small = _in_proj(x2d, pr['ln1'], pr['w_main'], pr['w_small'], layer, _block_rows(m, 1024))

    ya, ssm_conv_new, ssm_g = _ssd(proj, small, ssm_conv, _ssm_state_to_groups(ssm_s), pr['ssm_conv_w'],
                                   pr['ssm_conv_b'], pr['ssm_dt_bias'], pr['ssm_a_log'], pr['ssm_d'],
                                   pr['ssm_norm'], nb, t, tb)
    ssm_s_new = _ssm_state_from_groups(ssm_g)

    yb, hgrn_s_new = _hgrn(proj, hgrn_s, pr['hgrn_lb_logits'], pr['hgrn_norm'], nb, t, tb, layer)

    yd, lru_conv_new, lru_h_new = _lru(proj, lru_conv, lru_h.reshape(nb, 1, GROUP_W), pr['lru_conv_w'],
                                       pr['lru_conv_b'], pr['lru_w_r'], pr['lru_b_r'], pr['lru_w_i'],
                                       pr['lru_b_i'], pr['lru_lambda'], nb, t, tb, reset_first=(past == 0))

    if past:
        past_lf = jnp.pad(fox_lf.astype(F32), ((0, 0), (0, 0), (0, LANES - FOX_HEADS)))
        lf, cum, cum_past = _gate(small, pr['fox_f_bias'], past_lf, nb, t, t)
        fq, fk, fv, kb, vb = _foxprep(kv, layer, proj, pr['head_sum'], pr['fox_q_norm'], pr['fox_k_norm'], t,
                                      min(t, tm))
        cq = cum.reshape(nb, t, LANES)
        ck_t = jnp.transpose(jnp.concatenate([cum_past, cq], axis=1)[:, :, :FOX_HEADS], (0, 2, 1))
        per_head = lambda c: jnp.transpose(c, (0, 1, 3, 4, 2)).reshape(-1, FOX_HEADS, FOX_HEAD_DIM, past)
        yc = _attention_cached(fq, cq, ck_t, kb, vb, proj, per_head(fox_k), per_head(fox_v), layer, nb, t)
    else:
        lf, cum = _gate(small, pr['fox_f_bias'], None, nb, t, _block_rows(t, 512))
        fk, fv, qa, ka, va = _foxprep_aug(kv, layer, proj, cum, pr['head_sum'], pr['fox_q_norm'],
                                          pr['fox_k_norm'], pr['aug'], t, min(t, tm))
        yc = _attention_aug(qa, ka, va, proj, nb, t, _block_rows(t, 512), FOX_HEADS)
    lf = lf.reshape(nb, t, LANES)

    x1 = _out_proj(x2d, (ya, yb, yc, yd), pr['w_out'], layer, tm)
    x2 = _ffn(x1, pr['ln2'], pr['w_gate'], pr['w_up'], pr['w_down'], layer, tm, 512)

    new_state = (ssm_conv_new, ssm_s_new, hgrn_s_new, lf[:, :, :FOX_HEADS], lru_conv_new,
                 lru_h_new.reshape(nb, GROUP_W))
    return x2, new_state, (fk, fv)


def _prep_layer_params(l, ln1, ln2, ssm_conv_w, ssm_conv_b, ssm_dt_bias, ssm_a_log, ssm_d,
                       ssm_norm, hgrn_lb_logits, hgrn_norm, fox_q_norm, fox_k_norm, fox_f_bias, lru_conv_w,
                       lru_conv_b, lru_w_r, lru_b_r, lru_w_i, lru_b_i, lru_lambda):
    hid = jnp.arange(GROUP_W) // FOX_HEAD_DIM
    row = lambda v: v.astype(F32).reshape(1, -1)
    return dict(
        ln1=row(ln1[l]), ln2=row(ln2[l]),
        ssm_conv_w=ssm_conv_w[l], ssm_conv_b=row(ssm_conv_b[l]),
        ssm_dt_bias=_pad_lanes(ssm_dt_bias[l]), ssm_a_log=_pad_lanes(ssm_a_log[l]),
        ssm_d=row(jnp.repeat(ssm_d[l], SSM_HEAD_DIM)), ssm_norm=row(ssm_norm[l]),
        hgrn_lb_logits=hgrn_lb_logits.astype(F32), hgrn_norm=row(hgrn_norm[l]),
        head_sum=(hid[:, None] == hid[None, :]).astype(BF16), aug=_aug_constants(),
        fox_q_norm=row(jnp.tile(fox_q_norm[l], FOX_HEADS)), fox_k_norm=row(jnp.tile(fox_k_norm[l], FOX_HEADS)),
        fox_f_bias=_pad_lanes(fox_f_bias[l]),
        lru_conv_w=lru_conv_w[l], lru_conv_b=row(lru_conv_b[l]),
        lru_w_r=lru_w_r[l].astype(BF16), lru_b_r=row(lru_b_r[l]),
        lru_w_i=lru_w_i[l].astype(BF16), lru_b_i=row(lru_b_i[l]), lru_lambda=row(lru_lambda[l]))


def _prep_dense_weights(w_in, w_out, w_gate, w_up, w_down):
    depth = w_in.shape[0]
    o_dt = GROUP_W + SSM_CONV_DIM
    o_cf = o_dt + SSM_HEADS + 7 * GROUP_W
    w_main = jnp.concatenate([w_in[:, :, :o_dt], w_in[:, :, o_dt + SSM_HEADS:o_cf], w_in[:, :, o_cf + FOX_HEADS:]],
                             axis=2)
    w_small = jnp.zeros((depth, D_MODEL, 2 * LANES), F32)
    w_small = w_small.at[:, :, :SSM_HEADS].set(w_in[:, :, o_dt:o_dt + SSM_HEADS])
    w_small = w_small.at[:, :, LANES:LANES + FOX_HEADS].set(w_in[:, :, o_cf:o_cf + FOX_HEADS])
    return dict(w_main=w_main.astype(BF16), w_small=w_small.astype(BF16), w_out=w_out.astype(BF16),
                w_gate=w_gate.astype(BF16), w_up=w_up.astype(BF16), w_down=w_down.astype(BF16))


def kernel(x_prompt, x_sample, cache_fox_k, cache_fox_v, cache_fox_logf, state_ssm_conv, state_ssm, state_hgrn,
           state_lru_conv, state_lru, ln1, ln2, w_in, w_out, ssm_conv_w, ssm_conv_b, ssm_dt_bias, ssm_a_log,
           ssm_d, ssm_norm, hgrn_lb_logits, hgrn_norm, fox_q_norm, fox_k_norm, fox_f_bias, lru_conv_w,
           lru_conv_b, lru_w_r, lru_b_r, lru_w_i, lru_b_i, lru_lambda, w_gate, w_up, w_down):
    depth = ln1.shape[0]
    bp, tp, _ = x_prompt.shape
    bs, ts, _ = x_sample.shape
    fresh = (jnp.zeros((bp, SSM_CONV - 1, SSM_CONV_DIM), F32),
             jnp.zeros((bp, SSM_HEADS, SSM_STATE, SSM_HEAD_DIM), F32),
             jnp.zeros((bp, HGRN_HEADS, HGRN_HEAD_DIM, HGRN_HEAD_DIM), F32),
             None, None, None,
             jnp.zeros((bp, LRU_CONV - 1, GROUP_W), F32),
             jnp.zeros((bp, GROUP_W), F32))
    yp = x_prompt.reshape(bp * tp, D_MODEL)
    ys = x_sample.reshape(bs * ts, D_MODEL)
    p_states, s_states = [], []
    new_kv = lambda nb, t: tuple(jnp.zeros((depth * nb, FOX_HEADS, FOX_HEAD_DIM, t), F32) for _ in range(2))
    kv_p, kv_s = new_kv(bp, tp), new_kv(bs, ts)
    dense =_prep_dense_weights(w_in, w_out, w_gate, w_up, w_down)
    for l in range(depth):
        pr = _prep_layer_params(l, ln1, ln2, ssm_conv_w, ssm_conv_b, ssm_dt_bias, ssm_a_log, ssm_d,
                                ssm_norm, hgrn_lb_logits, hgrn_norm, fox_q_norm, fox_k_norm, fox_f_bias,
                                lru_conv_w, lru_conv_b, lru_w_r, lru_b_r, lru_w_i, lru_b_i, lru_lambda)
        pr.update(dense)
        yp, st_p, kv_p = _layer(yp, bp, tp, fresh, pr, l, kv_p)
        p_states.append(st_p)
        past = (state_ssm_conv[l], state_ssm[l], state_hgrn[l], cache_fox_k, cache_fox_v,
                cache_fox_logf[l], state_lru_conv[l], state_lru[l])
        ys, st_s, kv_s = _layer(ys, bs, ts, past, pr, l, kv_s)
        s_states.append(st_s)

    def group(states, kv, nb, t):
        conv, ssm, hgrn, lf, lru_conv, lru = [jnp.stack(parts, axis=0) for parts in zip(*states)]
        k, v = [jnp.transpose(a.reshape(depth, nb, FOX_HEADS, FOX_HEAD_DIM, t), (0, 1, 4, 2, 3)) for a in kv]
        return conv, ssm, hgrn, k, v, lf, lru_conv, lru

    return (yp.reshape(bp, tp, D_MODEL), ys.reshape(bs, ts, D_MODEL),
            *group(p_states, kv_p, bp, tp), *group(s_states, kv_s, bs, ts))
```

```python
import functools
import math

import jax
import jax.numpy as jnp
import numpy as np
from jax import lax
from jax.experimental import pallas as pl
from jax.experimental.pallas import tpu as pltpu

F32 = jnp.float32
BF16 = jnp.bfloat16

D_MODEL = 2048
GROUP_W = 512
CHUNK = 64
SUB = 16
SSM_HEADS = 8
SSM_HEAD_DIM = 64
SSM_GROUPS = 2
SSM_STATE = 128
SSM_CONV = 4
SSM_CONV_DIM = 1024
HGRN_HEADS = 4
HGRN_HEAD_DIM = 128
HGRN_F_FLOOR = 1e-30
FOX_HEADS = 8
FOX_HEAD_DIM = 64
LRU_BLOCKS = 4
LRU_BLOCK_W = 128
LRU_CONV = 4
LRU_C = 8.0
EPS = 1e-6
NEG = -1e30

LANES = 128
CONV_PAD = 8
VMEM_LIMIT = 56 * 1024 * 1024

COL_A_Z, COL_A_X, COL_A_BC, COL_B_Q, COL_B_F, COL_B_I, COL_B_G = 0, 1, 2, 3, 4, 5, 6
COL_C_Q, COL_C_K, COL_C_V, COL_C_G, COL_D_X, COL_D_G = 7, 8, 9, 10, 11, 12
N_MAIN_BLOCKS = 13


def _cparams(sem):
    return pltpu.CompilerParams(dimension_semantics=sem, vmem_limit_bytes=VMEM_LIMIT)


def _dot(a, b):
    return jnp.dot(a, b, preferred_element_type=F32)


def _dot_nt(a, b):
    return lax.dot_general(a, b, (((1,), (1,)), ((), ())), preferred_element_type=F32)


def _dot_tn(a, b):
    return lax.dot_general(a, b, (((0,), (0,)), ((), ())), preferred_element_type=F32)


def _split3(x):
    hi = x.astype(BF16)
    r = x - hi.astype(F32)
    mid = r.astype(BF16)
    lo = (r - mid.astype(F32)).astype(BF16)
    return hi, mid, lo


def _dot01(m01, x):
    hi, mid, lo = _split3(x)
    return _dot(m01, hi) + _dot(m01, mid) + _dot(m01, lo)


def _dot01_r(x, m01):
    hi, mid, lo = _split3(x)
    return _dot(hi, m01) + _dot(mid, m01) + _dot(lo, m01)


def _tri(n, lower=True):
    r = lax.broadcasted_iota(jnp.int32, (n, n), 0)
    c = lax.broadcasted_iota(jnp.int32, (n, n), 1)
    return (r >= c) if lower else (r <= c)


def _sigmoid(x):
    return jax.nn.sigmoid(x)


def _silu(x):
    return x * jax.nn.sigmoid(x)


def _softplus(x):
    return jnp.maximum(x, 0.0) + jnp.log1p(jnp.exp(-jnp.abs(x)))


def _log_sigmoid(x):
    return -_softplus(-x)


def _in_proj_kernel(x_ref, g_ref, w_ref, ws_ref, o_ref, os_ref, h_scr):
    @pl.when(pl.program_id(1) == 0)
    def _():
        x = x_ref[...]
        ms = jnp.mean(x * x, axis=-1, keepdims=True)
        h = (x * lax.rsqrt(ms + EPS) * g_ref[...]).astype(BF16)
        h_scr[...] = h
        os_ref[...] = _dot_nt(h, ws_ref[...])

    o_ref[...] = _dot_nt(h_scr[...], w_ref[...])


def _in_proj(x2d, ln, w_main, w_small, layer, tm):
    m = x2d.shape[0]
    n_main, n_small = w_main.shape[1], w_small.shape[1]
    return pl.pallas_call(
        _in_proj_kernel,
        grid=(m // tm, n_main // GROUP_W),
        in_specs=[
            pl.BlockSpec((tm, D_MODEL), lambda i, j: (i, 0)),
            pl.BlockSpec((1, D_MODEL), lambda i, j: (0, 0)),
            pl.BlockSpec((None, GROUP_W, D_MODEL), lambda i, j: (layer, j, 0)),
            pl.BlockSpec((None, n_small, D_MODEL), lambda i, j: (layer, 0, 0)),
        ],
        out_specs=[
            pl.BlockSpec((tm, GROUP_W), lambda i, j: (i, j)),
            pl.BlockSpec((tm, n_small), lambda i, j: (i, 0)),
        ],
        out_shape=[jax.ShapeDtypeStruct((m, n_main), F32), jax.ShapeDtypeStruct((m, n_small), F32)],
        scratch_shapes=[pltpu.VMEM((tm, D_MODEL), BF16)],
        compiler_params=_cparams(("parallel", "arbitrary")),
        name="in_proj",
    )(x2d, ln, w_main, w_small)


def _conv_block(xp_scr, cw_ref, cb_ref, tb, kw):
    first = CONV_PAD - (kw - 1)
    y = cb_ref[...]
    for j in range(kw):
        y = y + cw_ref[j:j + 1, :] * xp_scr[pl.ds(first + j, tb), :]
    tail = xp_scr[pl.ds(first + tb, kw - 1), :]
    xp_scr[pl.ds(first, kw - 1), :] = tail
    return y, tail


def _ssd_kernel(z_ref, xlo_ref, xhi_ref, sm_ref, conv0_ref, s0_ref, cw_ref, cb_ref, dtb_ref, alog_ref,
                dexp_ref, nrm_ref, y_ref, convo_ref, so_ref,
                xp_scr, xbc_scr, g_scr, dt_scr, y_scr, s_scr, *, tb):
    c = pl.program_id(1)
    last = pl.num_programs(1) - 1
    hpg = SSM_HEADS // SSM_GROUPS
    gw = hpg * SSM_HEAD_DIM

    @pl.when(c == 0)
    def _():
        s_scr[...] = s0_ref[...]
        xp_scr[0:CONV_PAD, :] = jnp.zeros((CONV_PAD, SSM_CONV_DIM), F32)
        xp_scr[CONV_PAD - (SSM_CONV - 1):CONV_PAD, :] = conv0_ref[...]

    xp_scr[CONV_PAD:CONV_PAD + tb, 0:GROUP_W] = xlo_ref[...]
    xp_scr[CONV_PAD:CONV_PAD + tb, GROUP_W:SSM_CONV_DIM] = xhi_ref[...]
    conv, tail = _conv_block(xp_scr, cw_ref, cb_ref, tb, SSM_CONV)

    @pl.when(c == last)
    def _():
        convo_ref[...] = tail

    xbc_scr[...] = _silu(conv)
    dt = _softplus(sm_ref[...] + dtb_ref[...])
    dt_scr[...] = dt
    g_scr[...] = dt * (-jnp.exp(alog_ref[...]))

    tri_mask = _tri(CHUNK)
    tri_l = tri_mask.astype(BF16)
    tri_u = _tri(CHUNK, lower=False).astype(BF16)
    b_off = GROUP_W
    c_off = GROUP_W + SSM_GROUPS * SSM_STATE

    def chunk(ci, carry):
        rows = pl.ds(pl.multiple_of(ci * CHUNK, CHUNK), CHUNK)
        gc = g_scr[rows, :]
        dtc = dt_scr[rows, :]
        cum = _dot01(tri_l, gc)
        cum_t = _dot01_r(gc.T, tri_u)
        tot = cum[CHUNK - 1:CHUNK, :]
        ys = []
        for grp in range(SSM_GROUPS):
            bm = xbc_scr[rows, b_off + grp * SSM_STATE:b_off + (grp + 1) * SSM_STATE].astype(BF16)
            cm = xbc_scr[rows, c_off + grp * SSM_STATE:c_off + (grp + 1) * SSM_STATE].astype(BF16)
            gmat = _dot_nt(cm, bm)
            s_g = s_scr[grp]
            cs = _dot(cm, s_g.astype(BF16))
            vdec, etot = [], []
            for hh in range(hpg):
                h = grp * hpg + hh
                col = cum[:, h:h + 1]
                row = cum_t[h:h + 1, :]
                dec = jnp.where(tri_mask, jnp.exp(jnp.minimum(col - row, 0.0)), 0.0)
                att = (gmat * dec).astype(BF16)
                xh = xbc_scr[rows, h * SSM_HEAD_DIM:(h + 1) * SSM_HEAD_DIM]
                vh = xh * dtc[:, h:h + 1]
                o = _dot(att, vh.astype(BF16)) + jnp.exp(col) * cs[:, hh * SSM_HEAD_DIM:(hh + 1) * SSM_HEAD_DIM]
                ys.append(o + dexp_ref[:, h * SSM_HEAD_DIM:(h + 1) * SSM_HEAD_DIM] * xh)
                toth = tot[:, h:h + 1]
                vdec.append(vh * jnp.exp(toth - col))
                etot.append(jnp.broadcast_to(jnp.exp(toth), (1, SSM_HEAD_DIM)))
            vdec = jnp.concatenate(vdec, axis=1).astype(BF16)
            etot = jnp.concatenate(etot, axis=1)
            s_scr[grp] = etot * s_g + _dot_tn(bm, vdec)
        y_scr[rows, :] = jnp.concatenate(ys, axis=1)
        return carry

    lax.fori_loop(0, tb // CHUNK, chunk, 0)

    y = y_scr[...] * _silu(z_ref[...])
    outs = []
    for grp in range(SSM_GROUPS):
        yg = y[:, grp * gw:(grp + 1) * gw]
        ms = jnp.mean(yg * yg, axis=-1, keepdims=True)
        outs.append(yg * lax.rsqrt(ms + EPS))
    y_ref[...] = (jnp.concatenate(outs, axis=1) * nrm_ref[...]).astype(y_ref.dtype)

    @pl.when(c == last)
    def _():
        so_ref[...] = s_scr[...]


def _ssd(proj, small, conv0, s0, cw, cb, dtb, alog, dexp, nrm, nb, t, tb):
    nc = t // tb
    hpg = SSM_HEADS // SSM_GROUPS
    sshape = (SSM_GROUPS, SSM_STATE, hpg * SSM_HEAD_DIM)
    row = lambda col: pl.BlockSpec((tb, GROUP_W), lambda b, c: (b * nc + c, col))
    full = lambda a: pl.BlockSpec(a.shape, lambda b, c: (0,) * a.ndim)
    return pl.pallas_call(
        functools.partial(_ssd_kernel, tb=tb),
        grid=(nb, nc),
        in_specs=[row(COL_A_Z), row(COL_A_X), row(COL_A_BC),
                  pl.BlockSpec((tb, LANES), lambda b, c: (b * nc + c, 0)),
                  pl.BlockSpec((None, SSM_CONV - 1, SSM_CONV_DIM), lambda b, c: (b, 0, 0)),
                  pl.BlockSpec((None,) + sshape, lambda b, c: (b, 0, 0, 0)),
                  full(cw), full(cb), full(dtb), full(alog), full(dexp), full(nrm)],
        out_specs=[pl.BlockSpec((tb, GROUP_W), lambda b, c: (b * nc + c, 0)),
                   pl.BlockSpec((None, SSM_CONV - 1, SSM_CONV_DIM), lambda b, c: (b, 0, 0)),
                   pl.BlockSpec((None,) + sshape, lambda b, c: (b, 0, 0, 0))],
        out_shape=[jax.ShapeDtypeStruct((nb * t, GROUP_W), BF16),
                   jax.ShapeDtypeStruct((nb, SSM_CONV - 1, SSM_CONV_DIM), F32),
                   jax.ShapeDtypeStruct((nb,) + sshape, F32)],
        scratch_shapes=[pltpu.VMEM((CONV_PAD + tb, SSM_CONV_DIM), F32),
                        pltpu.VMEM((tb, SSM_CONV_DIM), F32),
                        pltpu.VMEM((tb, LANES), F32),
                        pltpu.VMEM((tb, LANES), F32),
                        pltpu.VMEM((tb, GROUP_W), F32),
                        pltpu.VMEM(sshape, F32)],
        compiler_params=_cparams(("parallel", "arbitrary")),
        name="ssd",
    )(proj, proj, proj, small, conv0, s0, cw, cb, dtb, alog, dexp, nrm)


def _hgrn_kernel(q_ref, f_ref, i_ref, g_ref, s0_ref, lbl_ref, nrm_ref, y_ref, so_ref,
                 c_scr, k_scr, st_scr, *, tb, layer):
    c = pl.program_id(1)
    last = pl.num_programs(1) - 1
    hd = HGRN_HEAD_DIM
    nsub = CHUNK // SUB

    @pl.when(c == 0)
    def _():
        for h in range(HGRN_HEADS):
            st_scr[h] = s0_ref[h].T

    ll = lbl_ref[...]
    e = jnp.exp(ll - jnp.max(ll, axis=0, keepdims=True))
    p = e / jnp.sum(e, axis=0, keepdims=True)
    cs = p[0:1, :]
    for l in range(1, layer + 1):
        cs = cs + p[l:l + 1, :]
    lb = cs - p[0:1, :]

    tri_l = _tri(CHUNK).astype(BF16)
    rowid = lax.broadcasted_iota(jnp.int32, (CHUNK, 1), 0)
    subid = lax.broadcasted_iota(jnp.int32, (SUB, 1), 0)

    def chunk(ci, carry):
        r0 = pl.multiple_of(ci * CHUNK, CHUNK)
        rows = pl.ds(r0, CHUNK)
        fr = f_ref[rows, :]
        fg = lb + (1.0 - lb) * _sigmoid(fr)
        logf = jnp.log(jnp.maximum(fg, HGRN_F_FLOOR))
        hk = (1.0 - lb) * _sigmoid(-fr)
        hq = _silu(q_ref[rows, :])
        v = i_ref[rows, :]
        vb = v.astype(BF16)
        cum = _dot01(tri_l, logf)
        tot = cum[CHUNK - 1:CHUNK, :]
        c_scr[...] = cum
        k_scr[...] = hk
        qe = (hq * jnp.exp(cum)).astype(BF16)
        kd = (hk * jnp.exp(tot - cum)).astype(BF16)
        etot = jnp.exp(tot)

        qcat, kcat = [], []
        for j in range(nsub - 1):
            ej = cum[(j + 1) * SUB - 1:(j + 1) * SUB, :]
            qcat.append(jnp.where(rowid >= (j + 1) * SUB, hq * jnp.exp(jnp.minimum(cum - ej, 0.0)), 0.0).astype(BF16))
            in_j = (rowid >= j * SUB) & (rowid < (j + 1) * SUB)
            kcat.append(jnp.where(in_j, hk * jnp.exp(jnp.minimum(ej - cum, 0.0)), 0.0).astype(BF16))

        o_heads = []
        for h in range(HGRN_HEADS):
            hs = slice(h * hd, (h + 1) * hd)
            qc = jnp.concatenate([q[:, hs] for q in qcat], axis=1)
            kc = jnp.concatenate([k[:, hs] for k in kcat], axis=1)
            a_off = _dot_nt(qc, kc)
            st = st_scr[h]
            o_heads.append(_dot(a_off.astype(BF16), vb[:, hs]) + _dot_nt(qe[:, hs], st.astype(BF16)))
            st_scr[h] = st * etot[:, hs] + _dot_tn(vb[:, hs], kd[:, hs])
        o = jnp.concatenate(o_heads, axis=1)

        o_sub = []
        for i in range(nsub):
            base = i * SUB
            ci_ = cum[base:base + SUB, :]
            qi = hq[base:base + SUB, :]
            acc = [jnp.zeros((SUB, hd), F32) for _ in range(HGRN_HEADS)]
            for j in range(SUB):
                crow = c_scr[pl.ds(base + j, 1), :]
                krow = k_scr[pl.ds(base + j, 1), :]
                vrow = i_ref[pl.ds(r0 + base + j, 1), :]
                x = jnp.where(subid >= j, qi * jnp.exp(jnp.minimum(ci_ - crow, 0.0)) * krow, 0.0)
                for h in range(HGRN_HEADS):
                    hs = slice(h * hd, (h + 1) * hd)
                    a = jnp.sum(x[:, hs], axis=-1, keepdims=True)
                    acc[h] = acc[h] + a * vrow[:, hs]
            o_sub.append(jnp.concatenate(acc, axis=1))
        o = o + jnp.concatenate(o_sub, axis=0)

        outs = []
        for h in range(HGRN_HEADS):
            oh = o[:, h * hd:(h + 1) * hd]
            ms = jnp.mean(oh * oh, axis=-1, keepdims=True)
            outs.append(oh * lax.rsqrt(ms + EPS))
        y = jnp.concatenate(outs, axis=1) * nrm_ref[...] * _silu(g_ref[rows, :])
        y_ref[rows, :] = y.astype(y_ref.dtype)
        return carry

    lax.fori_loop(0, tb // CHUNK, chunk, 0)

    @pl.when(c == last)
    def _():
        for h in range(HGRN_HEADS):
            so_ref[h] = st_scr[h].T


def _hgrn(proj, s0, lbl, nrm, nb, t, tb, layer):
    nc = t // tb
    sshape = (HGRN_HEADS, HGRN_HEAD_DIM, HGRN_HEAD_DIM)
    row = lambda col: pl.BlockSpec((tb, GROUP_W), lambda b, c: (b * nc + c, col))
    full = lambda a: pl.BlockSpec(a.shape, lambda b, c: (0,) * a.ndim)
    return pl.pallas_call(
        functools.partial(_hgrn_kernel, tb=tb, layer=layer),
        grid=(nb, nc),
        in_specs=[row(COL_B_Q), row(COL_B_F), row(COL_B_I), row(COL_B_G),
                  pl.BlockSpec((None,) + sshape, lambda b, c: (b, 0, 0, 0)),
                  full(lbl), full(nrm)],
        out_specs=[pl.BlockSpec((tb, GROUP_W), lambda b, c: (b * nc + c, 0)),
                   pl.BlockSpec((None,) + sshape, lambda b, c: (b, 0, 0, 0))],
        out_shape=[jax.ShapeDtypeStruct((nb * t, GROUP_W), BF16),
                   jax.ShapeDtypeStruct((nb,) + sshape, F32)],
        scratch_shapes=[pltpu.VMEM((CHUNK, GROUP_W), F32),
                        pltpu.VMEM((CHUNK, GROUP_W), F32),
                        pltpu.VMEM(sshape, F32)],
        compiler_params=_cparams(("parallel", "arbitrary")),
        name="hgrn2",
    )(proj, proj, proj, proj, s0, lbl, nrm)


def _lru_kernel(x_ref, g_ref, conv0_ref, h0_ref, cw_ref, cb_ref, wr_ref, br_ref, wi_ref, bi_ref, lam_ref,
                y_ref, convo_ref, ho_ref, xp_scr, h_scr, *, tb, reset_first):
    c = pl.program_id(1)
    last = pl.num_programs(1) - 1

    @pl.when(c == 0)
    def _():
        h_scr[...] = h0_ref[...]
        xp_scr[0:CONV_PAD, :] = jnp.zeros((CONV_PAD, GROUP_W), F32)
        xp_scr[CONV_PAD - (LRU_CONV - 1):CONV_PAD, :] = conv0_ref[...]

    xp_scr[CONV_PAD:CONV_PAD + tb, :] = x_ref[...]
    xc, tail = _conv_block(xp_scr, cw_ref, cb_ref, tb, LRU_CONV)

    @pl.when(c == last)
    def _():
        convo_ref[...] = tail

    xcb = xc.astype(BF16)

    def blockdiag(w_ref):
        return jnp.concatenate(
            [_dot(xcb[:, n * LRU_BLOCK_W:(n + 1) * LRU_BLOCK_W], w_ref[n]) for n in range(LRU_BLOCKS)], axis=1)

    r = _sigmoid(blockdiag(wr_ref) + br_ref[...])
    gi = _sigmoid(blockdiag(wi_ref) + bi_ref[...])
    log_a = LRU_C * r * _log_sigmoid(lam_ref[...])
    a = jnp.exp(log_a)
    mult = jnp.sqrt(-jnp.tanh(log_a) * (a * a + 1.0))
    rowid = lax.broadcasted_iota(jnp.int32, (tb, 1), 0)
    first_row = rowid == 0
    if reset_first:
        rst = first_row & (c == 0)
        a = jnp.where(rst, 0.0, a)
        mult = jnp.where(rst, 1.0, mult)
    u = mult * gi * xc
    u = u + jnp.where(first_row, a * h_scr[...], 0.0)

    av, bv = a, u
    d = 1
    while d < tb:
        m = rowid >= d
        a_sh = pltpu.roll(av, d, axis=0)
        b_sh = pltpu.roll(bv, d, axis=0)
        bv = jnp.where(m, av * b_sh + bv, bv)
        av = jnp.where(m, av * a_sh, av)
        d *= 2
    h_new = bv[tb - 1:tb, :]
    h_scr[...] = h_new
    y_ref[...] = (bv * jax.nn.gelu(g_ref[...])).astype(y_ref.dtype)

    @pl.when(c == last)
    def _():
        ho_ref[...] = h_new


def _lru(proj, conv0, h0, cw, cb, wr, br, wi, bi, lam, nb, t, tb, reset_first):
    nc = t // tb
    row = lambda col: pl.BlockSpec((tb, GROUP_W), lambda b, c: (b * nc + c, col))
    full = lambda a: pl.BlockSpec(a.shape, lambda b, c: (0,) * a.ndim)
    return pl.pallas_call(
        functools.partial(_lru_kernel, tb=tb, reset_first=reset_first),
        grid=(nb, nc),
        in_specs=[row(COL_D_X), row(COL_D_G),
                  pl.BlockSpec((None, LRU_CONV - 1, GROUP_W), lambda b, c: (b, 0, 0)),
                  pl.BlockSpec((None, 1, GROUP_W), lambda b, c: (b, 0, 0)),
                  full(cw), full(cb), full(wr), full(br), full(wi), full(bi), full(lam)],
        out_specs=[pl.BlockSpec((tb, GROUP_W), lambda b, c: (b * nc + c, 0)),
                   pl.BlockSpec((None, LRU_CONV - 1, GROUP_W), lambda b, c: (b, 0, 0)),
                   pl.BlockSpec((None, 1, GROUP_W), lambda b, c: (b, 0, 0))],
        out_shape=[jax.ShapeDtypeStruct((nb * t, GROUP_W), BF16),
                   jax.ShapeDtypeStruct((nb, LRU_CONV - 1, GROUP_W), F32),
                   jax.ShapeDtypeStruct((nb, 1, GROUP_W), F32)],
        scratch_shapes=[pltpu.VMEM((CONV_PAD + tb, GROUP_W), F32),
                        pltpu.VMEM((1, GROUP_W), F32)],
        compiler_params=_cparams(("parallel", "arbitrary")),
        name="rglru",
    )(proj, proj, conv0, h0, cw, cb, wr, br, wi, bi, lam)


def _gate_kernel(*refs, tb, past):
    if past:
        sm_ref, fb_ref, plf_ref, lf_ref, cum_ref, cump_ref, carry_scr = refs
    else:
        sm_ref, fb_ref, lf_ref, cum_ref, carry_scr = refs

    @pl.when(pl.program_id(1) == 0)
    def _():
        if past:
            cp = _dot01(_tri(past).astype(BF16), plf_ref[...])
            cump_ref[...] = cp
            carry_scr[...] = cp[past - 1:past, :]
        else:
            carry_scr[...] = jnp.zeros_like(carry_scr)

    lf = _log_sigmoid(sm_ref[...] + fb_ref[...])
    lf_ref[...] = lf
    cum = _dot01(_tri(tb).astype(BF16), lf) + carry_scr[...]
    cum_ref[...] = cum
    carry_scr[...] = cum[tb - 1:tb, :]


def _gate(small, fb, past_lf, nb, t, tb):
    nc = t // tb
    past = 0 if past_lf is None else past_lf.shape[1]
    rows = pl.BlockSpec((tb, LANES), lambda b, c: (b * nc + c, 0))
    in_specs = [pl.BlockSpec((tb, LANES), lambda b, c: (b * nc + c, 1)),
                pl.BlockSpec(fb.shape, lambda b, c: (0, 0))]
    out_specs = [rows, rows]
    out_shape = [jax.ShapeDtypeStruct((nb * t, LANES), F32)] * 2
    args = [small, fb]
    if past:
        in_specs.append(pl.BlockSpec((None, past, LANES), lambda b, c: (b, 0, 0)))
        out_specs.append(pl.BlockSpec((None, past, LANES), lambda b, c: (b, 0, 0)))
        out_shape.append(jax.ShapeDtypeStruct((nb, past, LANES), F32))
        args.append(past_lf)
    return pl.pallas_call(
        functools.partial(_gate_kernel, tb=tb, past=past),
        grid=(nb, nc),
        in_specs=in_specs, out_specs=out_specs, out_shape=out_shape,
        scratch_shapes=[pltpu.VMEM((1, LANES), F32)],
        compiler_params=_cparams(("parallel", "arbitrary")),
        name="fox_gate",
    )(*args)


def _head_norm(x, gain, hsum):
    ms = _dot01_r(x * x, hsum) * (1.0 / FOX_HEAD_DIM)
    return x * lax.rsqrt(ms + EPS) * gain


def _store_heads(ref, x):
    xt = x.T
    for h in range(FOX_HEADS):
        ref[h] = xt[h * FOX_HEAD_DIM:(h + 1) * FOX_HEAD_DIM, :]


def _foxprep_kernel(kbuf_ref, vbuf_ref, q_ref, k_ref, v_ref, hsum_ref, qn_ref, kn_ref,
                    fq_ref, fk_ref, fv_ref, kb_ref, vb_ref):
    del kbuf_ref, vbuf_ref
    hsum = hsum_ref[...]
    fq_ref[...] = (_head_norm(q_ref[...], qn_ref[...], hsum) * (FOX_HEAD_DIM ** -0.5)).astype(BF16)
    fk = _head_norm(k_ref[...], kn_ref[...], hsum)
    _store_heads(fk_ref, fk)
    kb_ref[...] = fk.astype(BF16)
    v = v_ref[...]
    _store_heads(fv_ref, v)
    vb_ref[...] = v.astype(BF16)


def _heads_spec(tb, t, first_stream):
    nc = t // tb
    return pl.BlockSpec((None, FOX_HEADS, FOX_HEAD_DIM, tb), lambda i: (first_stream + i // nc, 0, 0, i % nc))


_ANY = pl.BlockSpec(memory_space=pl.ANY)


def _foxprep(kv, layer, proj, hsum, qn, kn, t, tb):
    m = proj.shape[0]
    row = lambda col: pl.BlockSpec((tb, GROUP_W), lambda i: (i, col))
    full = lambda a: pl.BlockSpec(a.shape, lambda i: (0,) * a.ndim)
    out_row = pl.BlockSpec((tb, GROUP_W), lambda i: (i, 0))
    heads = _heads_spec(tb, t, layer * (m // t))
    return pl.pallas_call(
        _foxprep_kernel,
        grid=(m // tb,),
        in_specs=[_ANY, _ANY, row(COL_C_Q), row(COL_C_K), row(COL_C_V), full(hsum), full(qn), full(kn)],
        out_specs=[out_row, heads, heads, out_row, out_row],
        out_shape=[jax.ShapeDtypeStruct((m, GROUP_W), BF16),
                   jax.ShapeDtypeStruct(kv[0].shape, F32),
                   jax.ShapeDtypeStruct(kv[1].shape, F32),
                   jax.ShapeDtypeStruct((m, GROUP_W), BF16),
                   jax.ShapeDtypeStruct((m, GROUP_W), BF16)],
        input_output_aliases={0: 1, 1: 2},
        compiler_params=_cparams(("parallel",)),
        name="fox_prep",
    )(kv[0], kv[1], proj, proj, proj, hsum, qn, kn)


AUG_W = FOX_HEADS * LANES
N_PIECES = 3


def _aug_constants():
    pq = np.zeros((N_PIECES * LANES, AUG_W), np.float32)
    pk = np.zeros((N_PIECES * LANES, AUG_W), np.float32)
    ones_q = np.zeros((1, AUG_W), np.float32)
    ones_k = np.zeros((1, AUG_W), np.float32)
    ones_v = np.zeros((1, AUG_W), np.float32)
    head = np.zeros((1, AUG_W), np.float32)
    for h in range(FOX_HEADS):
        own = h * LANES + FOX_HEAD_DIM * (h % 2)
        other = h * LANES + FOX_HEAD_DIM * (1 - h % 2)
        head[0, own:own + FOX_HEAD_DIM] = 1.0
        for piece in range(N_PIECES):
            pq[piece * LANES + h, other + piece] = 1.0
            ones_k[0, other + piece] = 1.0
            ones_q[0, other + N_PIECES + piece] = 1.0
            pk[piece * LANES + h, other + N_PIECES + piece] = -1.0
        ones_v[0, other] = 1.0
    return dict(aug_pq=jnp.asarray(pq, BF16), aug_pk=jnp.asarray(pk, BF16), aug_ones_q=jnp.asarray(ones_q),
                aug_ones_k=jnp.asarray(ones_k), aug_ones_v=jnp.asarray(ones_v), aug_head=jnp.asarray(head))


def _foxprep_aug_kernel(kbuf_ref, vbuf_ref, q_ref, k_ref, v_ref, cum_ref, hsum_ref, qn_ref, kn_ref, pq_ref, pk_ref,
                        oq_ref, ok_ref, ov_ref, head_ref, fk_ref, fv_ref, qa_ref, ka_ref, va_ref):
    del kbuf_ref, vbuf_ref
    hsum = hsum_ref[...]
    fq = _head_norm(q_ref[...], qn_ref[...], hsum) * (FOX_HEAD_DIM ** -0.5)
    fk = _head_norm(k_ref[...], kn_ref[...], hsum)
    v = v_ref[...]
    _store_heads(fk_ref, fk)
    _store_heads(fv_ref, v)
    pieces = jnp.concatenate(_split3(cum_ref[...]), axis=1)
    own = head_ref[...] > 0.5

    def per_head(x):
        return jnp.concatenate([x[:, (h // 2) * LANES:(h // 2 + 1) * LANES] for h in range(FOX_HEADS)], axis=1)

    qa = jnp.where(own, per_head(fq), _dot(pieces, pq_ref[...]) + oq_ref[...]).astype(BF16)
    ka = jnp.where(own, per_head(fk), _dot(pieces, pk_ref[...]) + ok_ref[...]).astype(BF16)
    va = jnp.where(own, per_head(v), ov_ref[...]).astype(BF16)
    for h in range(FOX_HEADS):
        hs = slice(h * LANES, (h + 1) * LANES)
        qa_ref[h] = qa[:, hs]
        ka_ref[h] = ka[:, hs]
        va_ref[h] = va[:, hs]


def _foxprep_aug(kv, layer, proj, cum, hsum, qn, kn, aug, t, tb):
    m = proj.shape[0]
    row = lambda col: pl.BlockSpec((tb, GROUP_W), lambda i: (i, col))
    full = lambda a: pl.BlockSpec(a.shape, lambda i: (0,) * a.ndim)
    out_aug = pl.BlockSpec((FOX_HEADS, tb, LANES), lambda i: (0, i, 0))
    heads = _heads_spec(tb, t, layer * (m // t))
    consts = [aug['aug_pq'], aug['aug_pk'], aug['aug_ones_q'], aug['aug_ones_k'], aug['aug_ones_v'],
              aug['aug_head']]
    return pl.pallas_call(
        _foxprep_aug_kernel,
        grid=(m // tb,),
        in_specs=[_ANY, _ANY, row(COL_C_Q), row(COL_C_K), row(COL_C_V),
                  pl.BlockSpec((tb, LANES), lambda i: (i, 0)),
                  full(hsum), full(qn), full(kn)] + [full(a) for a in consts],
        out_specs=[heads, heads, out_aug, out_aug, out_aug],
        out_shape=[jax.ShapeDtypeStruct(kv[0].shape, F32), jax.ShapeDtypeStruct(kv[1].shape, F32)]
                  + [jax.ShapeDtypeStruct((FOX_HEADS, m, LANES), BF16)] * 3,
        input_output_aliases={0: 0, 1: 1},
        compiler_params=_cparams(("parallel",)),
        name="fox_prep_aug",
    )(kv[0], kv[1], proj, proj, proj, cum, hsum, qn, kn, *consts)


def _attn_aug_kernel(q_ref, k_ref, v_ref, g_ref, y_ref, *, tq, hg):
    qi = pl.program_id(2)
    causal = _tri(tq)
    low = lax.broadcasted_iota(jnp.int32, (1, LANES), 1) < FOX_HEAD_DIM
    qs = [q_ref[i] for i in range(hg)]

    def step(j, carry, masked):
        rows = pl.ds(pl.multiple_of(j * tq, tq), tq)
        new = []
        for i in range(hg):
            m, acc = carry[i]
            s = _dot_nt(qs[i], k_ref[i, rows, :])
            if masked:
                s = jnp.where(causal, s, NEG)
            m_new = jnp.maximum(m, jnp.max(s, axis=-1, keepdims=True))
            p = jnp.exp(s - m_new).astype(BF16)
            acc = jnp.exp(m - m_new) * acc + _dot(p, v_ref[i, rows, :])
            new.append((m_new, acc))
        return tuple(new)

    init = tuple((jnp.full((tq, 1), NEG, F32), jnp.zeros((tq, LANES), F32)) for _ in range(hg))
    carry = lax.fori_loop(0, qi, lambda j, c: step(j, c, False), init)
    carry = step(qi, carry, True)
    outs = []
    for i in range(0, hg, 2):
        acc_e, acc_o = carry[i][1], carry[i + 1][1]
        out_e = acc_e / acc_e[:, FOX_HEAD_DIM:FOX_HEAD_DIM + 1]
        out_o = acc_o / acc_o[:, 0:1]
        outs.append(jnp.where(low, out_e, out_o))
    y = jnp.concatenate(outs, axis=1) * _sigmoid(g_ref[...])
    y_ref[...] = y.astype(y_ref.dtype)


def _attention_aug(qa, ka, va, proj, nb, t, tq, hg):
    nq = t // tq
    ngrp = FOX_HEADS // hg
    wout = hg * FOX_HEAD_DIM
    gate_col0 = COL_C_G * GROUP_W // wout
    return pl.pallas_call(
        functools.partial(_attn_aug_kernel, tq=tq, hg=hg),
        grid=(nb, ngrp, nq),
        in_specs=[pl.BlockSpec((hg, tq, LANES), lambda b, g, i: (g, b * nq + i, 0)),
                  pl.BlockSpec((hg, t, LANES), lambda b, g, i: (g, b, 0)),
                  pl.BlockSpec((hg, t, LANES), lambda b, g, i: (g, b, 0)),
                  pl.BlockSpec((tq, wout), lambda b, g, i: (b * nq + i, gate_col0 + g))],
        out_specs=pl.BlockSpec((tq, wout), lambda b, g, i: (b * nq + i, g)),
        out_shape=jax.ShapeDtypeStruct((nb * t, GROUP_W), BF16),
        compiler_params=_cparams(("parallel", "parallel", "arbitrary")),
        name="fox_attention_aug",
    )(qa, ka, va, proj)


def _attn_cached_kernel(q_ref, cq_ref, ck_ref, k_ref, v_ref, g_ref, pk_ref, pv_ref, y_ref, *, t, past):
    causal = _tri(t)
    outs = []
    for h in range(FOX_HEADS):
        hs = slice(h * FOX_HEAD_DIM, (h + 1) * FOX_HEAD_DIM)
        qh = q_ref[:, hs]
        pk = pk_ref[h].astype(BF16)
        pv = pv_ref[h].astype(BF16)
        cqh = cq_ref[:, h:h + 1]
        s_p = _dot(qh, pk) + cqh - ck_ref[h:h + 1, 0:past]
        s_c = jnp.where(causal, _dot_nt(qh, k_ref[:, hs]) + cqh - ck_ref[h:h + 1, past:past + t], NEG)
        m = jnp.maximum(jnp.max(s_p, axis=-1, keepdims=True), jnp.max(s_c, axis=-1, keepdims=True))
        p_p = jnp.exp(s_p - m)
        p_c = jnp.exp(s_c - m)
        l = jnp.sum(p_p, axis=-1, keepdims=True) + jnp.sum(p_c, axis=-1, keepdims=True)
        acc = _dot_nt(p_p.astype(BF16), pv) + _dot(p_c.astype(BF16), v_ref[:, hs])
        outs.append(acc / l)
    y = jnp.concatenate(outs, axis=1) * _sigmoid(g_ref[...])
    y_ref[...] = y.astype(y_ref.dtype)


def _attention_cached(fq, cq, ck_t, kb, vb, proj, past_k, past_v, layer, nb, t):
    past = past_k.shape[-1]
    ltot = ck_t.shape[-1]
    seq = lambda rows: pl.BlockSpec((None, rows, GROUP_W), lambda b: (b, 0, 0))
    cache = pl.BlockSpec((None, FOX_HEADS, FOX_HEAD_DIM, past), lambda b: (layer * nb + b, 0, 0, 0))
    return pl.pallas_call(
        functools.partial(_attn_cached_kernel, t=t, past=past),
        grid=(nb,),
        in_specs=[pl.BlockSpec((t, GROUP_W), lambda b: (b, 0)),
                  pl.BlockSpec((None, t, LANES), lambda b: (b, 0, 0)),
                  pl.BlockSpec((None, FOX_HEADS, ltot), lambda b: (b, 0, 0)),
                  seq(t), seq(t),
                  pl.BlockSpec((t, GROUP_W), lambda b: (b, COL_C_G)),
                  cache, cache],
        out_specs=pl.BlockSpec((t, GROUP_W), lambda b: (b, 0)),
        out_shape=jax.ShapeDtypeStruct((nb * t, GROUP_W), BF16),
        compiler_params=_cparams(("parallel",)),
        name="fox_attention_cached",
    )(fq, cq, ck_t, kb.reshape(nb, t, GROUP_W), vb.reshape(nb, t, GROUP_W), proj, past_k, past_v)


def _out_proj_kernel(x_ref, ya_ref, yb_ref, yc_ref, yd_ref, w_ref, o_ref):
    acc = x_ref[...]
    for n, y_ref in enumerate((ya_ref, yb_ref, yc_ref, yd_ref)):
        acc = acc + _dot(y_ref[...], w_ref[n * GROUP_W:(n + 1) * GROUP_W, :])
    o_ref[...] = acc


def _out_proj(x2d, ys, w_out, layer, tm):
    m = x2d.shape[0]
    yspec = pl.BlockSpec((tm, GROUP_W), lambda i: (i, 0))
    return pl.pallas_call(
        _out_proj_kernel,
        grid=(m // tm,),
        in_specs=[pl.BlockSpec((tm, D_MODEL), lambda i: (i, 0)), yspec, yspec, yspec, yspec,
                  pl.BlockSpec((None,) + w_out.shape[1:], lambda i: (layer, 0, 0))],
        out_specs=pl.BlockSpec((tm, D_MODEL), lambda i: (i, 0)),
        out_shape=jax.ShapeDtypeStruct((m, D_MODEL), F32),
        compiler_params=_cparams(("parallel",)),
        name="out_proj",
    )(x2d, *ys, w_out)


def _ffn_kernel(x_ref, g_ref, wg_ref, wu_ref, wd_ref, o_ref, h_scr, acc_scr):
    f = pl.program_id(1)

    @pl.when(f == 0)
    def _():
        x = x_ref[...]
        ms = jnp.mean(x * x, axis=-1, keepdims=True)
        h_scr[...] = (x * lax.rsqrt(ms + EPS) * g_ref[...]).astype(BF16)
        acc_scr[...] = x

    h = h_scr[...]
    a = _silu(_dot(h, wg_ref[...])) * _dot(h, wu_ref[...])
    acc_scr[...] += _dot(a.astype(BF16), wd_ref[...])

    @pl.when(f == pl.num_programs(1) - 1)
    def _():
        o_ref[...] = acc_scr[...]


def _ffn(x2d, ln, wg, wu, wd, layer, tm, tf):
    m = x2d.shape[0]
    d_ff = wg.shape[2]
    return pl.pallas_call(
        _ffn_kernel,
        grid=(m // tm, d_ff // tf),
        in_specs=[pl.BlockSpec((tm, D_MODEL), lambda i, f: (i, 0)),
                  pl.BlockSpec((1, D_MODEL), lambda i, f: (0, 0)),
                  pl.BlockSpec((None, D_MODEL, tf), lambda i, f: (layer, 0, f)),
                  pl.BlockSpec((None, D_MODEL, tf), lambda i, f: (layer, 0, f)),
                  pl.BlockSpec((None, tf, D_MODEL), lambda i, f: (layer, f, 0))],
        out_specs=pl.BlockSpec((tm, D_MODEL), lambda i, f: (i, 0)),
        out_shape=jax.ShapeDtypeStruct((m, D_MODEL), F32),
        scratch_shapes=[pltpu.VMEM((tm, D_MODEL), BF16), pltpu.VMEM((tm, D_MODEL), F32)],
        compiler_params=_cparams(("parallel", "arbitrary")),
        name="swiglu_ffn",
    )(x2d, ln, wg, wu, wd)


def _pad_lanes(v, offset=0):
    return jnp.zeros((1, LANES), F32).at[0, offset:offset + v.shape[0]].set(v.astype(F32))


def _ssm_state_to_groups(s):
    nb = s.shape[0]
    hpg = SSM_HEADS // SSM_GROUPS
    s = s.reshape(nb, SSM_GROUPS, hpg, SSM_STATE, SSM_HEAD_DIM)
    return jnp.transpose(s, (0, 1, 3, 2, 4)).reshape(nb, SSM_GROUPS, SSM_STATE, hpg * SSM_HEAD_DIM)


def _ssm_state_from_groups(s):
    nb = s.shape[0]
    hpg = SSM_HEADS // SSM_GROUPS
    s = s.reshape(nb, SSM_GROUPS, SSM_STATE, hpg, SSM_HEAD_DIM)
    return jnp.transpose(s, (0, 1, 3, 2, 4)).reshape(nb, SSM_HEADS, SSM_STATE, SSM_HEAD_DIM)


def _block_rows(t, want):
    return want if t % want == 0 else t


def _layer(x2d, nb, t, state, pr, layer, kv):
    (ssm_conv, ssm_s, hgrn_s, fox_k, fox_v, fox_lf, lru_conv, lru_h) = state
    m = nb * t
    past = 0 if fox_k is None else fox_k.shape[2]
    tm = _block_rows(m, 512)
    tb = _block_rows(t, 256)

    proj, small = _in_proj(x2d, pr['ln1'], pr['w_main'], pr['w_small'], layer, _block_rows(m, 1024))

    ya, ssm_conv_new, ssm_g = _ssd(proj, small, ssm_conv, _ssm_state_to_groups(ssm_s), pr['ssm_conv_w'],
                                   pr['ssm_conv_b'], pr['ssm_dt_bias'], pr['ssm_a_log'], pr['ssm_d'],
                                   pr['ssm_norm'], nb, t, tb)
    ssm_s_new = _ssm_state_from_groups(ssm_g)

    yb, hgrn_s_new = _hgrn(proj, hgrn_s, pr['hgrn_lb_logits'], pr['hgrn_norm'], nb, t, tb, layer)

    yd, lru_conv_new, lru_h_new = _lru(proj, lru_conv, lru_h.reshape(nb, 1, GROUP_W), pr['lru_conv_w'],
                                       pr['lru_conv_b'], pr['lru_w_r'], pr['lru_b_r'], pr['lru_w_i'],
                                       pr['lru_b_i'], pr['lru_lambda'], nb, t, tb, reset_first=(past == 0))

    if past:
        past_lf = jnp.pad(fox_lf.astype(F32), ((0, 0), (0, 0), (0, LANES - FOX_HEADS)))
        lf, cum, cum_past = _gate(small, pr['fox_f_bias'], past_lf, nb, t, t)
        fq, fk, fv, kb, vb = _foxprep(kv, layer, proj, pr['head_sum'], pr['fox_q_norm'], pr['fox_k_norm'], t,
                                      min(t, tm))
        cq = cum.reshape(nb, t, LANES)
        ck_t = jnp.transpose(jnp.concatenate([cum_past, cq], axis=1)[:, :, :FOX_HEADS], (0, 2, 1))
        per_head = lambda c: jnp.transpose(c, (0, 1, 3, 4, 2)).reshape(-1, FOX_HEADS, FOX_HEAD_DIM, past)
        yc = _attention_cached(fq, cq, ck_t, kb, vb, proj, per_head(fox_k), per_head(fox_v), layer, nb, t)
    else:
        lf, cum = _gate(small, pr['fox_f_bias'], None, nb, t, _block_rows(t, 512))
        fk, fv, qa, ka, va = _foxprep_aug(kv, layer, proj, cum, pr['head_sum'], pr['fox_q_norm'],
                                          pr['fox_k_norm'], pr['aug'], t, min(t, tm))
        yc = _attention_aug(qa, ka, va, proj, nb, t, _block_rows(t, 512), FOX_HEADS)
    lf = lf.reshape(nb, t, LANES)

    x1 = _out_proj(x2d, (ya, yb, yc, yd), pr['w_out'], layer, tm)
    x2 = _ffn(x1, pr['ln2'], pr['w_gate'], pr['w_up'], pr['w_down'], layer, tm, 512)

    new_state = (ssm_conv_new, ssm_s_new, hgrn_s_new, lf[:, :, :FOX_HEADS], lru_conv_new,
                 lru_h_new.reshape(nb, GROUP_W))
    return x2, new_state, (fk, fv)


def _prep_layer_params(l, ln1, ln2, ssm_conv_w, ssm_conv_b, ssm_dt_bias, ssm_a_log, ssm_d,
                       ssm_norm, hgrn_lb_logits, hgrn_norm, fox_q_norm, fox_k_norm, fox_f_bias, lru_conv_w,
                       lru_conv_b, lru_w_r, lru_b_r, lru_w_i, lru_b_i, lru_lambda):
    hid = jnp.arange(GROUP_W) // FOX_HEAD_DIM
    row = lambda v: v.astype(F32).reshape(1, -1)
    return dict(
        ln1=row(ln1[l]), ln2=row(ln2[l]),
        ssm_conv_w=ssm_conv_w[l], ssm_conv_b=row(ssm_conv_b[l]),
        ssm_dt_bias=_pad_lanes(ssm_dt_bias[l]), ssm_a_log=_pad_lanes(ssm_a_log[l]),
        ssm_d=row(jnp.repeat(ssm_d[l], SSM_HEAD_DIM)), ssm_norm=row(ssm_norm[l]),
        hgrn_lb_logits=hgrn_lb_logits.astype(F32), hgrn_norm=row(hgrn_norm[l]),
        head_sum=(hid[:, None] == hid[None, :]).astype(BF16), aug=_aug_constants(),
        fox_q_norm=row(jnp.tile(fox_q_norm[l], FOX_HEADS)), fox_k_norm=row(jnp.tile(fox_k_norm[l], FOX_HEADS)),
        fox_f_bias=_pad_lanes(fox_f_bias[l]),
        lru_conv_w=lru_conv_w[l], lru_conv_b=row(lru_conv_b[l]),
        lru_w_r=lru_w_r[l].astype(BF16), lru_b_r=row(lru_b_r[l]),
        lru_w_i=lru_w_i[l].astype(BF16), lru_b_i=row(lru_b_i[l]), lru_lambda=row(lru_lambda[l]))


def _prep_dense_weights(w_in, w_out, w_gate, w_up, w_down):
    depth = w_in.shape[0]
    o_dt = GROUP_W + SSM_CONV_DIM
    o_cf = o_dt + SSM_HEADS + 7 * GROUP_W
    w_t = jnp.transpose(w_in, (0, 2, 1))
    w_main = jnp.concatenate([w_t[:, :o_dt], w_t[:, o_dt + SSM_HEADS:o_cf], w_t[:, o_cf + FOX_HEADS:]], axis=1)
    w_small = jnp.zeros((depth, 2 * LANES, D_MODEL), F32)
    w_small = w_small.at[:, :SSM_HEADS].set(w_t[:, o_dt:o_dt + SSM_HEADS])
    w_small = w_small.at[:, LANES:LANES + FOX_HEADS].set(w_t[:, o_cf:o_cf + FOX_HEADS])
    return dict(w_main=w_main.astype(BF16), w_small=w_small.astype(BF16), w_out=w_out.astype(BF16),
                w_gate=w_gate.astype(BF16), w_up=w_up.astype(BF16), w_down=w_down.astype(BF16))


def kernel(x_prompt, x_sample, cache_fox_k, cache_fox_v, cache_fox_logf, state_ssm_conv, state_ssm, state_hgrn,
           state_lru_conv, state_lru, ln1, ln2, w_in, w_out, ssm_conv_w, ssm_conv_b, ssm_dt_bias, ssm_a_log,
           ssm_d, ssm_norm, hgrn_lb_logits, hgrn_norm, fox_q_norm, fox_k_norm, fox_f_bias, lru_conv_w,
           lru_conv_b, lru_w_r, lru_b_r, lru_w_i, lru_b_i, lru_lambda, w_gate, w_up, w_down):
    depth = ln1.shape[0]
    bp, tp, _ = x_prompt.shape
    bs, ts, _ = x_sample.shape
    fresh = (jnp.zeros((bp, SSM_CONV - 1, SSM_CONV_DIM), F32),
             jnp.zeros((bp, SSM_HEADS, SSM_STATE, SSM_HEAD_DIM), F32),
             jnp.zeros((bp, HGRN_HEADS, HGRN_HEAD_DIM, HGRN_HEAD_DIM), F32),
             None, None, None,
             jnp.zeros((bp, LRU_CONV - 1, GROUP_W), F32),
             jnp.zeros((bp, GROUP_W), F32))
    yp = x_prompt.reshape(bp * tp, D_MODEL)
    ys = x_sample.reshape(bs * ts, D_MODEL)
    p_states, s_states = [], []
    new_kv = lambda nb, t: tuple(jnp.zeros((depth * nb, FOX_HEADS, FOX_HEAD_DIM, t), F32) for _ in range(2))
    kv_p, kv_s = new_kv(bp, tp), new_kv(bs, ts)
    dense =_prep_dense_weights(w_in, w_out, w_gate, w_up, w_down)
    for l in range(depth):
        pr = _prep_layer_params(l, ln1, ln2, ssm_conv_w, ssm_conv_b, ssm_dt_bias, ssm_a_log, ssm_d,
                                ssm_norm, hgrn_lb_logits, hgrn_norm, fox_q_norm, fox_k_norm, fox_f_bias,
                                lru_conv_w, lru_conv_b, lru_w_r, lru_b_r, lru_w_i, lru_b_i, lru_lambda)
        pr.update(dense)
        yp, st_p, kv_p = _layer(yp, bp, tp, fresh, pr, l, kv_p)
        p_states.append(st_p)
        past = (state_ssm_conv[l], state_ssm[l], state_hgrn[l], cache_fox_k, cache_fox_v,
                cache_fox_logf[l], state_lru_conv[l], state_lru[l])
        ys, st_s, kv_s = _layer(ys, bs, ts, past, pr, l, kv_s)
        s_states.append(st_s)

    def group(states, kv, nb, t):
        conv, ssm, hgrn, lf, lru_conv, lru = [jnp.stack(parts, axis=0) for parts in zip(*states)]
        k, v = [jnp.transpose(a.reshape(depth, nb, FOX_HEADS, FOX_HEAD_DIM, t), (0, 1, 4, 2, 3)) for a in kv]
        return conv, ssm, hgrn, k, v, lf, lru_conv, lru

    return (yp.reshape(bp, tp, D_MODEL), ys.reshape(bs, ts, D_MODEL),
            *group(p_states, kv_p, bp, tp), *group(s_states, kv_s, bs, ts))
```

```python
import functools
import math

import jax
import jax.numpy as jnp
import numpy as np
from jax import lax
from jax.experimental import pallas as pl
from jax.experimental.pallas import tpu as pltpu

F32 = jnp.float32
BF16 = jnp.bfloat16

D_MODEL = 2048
GROUP_W = 512
CHUNK = 64
SUB = 16
SSM_HEADS = 8
SSM_HEAD_DIM = 64
SSM_GROUPS = 2
SSM_STATE = 128
SSM_CONV = 4
SSM_CONV_DIM = 1024
HGRN_HEADS = 4
HGRN_HEAD_DIM = 128
HGRN_F_FLOOR = 1e-30
FOX_HEADS = 8
FOX_HEAD_DIM = 64
LRU_BLOCKS = 4
LRU_BLOCK_W = 128
LRU_CONV = 4
LRU_C = 8.0
EPS = 1e-6
NEG = -1e30

LANES = 128
CONV_PAD = 8
VMEM_LIMIT = 56 * 1024 * 1024

COL_A_Z, COL_A_X, COL_A_BC, COL_B_Q, COL_B_F, COL_B_I, COL_B_G = 0, 1, 2, 3, 4, 5, 6
COL_C_Q, COL_C_K, COL_C_V, COL_C_G, COL_D_X, COL_D_G = 7, 8, 9, 10, 11, 12
N_MAIN_BLOCKS = 13


def _cparams(sem):
    return pltpu.CompilerParams(dimension_semantics=sem, vmem_limit_bytes=VMEM_LIMIT)


def _dot(a, b):
    return jnp.dot(a, b, preferred_element_type=F32)


def _dot_nt(a, b):
    return lax.dot_general(a, b, (((1,), (1,)), ((), ())), preferred_element_type=F32)


def _dot_tn(a, b):
    return lax.dot_general(a, b, (((0,), (0,)), ((), ())), preferred_element_type=F32)


def _split3(x):
    hi = x.astype(BF16)
    r = x - hi.astype(F32)
    mid = r.astype(BF16)
    lo = (r - mid.astype(F32)).astype(BF16)
    return hi, mid, lo


def _dot01(m01, x):
    hi, mid, lo = _split3(x)
    return _dot(m01, hi) + _dot(m01, mid) + _dot(m01, lo)


def _dot01_r(x, m01):
    hi, mid, lo = _split3(x)
    return _dot(hi, m01) + _dot(mid, m01) + _dot(lo, m01)


def _tri(n, lower=True):
    r = lax.broadcasted_iota(jnp.int32, (n, n), 0)
    c = lax.broadcasted_iota(jnp.int32, (n, n), 1)
    return (r >= c) if lower else (r <= c)


def _sigmoid(x):
    return jax.nn.sigmoid(x)


def _silu(x):
    return x * jax.nn.sigmoid(x)


def _softplus(x):
    return jnp.maximum(x, 0.0) + jnp.log1p(jnp.exp(-jnp.abs(x)))


def _log_sigmoid(x):
    return -_softplus(-x)


def _in_proj_kernel(x_ref, g_ref, w_ref, ws_ref, o_ref, os_ref, h_scr):
    @pl.when(pl.program_id(1) == 0)
    def _():
        x = x_ref[...]
        ms = jnp.mean(x * x, axis=-1, keepdims=True)
        h = (x * lax.rsqrt(ms + EPS) * g_ref[...]).astype(BF16)
        h_scr[...] = h
        os_ref[...] = _dot_nt(h, ws_ref[...])

    o_ref[...] = _dot_nt(h_scr[...], w_ref[...])


def _in_proj(x2d, ln, w_main, w_small, layer, tm):
    m = x2d.shape[0]
    n_main, n_small = w_main.shape[1], w_small.shape[1]
    return pl.pallas_call(
        _in_proj_kernel,
        grid=(m // tm, n_main // GROUP_W),
        in_specs=[
            pl.BlockSpec((tm, D_MODEL), lambda i, j: (i, 0)),
            pl.BlockSpec((1, D_MODEL), lambda i, j: (0, 0)),
            pl.BlockSpec((None, GROUP_W, D_MODEL), lambda i, j: (layer, j, 0)),
            pl.BlockSpec((None, n_small, D_MODEL), lambda i, j: (layer, 0, 0)),
        ],
        out_specs=[
            pl.BlockSpec((tm, GROUP_W), lambda i, j: (i, j)),
            pl.BlockSpec((tm, n_small), lambda i, j: (i, 0)),
        ],
        out_shape=[jax.ShapeDtypeStruct((m, n_main), F32), jax.ShapeDtypeStruct((m, n_small), F32)],
        scratch_shapes=[pltpu.VMEM((tm, D_MODEL), BF16)],
        compiler_params=_cparams(("parallel", "arbitrary")),
        name="in_proj",
    )(x2d, ln, w_main, w_small)


def _conv_block(xp_scr, cw_ref, cb_ref, tb, kw):
    first = CONV_PAD - (kw - 1)
    y = cb_ref[...]
    for j in range(kw):
        y = y + cw_ref[j:j + 1, :] * xp_scr[pl.ds(first + j, tb), :]
    tail = xp_scr[pl.ds(first + tb, kw - 1), :]
    xp_scr[pl.ds(first, kw - 1), :] = tail
    return y, tail


def _ssd_kernel(z_ref, xlo_ref, xhi_ref, sm_ref, conv0_ref, s0_ref, cw_ref, cb_ref, dtb_ref, alog_ref,
                dexp_ref, nrm_ref, y_ref, convo_ref, so_ref,
                xp_scr, xbc_scr, g_scr, dt_scr, y_scr, s_scr, *, tb):
    c = pl.program_id(1)
    last = pl.num_programs(1) - 1
    hpg = SSM_HEADS // SSM_GROUPS
    gw = hpg * SSM_HEAD_DIM

    @pl.when(c == 0)
    def _():
        s_scr[...] = s0_ref[...]
        xp_scr[0:CONV_PAD, :] = jnp.zeros((CONV_PAD, SSM_CONV_DIM), F32)
        xp_scr[CONV_PAD - (SSM_CONV - 1):CONV_PAD, :] = conv0_ref[...]

    xp_scr[CONV_PAD:CONV_PAD + tb, 0:GROUP_W] = xlo_ref[...]
    xp_scr[CONV_PAD:CONV_PAD + tb, GROUP_W:SSM_CONV_DIM] = xhi_ref[...]
    conv, tail = _conv_block(xp_scr, cw_ref, cb_ref, tb, SSM_CONV)

    @pl.when(c == last)
    def _():
        convo_ref[...] = tail

    xbc_scr[...] = _silu(conv)
    dt = _softplus(sm_ref[...] + dtb_ref[...])
    dt_scr[...] = dt
    g_scr[...] = dt * (-jnp.exp(alog_ref[...]))

    tri_mask = _tri(CHUNK)
    tri_l = tri_mask.astype(BF16)
    tri_u = _tri(CHUNK, lower=False).astype(BF16)
    b_off = GROUP_W
    c_off = GROUP_W + SSM_GROUPS * SSM_STATE

    def chunk(ci, carry):
        rows = pl.ds(pl.multiple_of(ci * CHUNK, CHUNK), CHUNK)
        gc = g_scr[rows, :]
        dtc = dt_scr[rows, :]
        cum = _dot01(tri_l, gc)
        cum_t = _dot01_r(gc.T, tri_u)
        tot = cum[CHUNK - 1:CHUNK, :]
        ys = []
        for grp in range(SSM_GROUPS):
            bm = xbc_scr[rows, b_off + grp * SSM_STATE:b_off + (grp + 1) * SSM_STATE].astype(BF16)
            cm = xbc_scr[rows, c_off + grp * SSM_STATE:c_off + (grp + 1) * SSM_STATE].astype(BF16)
            gmat = _dot_nt(cm, bm)
            s_g = s_scr[grp]
            cs = _dot(cm, s_g.astype(BF16))
            vdec, etot = [], []
            for hh in range(hpg):
                h = grp * hpg + hh
                col = cum[:, h:h + 1]
                row = cum_t[h:h + 1, :]
                dec = jnp.where(tri_mask, jnp.exp(jnp.minimum(col - row, 0.0)), 0.0)
                att = (gmat * dec).astype(BF16)
                xh = xbc_scr[rows, h * SSM_HEAD_DIM:(h + 1) * SSM_HEAD_DIM]
                vh = xh * dtc[:, h:h + 1]
                o = _dot(att, vh.astype(BF16)) + jnp.exp(col) * cs[:, hh * SSM_HEAD_DIM:(hh + 1) * SSM_HEAD_DIM]
                ys.append(o + dexp_ref[:, h * SSM_HEAD_DIM:(h + 1) * SSM_HEAD_DIM] * xh)
                toth = tot[:, h:h + 1]
                vdec.append(vh * jnp.exp(toth - col))
                etot.append(jnp.broadcast_to(jnp.exp(toth), (1, SSM_HEAD_DIM)))
            vdec = jnp.concatenate(vdec, axis=1).astype(BF16)
            etot = jnp.concatenate(etot, axis=1)
            s_scr[grp] = etot * s_g + _dot_tn(bm, vdec)
        y_scr[rows, :] = jnp.concatenate(ys, axis=1)
        return carry

    lax.fori_loop(0, tb // CHUNK, chunk, 0)

    y = y_scr[...] * _silu(z_ref[...])
    outs = []
    for grp in range(SSM_GROUPS):
        yg = y[:, grp * gw:(grp + 1) * gw]
        ms = jnp.mean(yg * yg, axis=-1, keepdims=True)
        outs.append(yg * lax.rsqrt(ms + EPS))
    y_ref[...] = (jnp.concatenate(outs, axis=1) * nrm_ref[...]).astype(y_ref.dtype)

    @pl.when(c == last)
    def _():
        so_ref[...] = s_scr[...]


def _ssd(proj, small, conv0, s0, cw, cb, dtb, alog, dexp, nrm, nb, t, tb):
    nc = t // tb
    hpg = SSM_HEADS // SSM_GROUPS
    sshape = (SSM_GROUPS, SSM_STATE, hpg * SSM_HEAD_DIM)
    row = lambda col: pl.BlockSpec((tb, GROUP_W), lambda b, c: (b * nc + c, col))
    full = lambda a: pl.BlockSpec(a.shape, lambda b, c: (0,) * a.ndim)
    return pl.pallas_call(
        functools.partial(_ssd_kernel, tb=tb),
        grid=(nb, nc),
        in_specs=[row(COL_A_Z), row(COL_A_X), row(COL_A_BC),
                  pl.BlockSpec((tb, LANES), lambda b, c: (b * nc + c, 0)),
                  pl.BlockSpec((None, SSM_CONV - 1, SSM_CONV_DIM), lambda b, c: (b, 0, 0)),
                  pl.BlockSpec((None,) + sshape, lambda b, c: (b, 0, 0, 0)),
                  full(cw), full(cb), full(dtb), full(alog), full(dexp), full(nrm)],
        out_specs=[pl.BlockSpec((tb, GROUP_W), lambda b, c: (b * nc + c, 0)),
                   pl.BlockSpec((None, SSM_CONV - 1, SSM_CONV_DIM), lambda b, c: (b, 0, 0)),
                   pl.BlockSpec((None,) + sshape, lambda b, c: (b, 0, 0, 0))],
        out_shape=[jax.ShapeDtypeStruct((nb * t, GROUP_W), BF16),
                   jax.ShapeDtypeStruct((nb, SSM_CONV - 1, SSM_CONV_DIM), F32),
                   jax.ShapeDtypeStruct((nb,) + sshape, F32)],
        scratch_shapes=[pltpu.VMEM((CONV_PAD + tb, SSM_CONV_DIM), F32),
                        pltpu.VMEM((tb, SSM_CONV_DIM), F32),
                        pltpu.VMEM((tb, LANES), F32),
                        pltpu.VMEM((tb, LANES), F32),
                        pltpu.VMEM((tb, GROUP_W), F32),
                        pltpu.VMEM(sshape, F32)],
        compiler_params=_cparams(("parallel", "arbitrary")),
        name="ssd",
    )(proj, proj, proj, small, conv0, s0, cw, cb, dtb, alog, dexp, nrm)


def _hgrn_kernel(q_ref, f_ref, i_ref, g_ref, s0_ref, lbl_ref, nrm_ref, y_ref, so_ref,
                 c_scr, k_scr, st_scr, *, tb, layer):
    c = pl.program_id(1)
    last = pl.num_programs(1) - 1
    hd = HGRN_HEAD_DIM
    nsub = CHUNK // SUB

    @pl.when(c == 0)
    def _():
        for h in range(HGRN_HEADS):
            st_scr[h] = s0_ref[h].T

    ll = lbl_ref[...]
    e = jnp.exp(ll - jnp.max(ll, axis=0, keepdims=True))
    p = e / jnp.sum(e, axis=0, keepdims=True)
    cs = p[0:1, :]
    for l in range(1, layer + 1):
        cs = cs + p[l:l + 1, :]
    lb = cs - p[0:1, :]

    tri_l = _tri(CHUNK).astype(BF16)
    rowid = lax.broadcasted_iota(jnp.int32, (CHUNK, 1), 0)
    subid = lax.broadcasted_iota(jnp.int32, (SUB, 1), 0)

    def chunk(ci, carry):
        r0 = pl.multiple_of(ci * CHUNK, CHUNK)
        rows = pl.ds(r0, CHUNK)
        fr = f_ref[rows, :]
        fg = lb + (1.0 - lb) * _sigmoid(fr)
        logf = jnp.log(jnp.maximum(fg, HGRN_F_FLOOR))
        hk = (1.0 - lb) * _sigmoid(-fr)
        hq = _silu(q_ref[rows, :])
        v = i_ref[rows, :]
        vb = v.astype(BF16)
        cum = _dot01(tri_l, logf)
        tot = cum[CHUNK - 1:CHUNK, :]
        c_scr[...] = cum
        k_scr[...] = hk
        qe = (hq * jnp.exp(cum)).astype(BF16)
        kd = (hk * jnp.exp(tot - cum)).astype(BF16)
        etot = jnp.exp(tot)

        qcat, kcat = [], []
        for j in range(nsub - 1):
            ej = cum[(j + 1) * SUB - 1:(j + 1) * SUB, :]
            qcat.append(jnp.where(rowid >= (j + 1) * SUB, hq * jnp.exp(jnp.minimum(cum - ej, 0.0)), 0.0).astype(BF16))
            in_j = (rowid >= j * SUB) & (rowid < (j + 1) * SUB)
            kcat.append(jnp.where(in_j, hk * jnp.exp(jnp.minimum(ej - cum, 0.0)), 0.0).astype(BF16))

        o_heads = []
        for h in range(HGRN_HEADS):
            hs = slice(h * hd, (h + 1) * hd)
            qc = jnp.concatenate([q[:, hs] for q in qcat], axis=1)
            kc = jnp.concatenate([k[:, hs] for k in kcat], axis=1)
            a_off = _dot_nt(qc, kc)
            st = st_scr[h]
            o_heads.append(_dot(a_off.astype(BF16), vb[:, hs]) + _dot_nt(qe[:, hs], st.astype(BF16)))
            st_scr[h] = st * etot[:, hs] + _dot_tn(vb[:, hs], kd[:, hs])
        o = jnp.concatenate(o_heads, axis=1)

        o_sub = []
        for i in range(nsub):
            base = i * SUB
            ci_ = cum[base:base + SUB, :]
            qi = hq[base:base + SUB, :]
            acc = [jnp.zeros((SUB, hd), F32) for _ in range(HGRN_HEADS)]
            for j in range(SUB):
                crow = c_scr[pl.ds(base + j, 1), :]
                krow = k_scr[pl.ds(base + j, 1), :]
                vrow = i_ref[pl.ds(r0 + base + j, 1), :]
                x = jnp.where(subid >= j, qi * jnp.exp(jnp.minimum(ci_ - crow, 0.0)) * krow, 0.0)
                for h in range(HGRN_HEADS):
                    hs = slice(h * hd, (h + 1) * hd)
                    a = jnp.sum(x[:, hs], axis=-1, keepdims=True)
                    acc[h] = acc[h] + a * vrow[:, hs]
            o_sub.append(jnp.concatenate(acc, axis=1))
        o = o + jnp.concatenate(o_sub, axis=0)

        outs = []
        for h in range(HGRN_HEADS):
            oh = o[:, h * hd:(h + 1) * hd]
            ms = jnp.mean(oh * oh, axis=-1, keepdims=True)
            outs.append(oh * lax.rsqrt(ms + EPS))
        y = jnp.concatenate(outs, axis=1) * nrm_ref[...] * _silu(g_ref[rows, :])
        y_ref[rows, :] = y.astype(y_ref.dtype)
        return carry

    lax.fori_loop(0, tb // CHUNK, chunk, 0)

    @pl.when(c == last)
    def _():
        for h in range(HGRN_HEADS):
            so_ref[h] = st_scr[h].T


def _hgrn(proj, s0, lbl, nrm, nb, t, tb, layer):
    nc = t // tb
    sshape = (HGRN_HEADS, HGRN_HEAD_DIM, HGRN_HEAD_DIM)
    row = lambda col: pl.BlockSpec((tb, GROUP_W), lambda b, c: (b * nc + c, col))
    full = lambda a: pl.BlockSpec(a.shape, lambda b, c: (0,) * a.ndim)
    return pl.pallas_call(
        functools.partial(_hgrn_kernel, tb=tb, layer=layer),
        grid=(nb, nc),
        in_specs=[row(COL_B_Q), row(COL_B_F), row(COL_B_I), row(COL_B_G),
                  pl.BlockSpec((None,) + sshape, lambda b, c: (b, 0, 0, 0)),
                  full(lbl), full(nrm)],
        out_specs=[pl.BlockSpec((tb, GROUP_W), lambda b, c: (b * nc + c, 0)),
                   pl.BlockSpec((None,) + sshape, lambda b, c: (b, 0, 0, 0))],
        out_shape=[jax.ShapeDtypeStruct((nb * t, GROUP_W), BF16),
                   jax.ShapeDtypeStruct((nb,) + sshape, F32)],
        scratch_shapes=[pltpu.VMEM((CHUNK, GROUP_W), F32),
                        pltpu.VMEM((CHUNK, GROUP_W), F32),
                        pltpu.VMEM(sshape, F32)],
        compiler_params=_cparams(("parallel", "arbitrary")),
        name="hgrn2",
    )(proj, proj, proj, proj, s0, lbl, nrm)


def _lru_kernel(x_ref, g_ref, conv0_ref, h0_ref, cw_ref, cb_ref, wr_ref, br_ref, wi_ref, bi_ref, lam_ref,
                y_ref, convo_ref, ho_ref, xp_scr, h_scr, *, tb, reset_first):
    c = pl.program_id(1)
    last = pl.num_programs(1) - 1

    @pl.when(c == 0)
    def _():
        h_scr[...] = h0_ref[...]
        xp_scr[0:CONV_PAD, :] = jnp.zeros((CONV_PAD, GROUP_W), F32)
        xp_scr[CONV_PAD - (LRU_CONV - 1):CONV_PAD, :] = conv0_ref[...]

    xp_scr[CONV_PAD:CONV_PAD + tb, :] = x_ref[...]
    xc, tail = _conv_block(xp_scr, cw_ref, cb_ref, tb, LRU_CONV)

    @pl.when(c == last)
    def _():
        convo_ref[...] = tail

    xcb = xc.astype(BF16)

    def blockdiag(w_ref):
        return jnp.concatenate(
            [_dot(xcb[:, n * LRU_BLOCK_W:(n + 1) * LRU_BLOCK_W], w_ref[n]) for n in range(LRU_BLOCKS)], axis=1)

    r = _sigmoid(blockdiag(wr_ref) + br_ref[...])
    gi = _sigmoid(blockdiag(wi_ref) + bi_ref[...])
    log_a = LRU_C * r * _log_sigmoid(lam_ref[...])
    a = jnp.exp(log_a)
    mult = jnp.sqrt(-jnp.tanh(log_a) * (a * a + 1.0))
    rowid = lax.broadcasted_iota(jnp.int32, (tb, 1), 0)
    first_row = rowid == 0
    if reset_first:
        rst = first_row & (c == 0)
        a = jnp.where(rst, 0.0, a)
        mult = jnp.where(rst, 1.0, mult)
    u = mult * gi * xc
    u = u + jnp.where(first_row, a * h_scr[...], 0.0)

    av, bv = a, u
    d = 1
    while d < tb:
        m = rowid >= d
        a_sh = pltpu.roll(av, d, axis=0)
        b_sh = pltpu.roll(bv, d, axis=0)
        bv = jnp.where(m, av * b_sh + bv, bv)
        av = jnp.where(m, av * a_sh, av)
        d *= 2
    h_new = bv[tb - 1:tb, :]
    h_scr[...] = h_new
    y_ref[...] = (bv * jax.nn.gelu(g_ref[...])).astype(y_ref.dtype)

    @pl.when(c == last)
    def _():
        ho_ref[...] = h_new


def _lru(proj, conv0, h0, cw, cb, wr, br, wi, bi, lam, nb, t, tb, reset_first):
    nc = t // tb
    row = lambda col: pl.BlockSpec((tb, GROUP_W), lambda b, c: (b * nc + c, col))
    full = lambda a: pl.BlockSpec(a.shape, lambda b, c: (0,) * a.ndim)
    return pl.pallas_call(
        functools.partial(_lru_kernel, tb=tb, reset_first=reset_first),
        grid=(nb, nc),
        in_specs=[row(COL_D_X), row(COL_D_G),
                  pl.BlockSpec((None, LRU_CONV - 1, GROUP_W), lambda b, c: (b, 0, 0)),
                  pl.BlockSpec((None, 1, GROUP_W), lambda b, c: (b, 0, 0)),
                  full(cw), full(cb), full(wr), full(br), full(wi), full(bi), full(lam)],
        out_specs=[pl.BlockSpec((tb, GROUP_W), lambda b, c: (b * nc + c, 0)),
                   pl.BlockSpec((None, LRU_CONV - 1, GROUP_W), lambda b, c: (b, 0, 0)),
                   pl.BlockSpec((None, 1, GROUP_W), lambda b, c: (b, 0, 0))],
        out_shape=[jax.ShapeDtypeStruct((nb * t, GROUP_W), BF16),
                   jax.ShapeDtypeStruct((nb, LRU_CONV - 1, GROUP_W), F32),
                   jax.ShapeDtypeStruct((nb, 1, GROUP_W), F32)],
        scratch_shapes=[pltpu.VMEM((CONV_PAD + tb, GROUP_W), F32),
                        pltpu.VMEM((1, GROUP_W), F32)],
        compiler_params=_cparams(("parallel", "arbitrary")),
        name="rglru",
    )(proj, proj, conv0, h0, cw, cb, wr, br, wi, bi, lam)


def _gate_kernel(*refs, tb, past):
    if past:
        sm_ref, fb_ref, plf_ref, lf_ref, cum_ref, cump_ref, carry_scr = refs
    else:
        sm_ref, fb_ref, lf_ref, cum_ref, carry_scr = refs

    @pl.when(pl.program_id(1) == 0)
    def _():
        if past:
            cp = _dot01(_tri(past).astype(BF16), plf_ref[...])
            cump_ref[...] = cp
            carry_scr[...] = cp[past - 1:past, :]
        else:
            carry_scr[...] = jnp.zeros_like(carry_scr)

    lf = _log_sigmoid(sm_ref[...] + fb_ref[...])
    lf_ref[...] = lf
    cum = _dot01(_tri(tb).astype(BF16), lf) + carry_scr[...]
    cum_ref[...] = cum
    carry_scr[...] = cum[tb - 1:tb, :]


def _gate(small, fb, past_lf, nb, t, tb):
    nc = t // tb
    past = 0 if past_lf is None else past_lf.shape[1]
    rows = pl.BlockSpec((tb, LANES), lambda b, c: (b * nc + c, 0))
    in_specs = [pl.BlockSpec((tb, LANES), lambda b, c: (b * nc + c, 1)),
                pl.BlockSpec(fb.shape, lambda b, c: (0, 0))]
    out_specs = [rows, rows]
    out_shape = [jax.ShapeDtypeStruct((nb * t, LANES), F32)] * 2
    args = [small, fb]
    if past:
        in_specs.append(pl.BlockSpec((None, past, LANES), lambda b, c: (b, 0, 0)))
        out_specs.append(pl.BlockSpec((None, past, LANES), lambda b, c: (b, 0, 0)))
        out_shape.append(jax.ShapeDtypeStruct((nb, past, LANES), F32))
        args.append(past_lf)
    return pl.pallas_call(
        functools.partial(_gate_kernel, tb=tb, past=past),
        grid=(nb, nc),
        in_specs=in_specs, out_specs=out_specs, out_shape=out_shape,
        scratch_shapes=[pltpu.VMEM((1, LANES), F32)],
        compiler_params=_cparams(("parallel", "arbitrary")),
        name="fox_gate",
    )(*args)


def _head_norm(x, gain, hsum):
    ms = _dot01_r(x * x, hsum) * (1.0 / FOX_HEAD_DIM)
    return x * lax.rsqrt(ms + EPS) * gain


def _store_heads(ref, x):
    xt = x.T
    for h in range(FOX_HEADS):
        ref[h] = xt[h * FOX_HEAD_DIM:(h + 1) * FOX_HEAD_DIM, :]


def _foxprep_kernel(kbuf_ref, vbuf_ref, q_ref, k_ref, v_ref, hsum_ref, qn_ref, kn_ref,
                    fq_ref, fk_ref, fv_ref, kb_ref, vb_ref):
    del kbuf_ref, vbuf_ref
    hsum = hsum_ref[...]
    fq_ref[...] = (_head_norm(q_ref[...], qn_ref[...], hsum) * (FOX_HEAD_DIM ** -0.5)).astype(BF16)
    fk = _head_norm(k_ref[...], kn_ref[...], hsum)
    _store_heads(fk_ref, fk)
    kb_ref[...] = fk.astype(BF16)
    v = v_ref[...]
    _store_heads(fv_ref, v)
    vb_ref[...] = v.astype(BF16)


def _heads_spec(tb, t, first_stream):
    nc = t // tb
    return pl.BlockSpec((None, FOX_HEADS, FOX_HEAD_DIM, tb), lambda i: (first_stream + i // nc, 0, 0, i % nc))


_ANY = pl.BlockSpec(memory_space=pl.ANY)


def _foxprep(kv, layer, proj, hsum, qn, kn, t, tb):
    m = proj.shape[0]
    row = lambda col: pl.BlockSpec((tb, GROUP_W), lambda i: (i, col))
    full = lambda a: pl.BlockSpec(a.shape, lambda i: (0,) * a.ndim)
    out_row = pl.BlockSpec((tb, GROUP_W), lambda i: (i, 0))
    heads = _heads_spec(tb, t, layer * (m // t))
    return pl.pallas_call(
        _foxprep_kernel,
        grid=(m // tb,),
        in_specs=[_ANY, _ANY, row(COL_C_Q), row(COL_C_K), row(COL_C_V), full(hsum), full(qn), full(kn)],
        out_specs=[out_row, heads, heads, out_row, out_row],
        out_shape=[jax.ShapeDtypeStruct((m, GROUP_W), BF16),
                   jax.ShapeDtypeStruct(kv[0].shape, F32),
                   jax.ShapeDtypeStruct(kv[1].shape, F32),
                   jax.ShapeDtypeStruct((m, GROUP_W), BF16),
                   jax.ShapeDtypeStruct((m, GROUP_W), BF16)],
        input_output_aliases={0: 1, 1: 2},
        compiler_params=_cparams(("parallel",)),
        name="fox_prep",
    )(kv[0], kv[1], proj, proj, proj, hsum, qn, kn)


AUG_W = FOX_HEADS * LANES
N_PIECES = 3


def _aug_constants():
    pq = np.zeros((N_PIECES * LANES, AUG_W), np.float32)
    pk = np.zeros((N_PIECES * LANES, AUG_W), np.float32)
    ones_q = np.zeros((1, AUG_W), np.float32)
    ones_k = np.zeros((1, AUG_W), np.float32)
    ones_v = np.zeros((1, AUG_W), np.float32)
    head = np.zeros((1, AUG_W), np.float32)
    for h in range(FOX_HEADS):
        own = h * LANES + FOX_HEAD_DIM * (h % 2)
        other = h * LANES + FOX_HEAD_DIM * (1 - h % 2)
        head[0, own:own + FOX_HEAD_DIM] = 1.0
        for piece in range(N_PIECES):
            pq[piece * LANES + h, other + piece] = 1.0
            ones_k[0, other + piece] = 1.0
            ones_q[0, other + N_PIECES + piece] = 1.0
            pk[piece * LANES + h, other + N_PIECES + piece] = -1.0
        ones_v[0, other] = 1.0
    return dict(aug_pq=jnp.asarray(pq, BF16), aug_pk=jnp.asarray(pk, BF16), aug_ones_q=jnp.asarray(ones_q),
                aug_ones_k=jnp.asarray(ones_k), aug_ones_v=jnp.asarray(ones_v), aug_head=jnp.asarray(head))


def _foxprep_aug_kernel(kbuf_ref, vbuf_ref, q_ref, k_ref, v_ref, cum_ref, hsum_ref, qn_ref, kn_ref, pq_ref, pk_ref,
                        oq_ref, ok_ref, ov_ref, head_ref, fk_ref, fv_ref, qa_ref, ka_ref, va_ref):
    del kbuf_ref, vbuf_ref
    hsum = hsum_ref[...]
    fq = _head_norm(q_ref[...], qn_ref[...], hsum) * (FOX_HEAD_DIM ** -0.5)
    fk = _head_norm(k_ref[...], kn_ref[...], hsum)
    v = v_ref[...]
    _store_heads(fk_ref, fk)
    _store_heads(fv_ref, v)
    pieces = jnp.concatenate(_split3(cum_ref[...]), axis=1)
    own = head_ref[...] > 0.5

    def per_head(x):
        return jnp.concatenate([x[:, (h // 2) * LANES:(h // 2 + 1) * LANES] for h in range(FOX_HEADS)], axis=1)

    qa = jnp.where(own, per_head(fq), _dot(pieces, pq_ref[...]) + oq_ref[...]).astype(BF16)
    ka = jnp.where(own, per_head(fk), _dot(pieces, pk_ref[...]) + ok_ref[...]).astype(BF16)
    va = jnp.where(own, per_head(v), ov_ref[...]).astype(BF16)
    for h in range(FOX_HEADS):
        hs = slice(h * LANES, (h + 1) * LANES)
        qa_ref[h] = qa[:, hs]
        ka_ref[h] = ka[:, hs]
        va_ref[h] = va[:, hs]


def _foxprep_aug(kv, layer, proj, cum, hsum, qn, kn, aug, t, tb):
    m = proj.shape[0]
    row = lambda col: pl.BlockSpec((tb, GROUP_W), lambda i: (i, col))
    full = lambda a: pl.BlockSpec(a.shape, lambda i: (0,) * a.ndim)
    out_aug = pl.BlockSpec((FOX_HEADS, tb, LANES), lambda i: (0, i, 0))
    heads = _heads_spec(tb, t, layer * (m // t))
    consts = [aug['aug_pq'], aug['aug_pk'], aug['aug_ones_q'], aug['aug_ones_k'], aug['aug_ones_v'],
              aug['aug_head']]
    return pl.pallas_call(
        _foxprep_aug_kernel,
        grid=(m // tb,),
        in_specs=[_ANY, _ANY, row(COL_C_Q), row(COL_C_K), row(COL_C_V),
                  pl.BlockSpec((tb, LANES), lambda i: (i, 0)),
                  full(hsum), full(qn), full(kn)] + [full(a) for a in consts],
        out_specs=[heads, heads, out_aug, out_aug, out_aug],
        out_shape=[jax.ShapeDtypeStruct(kv[0].shape, F32), jax.ShapeDtypeStruct(kv[1].shape, F32)]
                  + [jax.ShapeDtypeStruct((FOX_HEADS, m, LANES), BF16)] * 3,
        input_output_aliases={0: 0, 1: 1},
        compiler_params=_cparams(("parallel",)),
        name="fox_prep_aug",
    )(kv[0], kv[1], proj, proj, proj, cum, hsum, qn, kn, *consts)


def _attn_aug_kernel(q_ref, k_ref, v_ref, g_ref, y_ref, *, tq, hg):
    qi = pl.program_id(2)
    causal = _tri(tq)
    low = lax.broadcasted_iota(jnp.int32, (1, LANES), 1) < FOX_HEAD_DIM
    qs = [q_ref[i] for i in range(hg)]

    def step(j, carry, masked):
        rows = pl.ds(pl.multiple_of(j * tq, tq), tq)
        new = []
        for i in range(hg):
            m, acc = carry[i]
            s = _dot_nt(qs[i], k_ref[i, rows, :])
            if masked:
                s = jnp.where(causal, s, NEG)
            m_new = jnp.maximum(m, jnp.max(s, axis=-1, keepdims=True))
            p = jnp.exp(s - m_new).astype(BF16)
            acc = jnp.exp(m - m_new) * acc + _dot(p, v_ref[i, rows, :])
            new.append((m_new, acc))
        return tuple(new)

    init = tuple((jnp.full((tq, 1), NEG, F32), jnp.zeros((tq, LANES), F32)) for _ in range(hg))
    carry = lax.fori_loop(0, qi, lambda j, c: step(j, c, False), init)
    carry = step(qi, carry, True)
    outs = []
    for i in range(0, hg, 2):
        acc_e, acc_o = carry[i][1], carry[i + 1][1]
        out_e = acc_e / acc_e[:, FOX_HEAD_DIM:FOX_HEAD_DIM + 1]
        out_o = acc_o / acc_o[:, 0:1]
        outs.append(jnp.where(low, out_e, out_o))
    y = jnp.concatenate(outs, axis=1) * _sigmoid(g_ref[...])
    y_ref[...] = y.astype(y_ref.dtype)


def _attention_aug(qa, ka, va, proj, nb, t, tq, hg):
    nq = t // tq
    ngrp = FOX_HEADS // hg
    wout = hg * FOX_HEAD_DIM
    gate_col0 = COL_C_G * GROUP_W // wout
    return pl.pallas_call(
        functools.partial(_attn_aug_kernel, tq=tq, hg=hg),
        grid=(nb, ngrp, nq),
        in_specs=[pl.BlockSpec((hg, tq, LANES), lambda b, g, i: (g, b * nq + i, 0)),
                  pl.BlockSpec((hg, t, LANES), lambda b, g, i: (g, b, 0)),
                  pl.BlockSpec((hg, t, LANES), lambda b, g, i: (g, b, 0)),
                  pl.BlockSpec((tq, wout), lambda b, g, i: (b * nq + i, gate_col0 + g))],
        out_specs=pl.BlockSpec((tq, wout), lambda b, g, i: (b * nq + i, g)),
        out_shape=jax.ShapeDtypeStruct((nb * t, GROUP_W), BF16),
        compiler_params=_cparams(("parallel", "parallel", "arbitrary")),
        name="fox_attention_aug",
    )(qa, ka, va, proj)


def _attn_cached_kernel(q_ref, cq_ref, ck_ref, k_ref, v_ref, g_ref, pk_ref, pv_ref, y_ref, *, t, past):
    causal = _tri(t)
    outs = []
    for h in range(FOX_HEADS):
        hs = slice(h * FOX_HEAD_DIM, (h + 1) * FOX_HEAD_DIM)
        qh = q_ref[:, hs]
        pk = pk_ref[h].astype(BF16)
        pv = pv_ref[h].astype(BF16)
        cqh = cq_ref[:, h:h + 1]
        s_p = _dot(qh, pk) + cqh - ck_ref[h:h + 1, 0:past]
        s_c = jnp.where(causal, _dot_nt(qh, k_ref[:, hs]) + cqh - ck_ref[h:h + 1, past:past + t], NEG)
        m = jnp.maximum(jnp.max(s_p, axis=-1, keepdims=True), jnp.max(s_c, axis=-1, keepdims=True))
        p_p = jnp.exp(s_p - m)
        p_c = jnp.exp(s_c - m)
        l = jnp.sum(p_p, axis=-1, keepdims=True) + jnp.sum(p_c, axis=-1, keepdims=True)
        acc = _dot_nt(p_p.astype(BF16), pv) + _dot(p_c.astype(BF16), v_ref[:, hs])
        outs.append(acc / l)
    y = jnp.concatenate(outs, axis=1) * _sigmoid(g_ref[...])
    y_ref[...] = y.astype(y_ref.dtype)


def _attention_cached(fq, cq, ck_t, kb, vb, proj, past_k, past_v, layer, nb, t):
    past = past_k.shape[-1]
    ltot = ck_t.shape[-1]
    seq = lambda rows: pl.BlockSpec((None, rows, GROUP_W), lambda b: (b, 0, 0))
    cache = pl.BlockSpec((None, FOX_HEADS, FOX_HEAD_DIM, past), lambda b: (layer * nb + b, 0, 0, 0))
    return pl.pallas_call(
        functools.partial(_attn_cached_kernel, t=t, past=past),
        grid=(nb,),
        in_specs=[pl.BlockSpec((t, GROUP_W), lambda b: (b, 0)),
                  pl.BlockSpec((None, t, LANES), lambda b: (b, 0, 0)),
                  pl.BlockSpec((None, FOX_HEADS, ltot), lambda b: (b, 0, 0)),
                  seq(t), seq(t),
                  pl.BlockSpec((t, GROUP_W), lambda b: (b, COL_C_G)),
                  cache, cache],
        out_specs=pl.BlockSpec((t, GROUP_W), lambda b: (b, 0)),
        out_shape=jax.ShapeDtypeStruct((nb * t, GROUP_W), BF16),
        compiler_params=_cparams(("parallel",)),
        name="fox_attention_cached",
    )(fq, cq, ck_t, kb.reshape(nb, t, GROUP_W), vb.reshape(nb, t, GROUP_W), proj, past_k, past_v)


def _out_proj_kernel(x_ref, ya_ref, yb_ref, yc_ref, yd_ref, w_ref, o_ref):
    acc = x_ref[...]
    for n, y_ref in enumerate((ya_ref, yb_ref, yc_ref, yd_ref)):
        acc = acc + _dot(y_ref[...], w_ref[n * GROUP_W:(n + 1) * GROUP_W, :])
    o_ref[...] = acc


def _out_proj(x2d, ys, w_out, layer, tm):
    m = x2d.shape[0]
    yspec = pl.BlockSpec((tm, GROUP_W), lambda i: (i, 0))
    return pl.pallas_call(
        _out_proj_kernel,
        grid=(m // tm,),
        in_specs=[pl.BlockSpec((tm, D_MODEL), lambda i: (i, 0)), yspec, yspec, yspec, yspec,
                  pl.BlockSpec((None,) + w_out.shape[1:], lambda i: (layer, 0, 0))],
        out_specs=pl.BlockSpec((tm, D_MODEL), lambda i: (i, 0)),
        out_shape=jax.ShapeDtypeStruct((m, D_MODEL), F32),
        compiler_params=_cparams(("parallel",)),
        name="out_proj",
    )(x2d, *ys, w_out)


def _ffn_kernel(x_ref, g_ref, wg_ref, wu_ref, wd_ref, o_ref, h_scr):
    @pl.when(pl.program_id(1) == 0)
    def _():
        x = x_ref[...]
        ms = jnp.mean(x * x, axis=-1, keepdims=True)
        h_scr[...] = (x * lax.rsqrt(ms + EPS) * g_ref[...]).astype(BF16)
        o_ref[...] = x

    h = h_scr[...]
    a = _silu(_dot(h, wg_ref[...])) * _dot(h, wu_ref[...])
    o_ref[...] += _dot(a.astype(BF16), wd_ref[...])


def _ffn(x2d, ln, wg, wu, wd, layer, tm, tf):
    m = x2d.shape[0]
    d_ff = wg.shape[2]
    return pl.pallas_call(
        _ffn_kernel,
        grid=(m // tm, d_ff // tf),
        in_specs=[pl.BlockSpec((tm, D_MODEL), lambda i, f: (i, 0)),
                  pl.BlockSpec((1, D_MODEL), lambda i, f: (0, 0)),
                  pl.BlockSpec((None, D_MODEL, tf), lambda i, f: (layer, 0, f)),
                  pl.BlockSpec((None, D_MODEL, tf), lambda i, f: (layer, 0, f)),
                  pl.BlockSpec((None, tf, D_MODEL), lambda i, f: (layer, f, 0))],
        out_specs=pl.BlockSpec((tm, D_MODEL), lambda i, f: (i, 0)),
        out_shape=jax.ShapeDtypeStruct((m, D_MODEL), F32),
        scratch_shapes=[pltpu.VMEM((tm, D_MODEL), BF16)],
        compiler_params=_cparams(("parallel", "arbitrary")),
        name="swiglu_ffn",
    )(x2d, ln, wg, wu, wd)


def _pad_lanes(v, offset=0):
    return jnp.zeros((1, LANES), F32).at[0, offset:offset + v.shape[0]].set(v.astype(F32))


def _ssm_state_to_groups(s):
    nb = s.shape[0]
    hpg = SSM_HEADS // SSM_GROUPS
    s = s.reshape(nb, SSM_GROUPS, hpg, SSM_STATE, SSM_HEAD_DIM)
    return jnp.transpose(s, (0, 1, 3, 2, 4)).reshape(nb, SSM_GROUPS, SSM_STATE, hpg * SSM_HEAD_DIM)


def _ssm_state_from_groups(s):
    nb = s.shape[0]
    hpg = SSM_HEADS // SSM_GROUPS
    s = s.reshape(nb, SSM_GROUPS, SSM_STATE, hpg, SSM_HEAD_DIM)
    return jnp.transpose(s, (0, 1, 3, 2, 4)).reshape(nb, SSM_HEADS, SSM_STATE, SSM_HEAD_DIM)


def _block_rows(t, want):
    return want if t % want == 0 else t


def _layer(x2d, nb, t, state, pr, layer, kv):
    (ssm_conv, ssm_s, hgrn_s, fox_k, fox_v, fox_lf, lru_conv, lru_h) = state
    m = nb * t
    past = 0 if fox_k is None else fox_k.shape[2]
    tm = _block_rows(m, 512)
    tb = _block_rows(t, 256)

    proj, small = _in_proj(x2d, pr['ln1'], pr['w_main'], pr['w_small'], layer, _block_rows(m, 1024))

    ya, ssm_conv_new, ssm_g = _ssd(proj, small, ssm_conv, _ssm_state_to_groups(ssm_s), pr['ssm_conv_w'],
                                   pr['ssm_conv_b'], pr['ssm_dt_bias'], pr['ssm_a_log'], pr['ssm_d'],
                                   pr['ssm_norm'], nb, t, tb)
    ssm_s_new = _ssm_state_from_groups(ssm_g)

    yb, hgrn_s_new = _hgrn(proj, hgrn_s, pr['hgrn_lb_logits'], pr['hgrn_norm'], nb, t, tb, layer)

    yd, lru_conv_new, lru_h_new = _lru(proj, lru_conv, lru_h.reshape(nb, 1, GROUP_W), pr['lru_conv_w'],
                                       pr['lru_conv_b'], pr['lru_w_r'], pr['lru_b_r'], pr['lru_w_i'],
                                       pr['lru_b_i'], pr['lru_lambda'], nb, t, tb, reset_first=(past == 0))

    if past:
        past_lf = jnp.pad(fox_lf.astype(F32), ((0, 0), (0, 0), (0, LANES - FOX_HEADS)))
        lf, cum, cum_past = _gate(small, pr['fox_f_bias'], past_lf, nb, t, t)
        fq, fk, fv, kb, vb = _foxprep(kv, layer, proj, pr['head_sum'], pr['fox_q_norm'], pr['fox_k_norm'], t,
                                      min(t, tm))
        cq = cum.reshape(nb, t, LANES)
        ck_t = jnp.transpose(jnp.concatenate([cum_past, cq], axis=1)[:, :, :FOX_HEADS], (0, 2, 1))
        per_head = lambda c: jnp.transpose(c, (0, 1, 3, 4, 2)).reshape(-1, FOX_HEADS, FOX_HEAD_DIM, past)
        yc = _attention_cached(fq, cq, ck_t, kb, vb, proj, per_head(fox_k), per_head(fox_v), layer, nb, t)
    else:
        lf, cum = _gate(small, pr['fox_f_bias'], None, nb, t, _block_rows(t, 512))
        fk, fv, qa, ka, va = _foxprep_aug(kv, layer, proj, cum, pr['head_sum'], pr['fox_q_norm'],
                                          pr['fox_k_norm'], pr['aug'], t, min(t, tm))
        yc = _attention_aug(qa, ka, va, proj, nb, t, _block_rows(t, 512), FOX_HEADS)
    lf = lf.reshape(nb, t, LANES)

    x1 = _out_proj(x2d, (ya, yb, yc, yd), pr['w_out'], layer, tm)
    x2 = _ffn(x1, pr['ln2'], pr['w_gate'], pr['w_up'], pr['w_down'], layer, _block_rows(m, 1024), 512)

    new_state = (ssm_conv_new, ssm_s_new, hgrn_s_new, lf[:, :, :FOX_HEADS], lru_conv_new,
                 lru_h_new.reshape(nb, GROUP_W))
    return x2, new_state, (fk, fv)


def _prep_layer_params(l, ln1, ln2, ssm_conv_w, ssm_conv_b, ssm_dt_bias, ssm_a_log, ssm_d,
                       ssm_norm, hgrn_lb_logits, hgrn_norm, fox_q_norm, fox_k_norm, fox_f_bias, lru_conv_w,
                       lru_conv_b, lru_w_r, lru_b_r, lru_w_i, lru_b_i, lru_lambda):
    hid = jnp.arange(GROUP_W) // FOX_HEAD_DIM
    row = lambda v: v.astype(F32).reshape(1, -1)
    return dict(
        ln1=row(ln1[l]), ln2=row(ln2[l]),
        ssm_conv_w=ssm_conv_w[l], ssm_conv_b=row(ssm_conv_b[l]),
        ssm_dt_bias=_pad_lanes(ssm_dt_bias[l]), ssm_a_log=_pad_lanes(ssm_a_log[l]),
        ssm_d=row(jnp.repeat(ssm_d[l], SSM_HEAD_DIM)), ssm_norm=row(ssm_norm[l]),
        hgrn_lb_logits=hgrn_lb_logits.astype(F32), hgrn_norm=row(hgrn_norm[l]),
        head_sum=(hid[:, None] == hid[None, :]).astype(BF16), aug=_aug_constants(),
        fox_q_norm=row(jnp.tile(fox_q_norm[l], FOX_HEADS)), fox_k_norm=row(jnp.tile(fox_k_norm[l], FOX_HEADS)),
        fox_f_bias=_pad_lanes(fox_f_bias[l]),
        lru_conv_w=lru_conv_w[l], lru_conv_b=row(lru_conv_b[l]),
        lru_w_r=lru_w_r[l].astype(BF16), lru_b_r=row(lru_b_r[l]),
        lru_w_i=lru_w_i[l].astype(BF16), lru_b_i=row(lru_b_i[l]), lru_lambda=row(lru_lambda[l]))


def _prep_dense_weights(w_in, w_out, w_gate, w_up, w_down):
    depth = w_in.shape[0]
    o_dt = GROUP_W + SSM_CONV_DIM
    o_cf = o_dt + SSM_HEADS + 7 * GROUP_W
    w_t = jnp.transpose(w_in, (0, 2, 1))
    w_main = jnp.concatenate([w_t[:, :o_dt], w_t[:, o_dt + SSM_HEADS:o_cf], w_t[:, o_cf + FOX_HEADS:]], axis=1)
    w_small = jnp.zeros((depth, 2 * LANES, D_MODEL), F32)
    w_small = w_small.at[:, :SSM_HEADS].set(w_t[:, o_dt:o_dt + SSM_HEADS])
    w_small = w_small.at[:, LANES:LANES + FOX_HEADS].set(w_t[:, o_cf:o_cf + FOX_HEADS])
    return dict(w_main=w_main.astype(BF16), w_small=w_small.astype(BF16), w_out=w_out.astype(BF16),
                w_gate=w_gate.astype(BF16), w_up=w_up.astype(BF16), w_down=w_down.astype(BF16))


def kernel(x_prompt, x_sample, cache_fox_k, cache_fox_v, cache_fox_logf, state_ssm_conv, state_ssm, state_hgrn,
           state_lru_conv, state_lru, ln1, ln2, w_in, w_out, ssm_conv_w, ssm_conv_b, ssm_dt_bias, ssm_a_log,
           ssm_d, ssm_norm, hgrn_lb_logits, hgrn_norm, fox_q_norm, fox_k_norm, fox_f_bias, lru_conv_w,
           lru_conv_b, lru_w_r, lru_b_r, lru_w_i, lru_b_i, lru_lambda, w_gate, w_up, w_down):
    depth = ln1.shape[0]
    bp, tp, _ = x_prompt.shape
    bs, ts, _ = x_sample.shape
    fresh = (jnp.zeros((bp, SSM_CONV - 1, SSM_CONV_DIM), F32),
             jnp.zeros((bp, SSM_HEADS, SSM_STATE, SSM_HEAD_DIM), F32),
             jnp.zeros((bp, HGRN_HEADS, HGRN_HEAD_DIM, HGRN_HEAD_DIM), F32),
             None, None, None,
             jnp.zeros((bp, LRU_CONV - 1, GROUP_W), F32),
             jnp.zeros((bp, GROUP_W), F32))
    yp = x_prompt.reshape(bp * tp, D_MODEL)
    ys = x_sample.reshape(bs * ts, D_MODEL)
    p_states, s_states = [], []
    new_kv = lambda nb, t: tuple(jnp.zeros((depth * nb, FOX_HEADS, FOX_HEAD_DIM, t), F32) for _ in range(2))
    kv_p, kv_s = new_kv(bp, tp), new_kv(bs, ts)
    dense =_prep_dense_weights(w_in, w_out, w_gate, w_up, w_down)
    for l in range(depth):
        pr = _prep_layer_params(l, ln1, ln2, ssm_conv_w, ssm_conv_b, ssm_dt_bias, ssm_a_log, ssm_d,
                                ssm_norm, hgrn_lb_logits, hgrn_norm, fox_q_norm, fox_k_norm, fox_f_bias,
                                lru_conv_w, lru_conv_b, lru_w_r, lru_b_r, lru_w_i, lru_b_i, lru_lambda)
        pr.update(dense)
        yp, st_p, kv_p = _layer(yp, bp, tp, fresh, pr, l, kv_p)
        p_states.append(st_p)
        past = (state_ssm_conv[l], state_ssm[l], state_hgrn[l], cache_fox_k, cache_fox_v,
                cache_fox_logf[l], state_lru_conv[l], state_lru[l])
        ys, st_s, kv_s = _layer(ys, bs, ts, past, pr, l, kv_s)
        s_states.append(st_s)

    def group(states, kv, nb, t):
        conv, ssm, hgrn, lf, lru_conv, lru = [jnp.stack(parts, axis=0) for parts in zip(*states)]
        k, v = [jnp.transpose(a.reshape(depth, nb, FOX_HEADS, FOX_HEAD_DIM, t), (0, 1, 4, 2, 3)) for a in kv]
        return conv, ssm, hgrn, k, v, lf, lru_conv, lru

    return (yp.reshape(bp, tp, D_MODEL), ys.reshape(bs, ts, D_MODEL),
            *group(p_states, kv_p, bp, tp), *group(s_states, kv_s, bs, ts))
```
